```python
import jax, jax.numpy as jnp
from jax import lax
import numpy as np

D_MODEL = 2048
BATCH = 16
SEQ = 2048
DEPTH = 2
DEC_BATCH = 16
DEC_SEQ = 64
PAST_LEN = 2048

CHUNK = 64
HEAD_DIM = 128
A_HEADS = 8
A_KV_HEADS = 2
A_GROUP = A_HEADS // A_KV_HEADS
WINDOW = 128
B_HEADS = 8
B_PAST_CHUNKS = 8
B_BAND_PAST = B_PAST_CHUNKS * CHUNK
REL_CLIP = 128
C_HEADS = 16
C_Q_RANK = 768
C_KV_RANK = 512
C_NOPE = 128
C_ROPE = 64
C_V = 128
Q_BLOCK = 128
D_FF = 4 * D_MODEL
PLE_DIM = 256
ROPE_THETA = 10000.0
LN_EPS = 1e-5
RMS_EPS = 1e-6
NEG_INF = -1e30
DEEPNORM_ALPHA = (2 * DEPTH) ** 0.25
DEEPNORM_BETA = (8 * DEPTH) ** -0.25
N_AB_LAYERS = (DEPTH + 1) // 2
N_C_LAYERS = DEPTH // 2
A_Q_W = A_HEADS * HEAD_DIM
A_KV_W = A_KV_HEADS * HEAD_DIM
B_W = B_HEADS * HEAD_DIM
AB_IN_W = A_Q_W + 2 * A_KV_W + 3 * B_W
AB_MIX_W = A_Q_W + B_W
C_IN_W = C_Q_RANK + C_KV_RANK + C_ROPE

kernel_name = 'chunk_streaming_hybrid_swa_band_mla'


def layer_norm(x, g, b):
    xf = x.astype(jnp.float32)
    mu = jnp.mean(xf, -1, keepdims=True)
    var = jnp.mean(jnp.square(xf - mu), -1, keepdims=True)
    return ((xf - mu) * lax.rsqrt(var + LN_EPS) * g.astype(jnp.float32) + b.astype(jnp.float32)).astype(x.dtype)


def rms_norm(x, g):
    xf = x.astype(jnp.float32)
    return (xf * lax.rsqrt(jnp.mean(jnp.square(xf), -1, keepdims=True) + RMS_EPS) * g.astype(jnp.float32)).astype(x.dtype)


def rope(x, pos0):
    t, d = x.shape[1], x.shape[-1]
    half = d // 2
    inv = ROPE_THETA ** (-jnp.arange(half, dtype=jnp.float32) * (2.0 / d))
    ang = (jnp.arange(t, dtype=jnp.float32) + pos0)[:, None] * inv[None, :]
    cos = jnp.cos(ang)[None, :, None, :]
    sin = jnp.sin(ang)[None, :, None, :]
    xf = x.astype(jnp.float32)
    x1, x2 = xf[..., :half], xf[..., half:]
    return jnp.concatenate([x1 * cos - x2 * sin, x2 * cos + x1 * sin], -1).astype(x.dtype)


def keep_newest(past, new, cap):
    full = jnp.concatenate([past, new], 1)
    return full[:, -min(cap, full.shape[1]):]


def rel_position_bias(table, band_past):
    r = jnp.arange(CHUNK)[:, None]
    m = jnp.arange(band_past + CHUNK)[None, :]
    idx = jnp.clip(band_past + r - m, -REL_CLIP, REL_CLIP) + REL_CLIP
    return table[:, idx][:, None]


def band_attention(q, k_past, v_past, k_new, v_new, band_past, bias=None, sinks=None):
    b, t, nkv, g, d = q.shape
    n_past = k_past.shape[1]
    nc = -(-t // CHUNK)
    tp = nc * CHUNK
    pad_l = band_past - n_past
    pad_p = ((0, 0), (pad_l, 0), (0, 0), (0, 0))
    pad_n = ((0, 0), (0, tp - t), (0, 0), (0, 0))
    k_all = jnp.concatenate([jnp.pad(k_past, pad_p), jnp.pad(k_new, pad_n)], 1)
    v_all = jnp.concatenate([jnp.pad(v_past, pad_p), jnp.pad(v_new, pad_n)], 1)
    j = jnp.arange(band_past + tp)
    valid = (j >= pad_l) & (j < band_past + t)
    q_blocks = jnp.pad(q, ((0, 0), (0, tp - t), (0, 0), (0, 0), (0, 0))).reshape(b, nc, CHUNK, nkv, g, d).swapaxes(0, 1)
    lb = band_past + CHUNK
    scale = d ** -0.5

    def one_chunk(args):
        c, qc = args
        start = c * CHUNK
        kb = lax.dynamic_slice_in_dim(k_all, start, lb, axis=1)
        vb = lax.dynamic_slice_in_dim(v_all, start, lb, axis=1)
        vm = lax.dynamic_slice_in_dim(valid, start, lb)
        s = jnp.einsum('bckgd,blkd->bkgcl', qc, kb).astype(jnp.float32) * scale
        if bias is not None:
            s = s + bias.astype(jnp.float32)
        s = jnp.where(vm, s, NEG_INF)
        m = jnp.max(s, -1, keepdims=True)
        if sinks is not None:
            sk = sinks.astype(jnp.float32)[None, :, :, None, None]
            m = jnp.maximum(m, sk)
        e = jnp.exp(s - m)
        den = jnp.sum(e, -1, keepdims=True)
        if sinks is not None:
            den = den + jnp.exp(sk - m)
        return jnp.einsum('bkgcl,blkd->bckgd', (e / den).astype(vb.dtype), vb)

    out = lax.map(one_chunk, (jnp.arange(nc), q_blocks))
    return out.swapaxes(0, 1).reshape(b, tp, nkv * g * d)[:, :t]


def chunk_causal_attention(q, k, v, q_pos0):
    b, t, h, dq = q.shape
    s_len = k.shape[1]
    blk = min(Q_BLOCK, t)
    nb = -(-t // blk)
    tp = nb * blk
    q_blocks = jnp.pad(q, ((0, 0), (0, tp - t), (0, 0), (0, 0))).reshape(b, nb, blk, h, dq).swapaxes(0, 1)
    q_chunk = ((q_pos0 + jnp.arange(tp)) // CHUNK).reshape(nb, blk)
    k_chunk = jnp.arange(s_len) // CHUNK
    scale = dq ** -0.5

    def one_block(args):
        qb, qc = args
        s = jnp.einsum('bqhd,bkhd->bhqk', qb, k).astype(jnp.float32) * scale
        s = jnp.where(k_chunk[None, :] <= qc[:, None], s, NEG_INF)
        p = jax.nn.softmax(s, axis=-1).astype(v.dtype)
        return jnp.einsum('bhqk,bkhd->bqhd', p, v)

    out = lax.map(one_block, (q_blocks, q_chunk))
    return out.swapaxes(0, 1).reshape(b, tp, h * v.shape[-1])[:, :t]


def mixer_ab(x, pos0, past_ak, past_av, past_bk, past_bv, w_in, sinks, rel_bias, w_out):
    b, t, _ = x.shape
    h = x @ w_in
    o1 = A_Q_W
    o2 = o1 + A_KV_W
    o3 = o2 + A_KV_W
    o4 = o3 + B_W
    o5 = o4 + B_W
    qa = rope(h[..., :o1].reshape(b, t, A_HEADS, HEAD_DIM), pos0).reshape(b, t, A_KV_HEADS, A_GROUP, HEAD_DIM)
    ka = rope(h[..., o1:o2].reshape(b, t, A_KV_HEADS, HEAD_DIM), pos0)
    va = h[..., o2:o3].reshape(b, t, A_KV_HEADS, HEAD_DIM)
    qb = h[..., o3:o4].reshape(b, t, B_HEADS, 1, HEAD_DIM)
    kb = h[..., o4:o5].reshape(b, t, B_HEADS, HEAD_DIM)
    vb = h[..., o5:].reshape(b, t, B_HEADS, HEAD_DIM)
    out_a = band_attention(qa, past_ak, past_av, ka, va, WINDOW, sinks=sinks.reshape(A_KV_HEADS, A_GROUP))
    out_b = band_attention(qb, past_bk, past_bv, kb, vb, B_BAND_PAST, bias=rel_position_bias(rel_bias, B_BAND_PAST))
    out = jnp.concatenate([out_a, out_b], -1) @ w_out
    states = (keep_newest(past_ak, ka, WINDOW), keep_newest(past_av, va, WINDOW),
              keep_newest(past_bk, kb, B_BAND_PAST), keep_newest(past_bv, vb, B_BAND_PAST))
    return out, states


def mixer_c(x, pos0, past_ckv, past_kr, w_in, g_q, w_q_b, g_kv, w_kv_b, w_out):
    b, t, _ = x.shape
    h = x @ w_in
    cq = rms_norm(h[..., :C_Q_RANK], g_q)
    ckv = rms_norm(h[..., C_Q_RANK:C_Q_RANK + C_KV_RANK], g_kv)
    kr = rope(h[..., C_Q_RANK + C_KV_RANK:][:, :, None, :], pos0)[:, :, 0]
    q = (cq @ w_q_b).reshape(b, t, C_HEADS, C_NOPE + C_ROPE)
    q = jnp.concatenate([q[..., :C_NOPE], rope(q[..., C_NOPE:], pos0)], -1)
    ckv_all = jnp.concatenate([past_ckv, ckv], 1)
    kr_all = jnp.concatenate([past_kr, kr], 1)
    s_len = ckv_all.shape[1]
    kv = (ckv_all @ w_kv_b).reshape(b, s_len, C_HEADS, C_NOPE + C_V)
    k = jnp.concatenate([kv[..., :C_NOPE], jnp.broadcast_to(kr_all[:, :, None, :], (b, s_len, C_HEADS, C_ROPE))], -1)
    o = chunk_causal_attention(q, k, kv[..., C_NOPE:], pos0)
    return o @ w_out, (ckv, kr)


def sq_relu_mlp(x, w_up, w_down):
    return jnp.square(jax.nn.relu(x @ w_up)) @ w_down


def run_trunk(x, p, pos0, past_a_k, past_a_v, past_b_k, past_b_v, past_c_kv, past_c_kr, w):
    a_k, a_v, b_k, b_v, c_kv, c_kr = [], [], [], [], [], []
    for i in range(DEPTH):
        j = i // 2
        if i % 2 == 0:
            mix, (ak, av, bk, bv) = mixer_ab(x, pos0, past_a_k[j], past_a_v[j], past_b_k[j], past_b_v[j],
                                             w['w_in_ab'][j], w['sinks_a'][j], w['rel_bias_b'][j], w['w_out_ab'][j])
            a_k.append(ak)
            a_v.append(av)
            b_k.append(bk)
            b_v.append(bv)
        else:
            mix, (ckv, ckr) = mixer_c(x, pos0, past_c_kv[j], past_c_kr[j], w['w_in_c'][j], w['g_q_c'][j],
                                      w['w_q_b_c'][j], w['g_kv_c'][j], w['w_kv_b_c'][j], w['w_out_c'][j])
            c_kv.append(ckv)
            c_kr.append(ckr)
        x = layer_norm(DEEPNORM_ALPHA * x + mix, w['ln1_g'][i], w['ln1_b'][i])
        x = layer_norm(DEEPNORM_ALPHA * x + sq_relu_mlp(x, w['w_mlp_up'][i], w['w_mlp_down'][i]),
                       w['ln2_g'][i], w['ln2_b'][i])
        gate = jax.nn.sigmoid((x @ w['w_ple_gate'][i] + w['b_ple_gate'][i]).astype(jnp.float32)).astype(x.dtype)
        x = x + gate * (p[i] @ w['w_ple'][i])
    return x, jnp.stack(a_k), jnp.stack(a_v), jnp.stack(b_k), jnp.stack(b_v), jnp.stack(c_kv), jnp.stack(c_kr)


def setup_inputs(seed: int = 0) -> dict:
    key = jax.random.key(seed)
    ks = jax.random.split(key, 32)

    def nrm(k, shape, scale=1.0):
        return jax.random.normal(k, shape, jnp.float32) * scale

    a_cache = min(WINDOW, PAST_LEN)
    b_cache = min(B_BAND_PAST, PAST_LEN)
    return {
        'x_prompt': nrm(ks[0], (BATCH, SEQ, D_MODEL)),
        'x_sample': nrm(ks[1], (DEC_BATCH, DEC_SEQ, D_MODEL)),
        'cache_a_k': nrm(ks[2], (N_AB_LAYERS, DEC_BATCH, a_cache, A_KV_HEADS, HEAD_DIM)),
        'cache_a_v': nrm(ks[3], (N_AB_LAYERS, DEC_BATCH, a_cache, A_KV_HEADS, HEAD_DIM)),
        'cache_b_k': nrm(ks[4], (N_AB_LAYERS, DEC_BATCH, b_cache, B_HEADS, HEAD_DIM)),
        'cache_b_v': nrm(ks[5], (N_AB_LAYERS, DEC_BATCH, b_cache, B_HEADS, HEAD_DIM)),
        'cache_c_kv': nrm(ks[6], (N_C_LAYERS, DEC_BATCH, PAST_LEN, C_KV_RANK)),
        'cache_c_krope': nrm(ks[7], (N_C_LAYERS, DEC_BATCH, PAST_LEN, C_ROPE)),
        'p_prompt': nrm(ks[8], (DEPTH, BATCH, SEQ, PLE_DIM)),
        'p_sample': nrm(ks[9], (DEPTH, DEC_BATCH, DEC_SEQ, PLE_DIM)),
        'w_in_ab': nrm(ks[10], (N_AB_LAYERS, D_MODEL, AB_IN_W), D_MODEL ** -0.5),
        'sinks_a': nrm(ks[11], (N_AB_LAYERS, A_HEADS), 0.5),
        'rel_bias_b': nrm(ks[12], (N_AB_LAYERS, B_HEADS, 2 * REL_CLIP + 1), 0.1),
        'w_out_ab': nrm(ks[13], (N_AB_LAYERS, AB_MIX_W, D_MODEL), DEEPNORM_BETA * AB_MIX_W ** -0.5),
        'w_in_c': nrm(ks[14], (N_C_LAYERS, D_MODEL, C_IN_W), D_MODEL ** -0.5),
        'g_q_c': 1.0 + nrm(ks[15], (N_C_LAYERS, C_Q_RANK), 0.02),
        'w_q_b_c': nrm(ks[16], (N_C_LAYERS, C_Q_RANK, C_HEADS * (C_NOPE + C_ROPE)), C_Q_RANK ** -0.5),
        'g_kv_c': 1.0 + nrm(ks[17], (N_C_LAYERS, C_KV_RANK), 0.02),
        'w_kv_b_c': nrm(ks[18], (N_C_LAYERS, C_KV_RANK, C_HEADS * (C_NOPE + C_V)), C_KV_RANK ** -0.5),
        'w_out_c': nrm(ks[19], (N_C_LAYERS, C_HEADS * C_V, D_MODEL), DEEPNORM_BETA * (C_HEADS * C_V) ** -0.5),
        'ln1_g': 1.0 + nrm(ks[20], (DEPTH, D_MODEL), 0.02),
        'ln1_b': nrm(ks[21], (DEPTH, D_MODEL), 0.02),
        'ln2_g': 1.0 + nrm(ks[22], (DEPTH, D_MODEL), 0.02),
        'ln2_b': nrm(ks[23], (DEPTH, D_MODEL), 0.02),
        'w_mlp_up': nrm(ks[24], (DEPTH, D_MODEL, D_FF), D_MODEL ** -0.5),
        'w_mlp_down': nrm(ks[25], (DEPTH, D_FF, D_MODEL), DEEPNORM_BETA * D_FF ** -0.5),
        'w_ple_gate': nrm(ks[26], (DEPTH, D_MODEL, D_MODEL), D_MODEL ** -0.5),
        'b_ple_gate': nrm(ks[27], (DEPTH, D_MODEL), 0.02),
        'w_ple': nrm(ks[28], (DEPTH, PLE_DIM, D_MODEL), PLE_DIM ** -0.5),
    }


def reference(x_prompt, x_sample, cache_a_k, cache_a_v, cache_b_k, cache_b_v, cache_c_kv, cache_c_krope,
              p_prompt, p_sample, w_in_ab, sinks_a, rel_bias_b, w_out_ab, w_in_c, g_q_c, w_q_b_c, g_kv_c,
              w_kv_b_c, w_out_c, ln1_g, ln1_b, ln2_g, ln2_b, w_mlp_up, w_mlp_down, w_ple_gate, b_ple_gate, w_ple):
    w = {
        'w_in_ab': w_in_ab, 'sinks_a': sinks_a, 'rel_bias_b': rel_bias_b, 'w_out_ab': w_out_ab,
        'w_in_c': w_in_c, 'g_q_c': g_q_c, 'w_q_b_c': w_q_b_c, 'g_kv_c': g_kv_c, 'w_kv_b_c': w_kv_b_c,
        'w_out_c': w_out_c, 'ln1_g': ln1_g, 'ln1_b': ln1_b, 'ln2_g': ln2_g, 'ln2_b': ln2_b,
        'w_mlp_up': w_mlp_up, 'w_mlp_down': w_mlp_down, 'w_ple_gate': w_ple_gate, 'b_ple_gate': b_ple_gate,
        'w_ple': w_ple,
    }
    bp = x_prompt.shape[0]
    dt = x_prompt.dtype
    empty_a = jnp.zeros((N_AB_LAYERS, bp, 0, A_KV_HEADS, HEAD_DIM), dt)
    empty_b = jnp.zeros((N_AB_LAYERS, bp, 0, B_HEADS, HEAD_DIM), dt)
    empty_c = jnp.zeros((N_C_LAYERS, bp, 0, C_KV_RANK), dt)
    empty_r = jnp.zeros((N_C_LAYERS, bp, 0, C_ROPE), dt)
    y_prompt, pa_k, pa_v, pb_k, pb_v, pc_kv, pc_kr = run_trunk(
        x_prompt, p_prompt, 0, empty_a, empty_a, empty_b, empty_b, empty_c, empty_r, w)
    y_sample, sa_k, sa_v, sb_k, sb_v, sc_kv, sc_kr = run_trunk(
        x_sample, p_sample, PAST_LEN, cache_a_k, cache_a_v, cache_b_k, cache_b_v, cache_c_kv, cache_c_krope, w)
    return (y_prompt, y_sample, pa_k, pa_v, pb_k, pb_v, pc_kv, pc_kr, sa_k, sa_v, sb_k, sb_v, sc_kv, sc_kr)
```

```python
import functools

import jax
import jax.numpy as jnp
from jax import lax
from jax.experimental import pallas as pl
from jax.experimental.pallas import tpu as pltpu

F32 = jnp.float32
BF16 = jnp.bfloat16

CHUNK = 64
HEAD_DIM = 128
A_HEADS = 8
A_KV_HEADS = 2
WINDOW = 128
B_HEADS = 8
B_BAND_PAST = 512
REL_CLIP = 128
C_HEADS = 16
C_Q_RANK = 768
C_KV_RANK = 512
C_NOPE = 128
C_ROPE = 64
C_V = 128
DEPTH = 2
ROPE_THETA = 10000.0
LN_EPS = 1e-5
RMS_EPS = 1e-6
NEG_INF = -1e30
DEEPNORM_ALPHA = (2 * DEPTH) ** 0.25

A_Q_W = A_HEADS * HEAD_DIM
A_KV_W = A_KV_HEADS * HEAD_DIM
B_W = B_HEADS * HEAD_DIM
AB_IN_W = A_Q_W + 2 * A_KV_W + 3 * B_W
AB_KV_COL0 = A_Q_W + B_W
AB_KV_W = AB_IN_W - AB_KV_COL0
C_IN_W = C_Q_RANK + C_KV_RANK + C_ROPE
C_QK = 256

LANES = 128
V7X_VMEM_BYTES = 64 * 1024 * 1024
VMEM_LIMIT = V7X_VMEM_BYTES - 8 * 1024 * 1024

ROW_TILE = 512
COL_CHUNK = 512


def _params(n_axes):
    return pltpu.CompilerParams(dimension_semantics=("arbitrary",) * n_axes, vmem_limit_bytes=VMEM_LIMIT)


def _resident(shape):
    nd = len(shape)
    return pl.BlockSpec(shape, lambda *_: (0,) * nd, pipeline_mode=pl.Buffered(1))


def _row_tile(n):
    return ROW_TILE if n % ROW_TILE == 0 else n


def _dot(a, b):
    return jnp.dot(a, b, preferred_element_type=F32)


def _dot_t(a, b):
    return lax.dot_general(a, b, (((1,), (1,)), ((), ())), preferred_element_type=F32)


def _layer_norm(y, g, b):
    mu = jnp.mean(y, -1, keepdims=True)
    var = jnp.mean(jnp.square(y - mu), -1, keepdims=True)
    return (y - mu) * lax.rsqrt(var + LN_EPS) * g + b


def _rms_norm(y, g):
    return y * lax.rsqrt(jnp.mean(jnp.square(y), -1, keepdims=True) + RMS_EPS) * g


def _rope_tile(t, cos, sin, d):
    if d == LANES:
        swapped = pltpu.roll(t, LANES // 2, 1)
    else:
        lane = lax.broadcasted_iota(jnp.int32, t.shape, 1)
        swapped = jnp.where((lane % d) < d // 2, pltpu.roll(t, LANES - d // 2, 1), pltpu.roll(t, d // 2, 1))
    return t * cos + swapped * sin


def _rope_tables(t, pos0, d, rows):
    half = d // 2
    inv = ROPE_THETA ** (-jnp.arange(half, dtype=F32) * (2.0 / d))
    ang = (jnp.arange(t, dtype=F32) + pos0)[:, None] * inv[None, :]
    cos = jnp.cos(ang)
    sin = jnp.sin(ang)
    reps = (rows // t, LANES // d)
    return jnp.tile(jnp.concatenate([cos, cos], 1), reps), jnp.tile(jnp.concatenate([-sin, sin], 1), reps)


def _proj_ab_kernel(x_ref, w_ref, cos_ref, sin_ref, h_ref, kv_ref, *, kv_period):
    xb = x_ref[...].astype(BF16)
    cos = cos_ref[...]
    sin = sin_ref[...]
    keep_state = (pl.program_id(0) % kv_period) == kv_period - 1
    rope_tiles = set(range(A_HEADS)) | {(AB_IN_W - 2 * A_KV_W) // LANES + u for u in range(A_KV_HEADS)}
    for j in range(AB_IN_W // COL_CHUNK):
        c0 = j * COL_CHUNK
        acc = _dot(xb, w_ref[:, c0:c0 + COL_CHUNK])
        parts = []
        for u in range(COL_CHUNK // LANES):
            part = acc[:, u * LANES:(u + 1) * LANES]
            if c0 // LANES + u in rope_tiles:
                part = _rope_tile(part, cos, sin, HEAD_DIM)
            parts.append(part)
        acc = jnp.concatenate(parts, axis=1)
        h_ref[:, c0:c0 + COL_CHUNK] = acc.astype(h_ref.dtype)
        if c0 >= AB_KV_COL0:
            @pl.when(keep_state)
            def _(acc=acc, c0=c0):
                kv_ref[:, c0 - AB_KV_COL0:c0 - AB_KV_COL0 + COL_CHUNK] = acc


def _proj_ab(xf, w, seq, pos0):
    n, d = xf.shape
    tm = _row_tile(n)
    assert seq % tm == 0 or tm % seq == 0
    kv_period = max(seq // tm, 1)
    assert min(seq, B_BAND_PAST) == min(seq, tm)
    cos, sin = _rope_tables(seq, pos0, HEAD_DIM, max(seq, tm))
    n_tab = cos.shape[0] // tm
    return pl.pallas_call(
        functools.partial(_proj_ab_kernel, kv_period=kv_period),
        grid=(n // tm,),
        in_specs=[
            pl.BlockSpec((tm, d), lambda i: (i, 0)),
            _resident(w.shape),
            pl.BlockSpec((tm, LANES), lambda i: (i % n_tab, 0)),
            pl.BlockSpec((tm, LANES), lambda i: (i % n_tab, 0)),
        ],
        out_specs=[
            pl.BlockSpec((tm, AB_IN_W), lambda i: (i, 0)),
            pl.BlockSpec((tm, AB_KV_W), lambda i: (i // kv_period, 0)),
        ],
        out_shape=[
            jax.ShapeDtypeStruct((n, AB_IN_W), BF16),
            jax.ShapeDtypeStruct((n // kv_period, AB_KV_W), F32),
        ],
        compiler_params=_params(1),
        name="proj_ab",
    )(xf, w, cos, sin)


def _rel_bias_kernel(tab_ref, o_ref, *, band):
    h = pl.program_id(0)
    shape = o_ref.shape[1:]
    r = lax.broadcasted_iota(jnp.int32, shape, 0)
    w = lax.broadcasted_iota(jnp.int32, shape, 1)
    idx = jnp.clip(band + r - w, -REL_CLIP, REL_CLIP) + REL_CLIP

    def body(d, acc):
        return jnp.where(idx == d, tab_ref[h, d], acc)

    o_ref[0] = lax.fori_loop(0, 2 * REL_CLIP + 1, body, jnp.zeros(shape, F32))


def _rel_bias(table, band, tq):
    heads = table.shape[0]
    return pl.pallas_call(
        functools.partial(_rel_bias_kernel, band=band),
        grid=(heads,),
        in_specs=[pl.BlockSpec(memory_space=pltpu.SMEM)],
        out_specs=pl.BlockSpec((1, tq, band + tq), lambda h: (h, 0, 0)),
        out_shape=jax.ShapeDtypeStruct((heads, tq, band + tq), F32),
        compiler_params=_params(1),
        name="rel_bias",
    )(table)


def _band_attn_kernel(*refs, heads, kv_heads, band, tq, past, has_bias, has_sinks):
    q_ref, k_ref, v_ref = refs[:3]
    rest = list(refs[3:])
    bias_ref = rest.pop(0) if has_bias else None
    sink_ref = rest.pop(0) if has_sinks else None
    o_ref = rest.pop(0)

    width = band + tq
    scale = HEAD_DIM ** -0.5
    ws = past + pl.program_id(1) * tq - band
    r = lax.broadcasted_iota(jnp.int32, (tq, width), 0) // CHUNK
    w = lax.broadcasted_iota(jnp.int32, (tq, width), 1)
    wc = w // CHUNK
    allowed = (wc >= r) & (wc <= r + band // CHUNK) & (w + ws >= 0)

    def window(ref, cols):
        if past >= band:
            return ref[0, pl.ds(pl.multiple_of(ws, CHUNK), width), cols]
        pieces = [ref[0, pl.ds(pl.multiple_of(jnp.maximum(ws + c * LANES, 0), LANES), LANES), cols]
                  for c in range(width // LANES)]
        return jnp.concatenate(pieces, axis=0)

    group = heads // kv_heads
    for kh in range(kv_heads):
        cols = slice(kh * HEAD_DIM, (kh + 1) * HEAD_DIM)
        k_w = window(k_ref, cols)
        v_w = window(v_ref, cols)
        for g in range(group):
            h = kh * group + g
            hcols = slice(h * HEAD_DIM, (h + 1) * HEAD_DIM)
            s = _dot_t(q_ref[0, :, hcols], k_w) * scale
            if has_bias:
                s = s + bias_ref[h]
            s = jnp.where(allowed, s, NEG_INF)
            m = jnp.max(s, -1, keepdims=True)
            if has_sinks:
                sink = sink_ref[h]
                m = jnp.maximum(m, sink)
            e = jnp.exp(s - m)
            den = jnp.sum(e, -1, keepdims=True)
            if has_sinks:
                den = den + jnp.exp(sink - m)
            p = (e * (1.0 / den)).astype(BF16)
            o_ref[0, :, hcols] = _dot(p, v_w).astype(o_ref.dtype)


def _band_attn(q, k, v, *, heads, kv_heads, band, tq, past, bias=None, sinks=None):
    (qa, qw, qi), (ka, kw, ki), (va, vw, vi) = q, k, v
    b, t, _ = qa.shape
    s_len = ka.shape[1]
    assert t % tq == 0 and s_len == past + t and qw == heads * HEAD_DIM and kw == kv_heads * HEAD_DIM
    assert past >= band or (past == 0 and tq % LANES == 0 and band % LANES == 0)
    in_specs = [
        pl.BlockSpec((1, tq, qw), lambda bi, i: (bi, i, qi)),
        pl.BlockSpec((1, s_len, kw), lambda bi, i: (bi, 0, ki)),
        pl.BlockSpec((1, s_len, vw), lambda bi, i: (bi, 0, vi)),
    ]
    args = [qa, ka, va]
    if bias is not None:
        in_specs.append(_resident(bias.shape))
        args.append(bias)
    if sinks is not None:
        in_specs.append(pl.BlockSpec(memory_space=pltpu.SMEM))
        args.append(sinks)
    return pl.pallas_call(
        functools.partial(_band_attn_kernel, heads=heads, kv_heads=kv_heads, band=band, tq=tq, past=past,
                          has_bias=bias is not None, has_sinks=sinks is not None),
        grid=(b, t // tq),
        in_specs=in_specs,
        out_specs=pl.BlockSpec((1, tq, qw), lambda bi, i: (bi, i, 0)),
        out_shape=jax.ShapeDtypeStruct((b, t, qw), BF16),
        compiler_params=_params(2),
        name="band_attn",
    )(*args)


def _outproj_ln_kernel(*refs, n_in):
    a_refs = refs[:n_in]
    w_ref, x_ref, g_ref, b_ref, o_ref = refs[n_in:]
    d_out = o_ref.shape[1]
    for j in range(d_out // COL_CHUNK):
        cols = slice(j * COL_CHUNK, (j + 1) * COL_CHUNK)
        y = DEEPNORM_ALPHA * x_ref[:, cols]
        r0 = 0
        for a_ref in a_refs:
            kk = a_ref.shape[1]
            y = y + _dot(a_ref[...], w_ref[r0:r0 + kk, cols])
            r0 += kk
        o_ref[:, cols] = y
    o_ref[...] = _layer_norm(o_ref[...], g_ref[...], b_ref[...])


def _outproj_ln(a_list, w, xf, g, b):
    n, d = xf.shape
    tm = _row_tile(n)
    assert sum(a.shape[1] for a in a_list) == w.shape[0]
    return pl.pallas_call(
        functools.partial(_outproj_ln_kernel, n_in=len(a_list)),
        grid=(n // tm,),
        in_specs=[pl.BlockSpec((tm, a.shape[1]), lambda i: (i, 0)) for a in a_list] + [
            _resident(w.shape),
            pl.BlockSpec((tm, d), lambda i: (i, 0)),
            _resident(g.shape),
            _resident(b.shape),
        ],
        out_specs=pl.BlockSpec((tm, d), lambda i: (i, 0)),
        out_shape=jax.ShapeDtypeStruct((n, d), F32),
        compiler_params=_params(1),
        name="outproj_ln",
    )(*a_list, w, xf, g, b)


def _mlp_ln_kernel(x_ref, wu_ref, wd_ref, g_ref, b_ref, o_ref, xb_ref):
    f = pl.program_id(1)

    @pl.when(f == 0)
    def _():
        xb_ref[...] = x_ref[...].astype(BF16)

    hid = _dot(xb_ref[...], wu_ref[...])
    hid = jnp.square(jnp.maximum(hid, 0.0)).astype(BF16)
    part = _dot(hid, wd_ref[...])

    @pl.when(f == 0)
    def _():
        o_ref[...] = part

    @pl.when(f > 0)
    def _():
        o_ref[...] += part

    @pl.when(f == pl.num_programs(1) - 1)
    def _():
        o_ref[...] = _layer_norm(DEEPNORM_ALPHA * x_ref[...] + o_ref[...], g_ref[...], b_ref[...])


def _mlp_ln(xf, w_up, w_down, g, b, tf=1024):
    n, d = xf.shape
    d_ff = w_up.shape[1]
    tm = _row_tile(n)
    return pl.pallas_call(
        _mlp_ln_kernel,
        grid=(n // tm, d_ff // tf),
        in_specs=[
            pl.BlockSpec((tm, d), lambda i, f: (i, 0)),
            pl.BlockSpec((d, tf), lambda i, f: (0, f)),
            pl.BlockSpec((tf, d), lambda i, f: (f, 0)),
            _resident(g.shape),
            _resident(b.shape),
        ],
        out_specs=pl.BlockSpec((tm, d), lambda i, f: (i, 0)),
        out_shape=jax.ShapeDtypeStruct((n, d), F32),
        scratch_shapes=[pltpu.VMEM((tm, d), BF16)],
        compiler_params=_params(2),
        name="mlp_ln",
    )(xf, w_up, w_down, g, b)


def _ple_kernel(x_ref, p_ref, wg_ref, bg_ref, wp_ref, o_ref):
    xb = x_ref[...].astype(BF16)
    pb = p_ref[...].astype(BF16)
    for j in range(o_ref.shape[1] // COL_CHUNK):
        cols = slice(j * COL_CHUNK, (j + 1) * COL_CHUNK)
        gate = jax.nn.sigmoid(_dot(xb, wg_ref[:, cols]) + bg_ref[:, cols])
        o_ref[:, cols] = x_ref[:, cols] + gate * _dot(pb, wp_ref[:, cols])


def _ple(xf, pf, wg, bg, wp):
    n, d = xf.shape
    tm = _row_tile(n)
    return pl.pallas_call(
        _ple_kernel,
        grid=(n // tm,),
        in_specs=[
            pl.BlockSpec((tm, d), lambda i: (i, 0)),
            pl.BlockSpec((tm, pf.shape[1]), lambda i: (i, 0)),
            _resident(wg.shape),
            _resident(bg.shape),
            _resident(wp.shape),
        ],
        out_specs=pl.BlockSpec((tm, d), lambda i: (i, 0)),
        out_shape=jax.ShapeDtypeStruct((n, d), F32),
        compiler_params=_params(1),
        name="ple",
    )(xf, pf, wg, bg, wp)


def _proj_c_kernel(x_ref, w_ref, gq_ref, gkv_ref, cos_ref, sin_ref, cq_ref, ckv_ref, kr_ref, krp_ref):
    h = _dot(x_ref[...].astype(BF16), w_ref[...])
    cq_ref[...] = _rms_norm(h[:, :C_Q_RANK], gq_ref[...]).astype(cq_ref.dtype)
    ckv_ref[...] = _rms_norm(h[:, C_Q_RANK:C_Q_RANK + C_KV_RANK], gkv_ref[...])
    t = h[:, C_Q_RANK + C_KV_RANK:]
    rot = _rope_tile(t, cos_ref[...], sin_ref[...], C_ROPE)
    lane = lax.broadcasted_iota(jnp.int32, rot.shape, 1)
    rot = jnp.where(lane < C_ROPE, rot, 0.0)
    kr_ref[...] = rot[:, :C_ROPE]
    krp_ref[...] = rot.astype(krp_ref.dtype)


def _proj_c(xf, w, gq, gkv, cos, sin):
    n, d = xf.shape
    tm = _row_tile(n)
    n_tab = cos.shape[0] // tm
    rows = lambda i: (i, 0)
    return pl.pallas_call(
        _proj_c_kernel,
        grid=(n // tm,),
        in_specs=[
            pl.BlockSpec((tm, d), rows),
            _resident(w.shape),
            _resident(gq.shape),
            _resident(gkv.shape),
            pl.BlockSpec((tm, LANES), lambda i: (i % n_tab, 0)),
            pl.BlockSpec((tm, LANES), lambda i: (i % n_tab, 0)),
        ],
        out_specs=[
            pl.BlockSpec((tm, C_Q_RANK), rows),
            pl.BlockSpec((tm, C_KV_RANK), rows),
            pl.BlockSpec((tm, C_ROPE), rows),
            pl.BlockSpec((tm, LANES), rows),
        ],
        out_shape=[
            jax.ShapeDtypeStruct((n, C_Q_RANK), BF16),
            jax.ShapeDtypeStruct((n, C_KV_RANK), F32),
            jax.ShapeDtypeStruct((n, C_ROPE), F32),
            jax.ShapeDtypeStruct((n, LANES), BF16),
        ],
        compiler_params=_params(1),
        name="proj_c",
    )(xf, w, gq, gkv, cos, sin)


def _q_c_kernel(cq_ref, w_ref, cos_ref, sin_ref, q_ref):
    cq = cq_ref[...]
    cos = cos_ref[...]
    sin = sin_ref[...]
    pair_w = 2 * C_NOPE + 2 * C_ROPE
    for p in range(C_HEADS // 2):
        acc = _dot(cq, w_ref[:, p * pair_w:(p + 1) * pair_w])
        rot = _rope_tile(acc[:, 2 * C_NOPE:], cos, sin, C_ROPE)
        lane = lax.broadcasted_iota(jnp.int32, rot.shape, 1)
        o0 = 2 * p * C_QK
        q_ref[:, o0:o0 + C_NOPE] = acc[:, :C_NOPE].astype(q_ref.dtype)
        q_ref[:, o0 + C_NOPE:o0 + C_QK] = jnp.where(lane < C_ROPE, rot, 0.0).astype(q_ref.dtype)
        q_ref[:, o0 + C_QK:o0 + C_QK + C_NOPE] = acc[:, C_NOPE:2 * C_NOPE].astype(q_ref.dtype)
        q_ref[:, o0 + C_QK + C_NOPE:o0 + 2 * C_QK] = jnp.where(
            lane < C_ROPE, pltpu.roll(rot, C_ROPE, 1), 0.0).astype(q_ref.dtype)


def _q_c(cq, w, cos, sin):
    n = cq.shape[0]
    tm = _row_tile(n)
    n_tab = cos.shape[0] // tm
    return pl.pallas_call(
        _q_c_kernel,
        grid=(n // tm,),
        in_specs=[
            pl.BlockSpec((tm, C_Q_RANK), lambda i: (i, 0)),
            _resident(w.shape),
            pl.BlockSpec((tm, LANES), lambda i: (i % n_tab, 0)),
            pl.BlockSpec((tm, LANES), lambda i: (i % n_tab, 0)),
        ],
        out_specs=pl.BlockSpec((tm, C_HEADS * C_QK), lambda i: (i, 0)),
        out_shape=jax.ShapeDtypeStruct((n, C_HEADS * C_QK), BF16),
        compiler_params=_params(1),
        name="q_c",
    )(cq, w, cos, sin)


def _kv_c_kernel(ckv_ref, krp_ref, w_ref, k_ref, v_ref):
    cb = ckv_ref[...].astype(BF16)
    krp = krp_ref[...]
    for p in range(C_HEADS // 2):
        kk = _dot(cb, w_ref[:, 2 * p * C_NOPE:(2 * p + 2) * C_NOPE]).astype(k_ref.dtype)
        for u in range(2):
            o0 = (2 * p + u) * C_QK
            k_ref[:, o0:o0 + C_NOPE] = kk[:, u * C_NOPE:(u + 1) * C_NOPE]
            k_ref[:, o0 + C_NOPE:o0 + C_QK] = krp
    v0 = C_HEADS * C_NOPE
    for j in range(C_HEADS * C_V // COL_CHUNK):
        cols = slice(j * COL_CHUNK, (j + 1) * COL_CHUNK)
        v_ref[:, cols] = _dot(cb, w_ref[:, v0 + j * COL_CHUNK:v0 + (j + 1) * COL_CHUNK]).astype(v_ref.dtype)


def _kv_c(ckv, krp, w):
    n = ckv.shape[0]
    tm = _row_tile(n)
    rows = lambda i: (i, 0)
    return pl.pallas_call(
        _kv_c_kernel,
        grid=(n // tm,),
        in_specs=[pl.BlockSpec((tm, C_KV_RANK), rows), pl.BlockSpec((tm, LANES), rows), _resident(w.shape)],
        out_specs=[pl.BlockSpec((tm, C_HEADS * C_QK), rows), pl.BlockSpec((tm, C_HEADS * C_V), rows)],
        out_shape=[
            jax.ShapeDtypeStruct((n, C_HEADS * C_QK), BF16),
            jax.ShapeDtypeStruct((n, C_HEADS * C_V), BF16),
        ],
        compiler_params=_params(1),
        name="kv_c",
    )(ckv, krp, w)


def _mla_attn_kernel(q_ref, k_ref, v_ref, o_ref, *, tq, tk, pos0):
    t = q_ref.shape[1]
    scale = (C_NOPE + C_ROPE) ** -0.5
    row = lax.broadcasted_iota(jnp.int32, (tq, tk), 0) // CHUNK
    col = lax.broadcasted_iota(jnp.int32, (tq, tk), 1) // CHUNK
    diag_ok = col <= row

    def attend(q, kb, masked, carry):
        m, l, acc = carry
        k0 = pl.multiple_of(kb * tk, tk)
        s = _dot_t(q, k_ref[0, pl.ds(k0, tk), :]) * scale
        if masked:
            s = jnp.where(diag_ok, s, NEG_INF)
        m_new = jnp.maximum(m, jnp.max(s, -1, keepdims=True))
        alpha = jnp.exp(m - m_new)
        p = jnp.exp(s - m_new)
        l = alpha * l + jnp.sum(p, -1, keepdims=True)
        acc = alpha * acc + _dot(p.astype(BF16), v_ref[0, pl.ds(k0, tk), :])
        return m_new, l, acc

    def q_block(qi, carry):
        q0 = pl.multiple_of(qi * tq, tq)
        q = q_ref[0, pl.ds(q0, tq), :]
        n_full = (pos0 + q0) // tk
        init = (jnp.full((tq, 1), NEG_INF, F32), jnp.zeros((tq, 1), F32), jnp.zeros((tq, C_V), F32))
        state = lax.fori_loop(0, n_full, lambda kb, c: attend(q, kb, False, c), init)
        _, l, acc = attend(q, n_full, True, state)
        o_ref[0, pl.ds(q0, tq), :] = (acc * (1.0 / l)).astype(o_ref.dtype)
        return carry

    lax.fori_loop(0, t // tq, q_block, 0)


def _mla_attn(q, k, v, *, tq, tk, pos0):
    b, t, _ = q.shape
    s_len = k.shape[1]
    assert t % tq == 0 and s_len % tk == 0 and pos0 % tk == 0 and (tq == tk or t == tq <= tk)
    assert pos0 + t <= s_len
    return pl.pallas_call(
        functools.partial(_mla_attn_kernel, tq=tq, tk=tk, pos0=pos0),
        grid=(b, C_HEADS),
        in_specs=[
            pl.BlockSpec((1, t, C_QK), lambda bi, h: (bi, 0, h)),
            pl.BlockSpec((1, s_len, C_QK), lambda bi, h: (bi, 0, h)),
            pl.BlockSpec((1, s_len, C_V), lambda bi, h: (bi, 0, h)),
        ],
        out_specs=pl.BlockSpec((1, t, C_V), lambda bi, h: (bi, 0, h)),
        out_shape=jax.ShapeDtypeStruct((b, t, C_HEADS * C_V), BF16),
        compiler_params=_params(2),
        name="mla_attn",
    )(q, k, v)


def _prepare_weights(w):
    o1 = A_Q_W
    o2 = o1 + A_KV_W
    o3 = o2 + A_KV_W
    o4 = o3 + B_W
    o5 = o4 + B_W
    w_ab = w['w_in_ab'][0]
    w_ab = jnp.concatenate([w_ab[:, :o1], w_ab[:, o3:o4], w_ab[:, o4:o5], w_ab[:, o5:], w_ab[:, o1:o2],
                            w_ab[:, o2:o3]], axis=1)
    w_c = jnp.pad(w['w_in_c'][0], ((0, 0), (0, LANES - C_ROPE)))
    hq = C_NOPE + C_ROPE
    q_cols = []
    for p in range(C_HEADS // 2):
        h0, h1 = 2 * p, 2 * p + 1
        q_cols += [jnp.arange(h0 * hq, h0 * hq + C_NOPE), jnp.arange(h1 * hq, h1 * hq + C_NOPE),
                   jnp.arange(h0 * hq + C_NOPE, (h0 + 1) * hq), jnp.arange(h1 * hq + C_NOPE, (h1 + 1) * hq)]
    w_q = w['w_q_b_c'][0][:, jnp.concatenate(q_cols)]
    hkv = C_NOPE + C_V
    k_cols = jnp.concatenate([jnp.arange(h * hkv, h * hkv + C_NOPE) for h in range(C_HEADS)])
    v_cols = jnp.concatenate([jnp.arange(h * hkv + C_NOPE, (h + 1) * hkv) for h in range(C_HEADS)])
    w_kv = w['w_kv_b_c'][0][:, jnp.concatenate([k_cols, v_cols])]
    row = lambda a: a.reshape(1, -1)
    return {
        'w_in_ab': w_ab.astype(BF16), 'w_out_ab': w['w_out_ab'][0].astype(BF16),
        'w_in_c': w_c.astype(BF16), 'w_q_b_c': w_q.astype(BF16), 'w_kv_b_c': w_kv.astype(BF16),
        'w_out_c': w['w_out_c'][0].astype(BF16),
        'g_q_c': row(w['g_q_c'][0]), 'g_kv_c': row(w['g_kv_c'][0]),
        'sinks_a': w['sinks_a'][0], 'rel_bias_b': w['rel_bias_b'][0],
        'ln1_g': [row(w['ln1_g'][i]) for i in range(DEPTH)], 'ln1_b': [row(w['ln1_b'][i]) for i in range(DEPTH)],
        'ln2_g': [row(w['ln2_g'][i]) for i in range(DEPTH)], 'ln2_b': [row(w['ln2_b'][i]) for i in range(DEPTH)],
        'w_mlp_up': [w['w_mlp_up'][i].astype(BF16) for i in range(DEPTH)],
        'w_mlp_down': [w['w_mlp_down'][i].astype(BF16) for i in range(DEPTH)],
        'w_ple_gate': [w['w_ple_gate'][i].astype(BF16) for i in range(DEPTH)],
        'b_ple_gate': [row(w['b_ple_gate'][i]) for i in range(DEPTH)],
        'w_ple': [w['w_ple'][i].astype(BF16) for i in range(DEPTH)],
    }


def _channel_mix(xf, pf, pw, i):
    xf = _mlp_ln(xf, pw['w_mlp_up'][i], pw['w_mlp_down'][i], pw['ln2_g'][i], pw['ln2_b'][i])
    return _ple(xf, pf, pw['w_ple_gate'][i], pw['b_ple_gate'][i], pw['w_ple'][i])


def _trunk(x, p, pos0, past, pw):
    b, t, d = x.shape
    n = b * t
    tm = _row_tile(n)
    xf = x.reshape(n, d)

    h, kv_state = _proj_ab(xf, pw['w_in_ab'], t, pos0)
    h3 = h.reshape(b, t, AB_IN_W)
    s_rows = kv_state.shape[0] // b
    kv_state = kv_state.reshape(b, s_rows, AB_KV_W)
    kb_new = kv_state[:, :, :B_W]
    vb_new = kv_state[:, :, B_W:2 * B_W]
    ka_new = kv_state[:, :, 2 * B_W:2 * B_W + A_KV_W]
    va_new = kv_state[:, :, 2 * B_W + A_KV_W:]
    q_a = (h3, A_Q_W, 0)
    q_b = (h3, B_W, 1)
    if past is None:
        tq = 2 * CHUNK
        k_b, v_b = (h3, B_W, 2), (h3, B_W, 3)
        k_a, v_a = (h3, A_KV_W, (AB_IN_W - 2 * A_KV_W) // A_KV_W), (h3, A_KV_W, (AB_IN_W - A_KV_W) // A_KV_W)
        n_past_a = n_past_b = 0
        ak, av, bk, bv = ka_new[:, -WINDOW:], va_new[:, -WINDOW:], kb_new, vb_new
    else:
        tq = CHUNK
        n_past_a, n_past_b = past[0].shape[1], past[2].shape[1]
        full = [jnp.concatenate([c.reshape(b, c.shape[1], -1), new], axis=1)
                for c, new in zip(past[:4], (ka_new, va_new, kb_new, vb_new))]
        ak, av = full[0][:, -WINDOW:], full[1][:, -WINDOW:]
        bk, bv = full[2][:, -B_BAND_PAST:], full[3][:, -B_BAND_PAST:]
        k_a, v_a, k_b, v_b = [(f.astype(BF16), f.shape[2], 0) for f in full]
    attn_a = _band_attn(q_a, k_a, v_a, heads=A_HEADS, kv_heads=A_KV_HEADS, band=WINDOW, tq=tq, past=n_past_a,
                        sinks=pw['sinks_a'])
    bias_b = _rel_bias(pw['rel_bias_b'], B_BAND_PAST, tq)
    attn_b = _band_attn(q_b, k_b, v_b, heads=B_HEADS, kv_heads=B_HEADS, band=B_BAND_PAST, tq=tq, past=n_past_b,
                        bias=bias_b)
    xf = _outproj_ln([attn_a.reshape(n, A_Q_W), attn_b.reshape(n, B_W)], pw['w_out_ab'], xf,
                     pw['ln1_g'][0], pw['ln1_b'][0])
    xf = _channel_mix(xf, p[0].reshape(n, -1), pw, 0)

    cos_c, sin_c = _rope_tables(t, pos0, C_ROPE, max(t, tm))
    cq, ckv, kr, krp = _proj_c(xf, pw['w_in_c'], pw['g_q_c'], pw['g_kv_c'], cos_c, sin_c)
    q = _q_c(cq, pw['w_q_b_c'], cos_c, sin_c).reshape(b, t, C_HEADS * C_QK)
    if past is None:
        s_len = t
        tq_c = tk_c = min(256, t)
        ckv_all, krp_all = ckv, krp
    else:
        past_ckv, past_kr = past[4], past[5]
        tq_c, tk_c = t, 2 * CHUNK
        s_len = -(-(past_ckv.shape[1] + t) // tk_c) * tk_c
        pad = s_len - past_ckv.shape[1] - t
        ckv_all = jnp.pad(jnp.concatenate([past_ckv, ckv.reshape(b, t, C_KV_RANK)], axis=1),
                          ((0, 0), (0, pad), (0, 0))).reshape(b * s_len, C_KV_RANK)
        past_krp = jnp.pad(past_kr, ((0, 0), (0, 0), (0, LANES - C_ROPE))).astype(BF16)
        krp_all = jnp.pad(jnp.concatenate([past_krp, krp.reshape(b, t, LANES)], axis=1),
                          ((0, 0), (0, pad), (0, 0))).reshape(b * s_len, LANES)
    k_c, v_c = _kv_c(ckv_all, krp_all, pw['w_kv_b_c'])
    attn_c = _mla_attn(q, k_c.reshape(b, s_len, C_HEADS * C_QK), v_c.reshape(b, s_len, C_HEADS * C_V),
                       tq=tq_c, tk=tk_c, pos0=pos0)
    xf = _outproj_ln([attn_c.reshape(n, C_HEADS * C_V)], pw['w_out_c'], xf, pw['ln1_g'][1], pw['ln1_b'][1])
    xf = _channel_mix(xf, p[1].reshape(n, -1), pw, 1)

    heads4 = lambda a, hh: a.reshape(1, b, a.shape[1], hh, HEAD_DIM)
    return (xf.reshape(b, t, d), heads4(ak, A_KV_HEADS), heads4(av, A_KV_HEADS), heads4(bk, B_HEADS),
            heads4(bv, B_HEADS), ckv.reshape(1, b, t, C_KV_RANK), kr.reshape(1, b, t, C_ROPE))


def kernel(x_prompt, x_sample, cache_a_k, cache_a_v, cache_b_k, cache_b_v, cache_c_kv, cache_c_krope, p_prompt,
           p_sample, w_in_ab, sinks_a, rel_bias_b, w_out_ab, w_in_c, g_q_c, w_q_b_c, g_kv_c, w_kv_b_c, w_out_c,
           ln1_g, ln1_b, ln2_g, ln2_b, w_mlp_up, w_mlp_down, w_ple_gate, b_ple_gate, w_ple):
    pw = _prepare_weights({
        'w_in_ab': w_in_ab, 'sinks_a': sinks_a, 'rel_bias_b': rel_bias_b, 'w_out_ab': w_out_ab,
        'w_in_c': w_in_c, 'g_q_c': g_q_c, 'w_q_b_c': w_q_b_c, 'g_kv_c': g_kv_c, 'w_kv_b_c': w_kv_b_c,
        'w_out_c': w_out_c, 'ln1_g': ln1_g, 'ln1_b': ln1_b, 'ln2_g': ln2_g, 'ln2_b': ln2_b,
        'w_mlp_up': w_mlp_up, 'w_mlp_down': w_mlp_down, 'w_ple_gate': w_ple_gate, 'b_ple_gate': b_ple_gate,
        'w_ple': w_ple,
    })
    prompt = _trunk(x_prompt, p_prompt, 0, None, pw)
    past = (cache_a_k[0], cache_a_v[0], cache_b_k[0], cache_b_v[0], cache_c_kv[0], cache_c_krope[0])
    sample = _trunk(x_sample, p_sample, cache_c_kv.shape[2], past, pw)
    return (prompt[0], sample[0]) + prompt[1:] + sample[1:]
```

```python
import functools

import jax
import jax.numpy as jnp
from jax import lax
from jax.experimental import pallas as pl
from jax.experimental.pallas import tpu as pltpu

F32 = jnp.float32
BF16 = jnp.bfloat16

CHUNK = 64
HEAD_DIM = 128
A_HEADS = 8
A_KV_HEADS = 2
WINDOW = 128
B_HEADS = 8
B_BAND_PAST = 512
REL_CLIP = 128
C_HEADS = 16
C_Q_RANK = 768
C_KV_RANK = 512
C_NOPE = 128
C_ROPE = 64
C_V = 128
DEPTH = 2
ROPE_THETA = 10000.0
LN_EPS = 1e-5
RMS_EPS = 1e-6
NEG_INF = -1e30
DEEPNORM_ALPHA = (2 * DEPTH) ** 0.25

A_Q_W = A_HEADS * HEAD_DIM
A_KV_W = A_KV_HEADS * HEAD_DIM
B_W = B_HEADS * HEAD_DIM
AB_IN_W = A_Q_W + 2 * A_KV_W + 3 * B_W
AB_KV_COL0 = A_Q_W + B_W
AB_KV_W = AB_IN_W - AB_KV_COL0
C_IN_W = C_Q_RANK + C_KV_RANK + C_ROPE
C_QK = 256

LANES = 128
V7X_VMEM_BYTES = 64 * 1024 * 1024
VMEM_LIMIT = V7X_VMEM_BYTES - 8 * 1024 * 1024

ROW_TILE = 512
COL_CHUNK = 512


def _params(n_axes):
    return pltpu.CompilerParams(dimension_semantics=("arbitrary",) * n_axes, vmem_limit_bytes=VMEM_LIMIT)


def _resident(shape):
    nd = len(shape)
    return pl.BlockSpec(shape, lambda *_: (0,) * nd, pipeline_mode=pl.Buffered(1))


def _row_tile(n):
    return ROW_TILE if n % ROW_TILE == 0 else n


def _dot(a, b):
    return jnp.dot(a, b, preferred_element_type=F32)


def _dot_t(a, b):
    return lax.dot_general(a, b, (((1,), (1,)), ((), ())), preferred_element_type=F32)


def _layer_norm(y, g, b):
    mu = jnp.mean(y, -1, keepdims=True)
    var = jnp.mean(jnp.square(y - mu), -1, keepdims=True)
    return (y - mu) * lax.rsqrt(var + LN_EPS) * g + b


def _rms_norm(y, g):
    return y * lax.rsqrt(jnp.mean(jnp.square(y), -1, keepdims=True) + RMS_EPS) * g


def _rope_tile(t, cos, sin, d):
    if d == LANES:
        swapped = pltpu.roll(t, LANES // 2, 1)
    else:
        lane = lax.broadcasted_iota(jnp.int32, t.shape, 1)
        swapped = jnp.where((lane % d) < d // 2, pltpu.roll(t, LANES - d // 2, 1), pltpu.roll(t, d // 2, 1))
    return t * cos + swapped * sin


def _rope_tables(t, pos0, d, rows):
    half = d // 2
    inv = ROPE_THETA ** (-jnp.arange(half, dtype=F32) * (2.0 / d))
    ang = (jnp.arange(t, dtype=F32) + pos0)[:, None] * inv[None, :]
    cos = jnp.cos(ang)
    sin = jnp.sin(ang)
    reps = (rows // t, LANES // d)
    return jnp.tile(jnp.concatenate([cos, cos], 1), reps), jnp.tile(jnp.concatenate([-sin, sin], 1), reps)


def _proj_ab_kernel(x_ref, w_ref, cos_ref, sin_ref, h_ref, kv_ref, *, kv_period):
    xb = x_ref[...].astype(BF16)
    cos = cos_ref[...]
    sin = sin_ref[...]
    keep_state = (pl.program_id(0) % kv_period) == kv_period - 1
    rope_tiles = set(range(A_HEADS)) | {(AB_IN_W - 2 * A_KV_W) // LANES + u for u in range(A_KV_HEADS)}
    for j in range(AB_IN_W // COL_CHUNK):
        c0 = j * COL_CHUNK
        acc = _dot(xb, w_ref[:, c0:c0 + COL_CHUNK])
        parts = []
        for u in range(COL_CHUNK // LANES):
            part = acc[:, u * LANES:(u + 1) * LANES]
            if c0 // LANES + u in rope_tiles:
                part = _rope_tile(part, cos, sin, HEAD_DIM)
            parts.append(part)
        acc = jnp.concatenate(parts, axis=1)
        h_ref[:, c0:c0 + COL_CHUNK] = acc.astype(h_ref.dtype)
        if c0 >= AB_KV_COL0:
            @pl.when(keep_state)
            def _(acc=acc, c0=c0):
                kv_ref[:, c0 - AB_KV_COL0:c0 - AB_KV_COL0 + COL_CHUNK] = acc


def _proj_ab(xf, w, seq, pos0):
    n, d = xf.shape
    tm = _row_tile(n)
    assert seq % tm == 0 or tm % seq == 0
    kv_period = max(seq // tm, 1)
    assert min(seq, B_BAND_PAST) == min(seq, tm)
    cos, sin = _rope_tables(seq, pos0, HEAD_DIM, max(seq, tm))
    n_tab = cos.shape[0] // tm
    return pl.pallas_call(
        functools.partial(_proj_ab_kernel, kv_period=kv_period),
        grid=(n // tm,),
        in_specs=[
            pl.BlockSpec((tm, d), lambda i: (i, 0)),
            _resident(w.shape),
            pl.BlockSpec((tm, LANES), lambda i: (i % n_tab, 0)),
            pl.BlockSpec((tm, LANES), lambda i: (i % n_tab, 0)),
        ],
        out_specs=[
            pl.BlockSpec((tm, AB_IN_W), lambda i: (i, 0)),
            pl.BlockSpec((tm, AB_KV_W), lambda i: (i // kv_period, 0)),
        ],
        out_shape=[
            jax.ShapeDtypeStruct((n, AB_IN_W), BF16),
            jax.ShapeDtypeStruct((n // kv_period, AB_KV_W), F32),
        ],
        compiler_params=_params(1),
        name="proj_ab",
    )(xf, w, cos, sin)


def _rel_bias_kernel(tab_ref, o_ref, *, band):
    h = pl.program_id(0)
    shape = o_ref.shape[1:]
    r = lax.broadcasted_iota(jnp.int32, shape, 0)
    w = lax.broadcasted_iota(jnp.int32, shape, 1)
    idx = jnp.clip(band + r - w, -REL_CLIP, REL_CLIP) + REL_CLIP

    def body(d, acc):
        return jnp.where(idx == d, tab_ref[h, d], acc)

    o_ref[0] = lax.fori_loop(0, 2 * REL_CLIP + 1, body, jnp.zeros(shape, F32))


def _rel_bias(table, band, tq):
    heads = table.shape[0]
    return pl.pallas_call(
        functools.partial(_rel_bias_kernel, band=band),
        grid=(heads,),
        in_specs=[pl.BlockSpec(memory_space=pltpu.SMEM)],
        out_specs=pl.BlockSpec((1, tq, band + tq), lambda h: (h, 0, 0)),
        out_shape=jax.ShapeDtypeStruct((heads, tq, band + tq), F32),
        compiler_params=_params(1),
        name="rel_bias",
    )(table)


def _band_attn_kernel(*refs, heads, kv_heads, band, tq, past, has_bias, has_sinks):
    q_ref, k_ref, v_ref = refs[:3]
    rest = list(refs[3:])
    bias_ref = rest.pop(0) if has_bias else None
    sink_ref = rest.pop(0) if has_sinks else None
    o_ref = rest.pop(0)

    width = band + tq
    scale = HEAD_DIM ** -0.5
    ws = past + pl.program_id(1) * tq - band
    r = lax.broadcasted_iota(jnp.int32, (tq, width), 0) // CHUNK
    w = lax.broadcasted_iota(jnp.int32, (tq, width), 1)
    wc = w // CHUNK
    allowed = (wc >= r) & (wc <= r + band // CHUNK) & (w + ws >= 0)

    def window(ref, cols):
        if past >= band:
            return ref[0, pl.ds(pl.multiple_of(ws, CHUNK), width), cols]
        pieces = [ref[0, pl.ds(pl.multiple_of(jnp.maximum(ws + c * LANES, 0), LANES), LANES), cols]
                  for c in range(width // LANES)]
        return jnp.concatenate(pieces, axis=0)

    group = heads // kv_heads
    for kh in range(kv_heads):
        cols = slice(kh * HEAD_DIM, (kh + 1) * HEAD_DIM)
        k_w = window(k_ref, cols)
        v_w = window(v_ref, cols)
        for g in range(group):
            h = kh * group + g
            hcols = slice(h * HEAD_DIM, (h + 1) * HEAD_DIM)
            s = _dot_t(q_ref[0, :, hcols], k_w) * scale
            if has_bias:
                s = s + bias_ref[h]
            s = jnp.where(allowed, s, NEG_INF)
            m = jnp.max(s, -1, keepdims=True)
            if has_sinks:
                sink = sink_ref[h]
                m = jnp.maximum(m, sink)
            e = jnp.exp(s - m)
            den = jnp.sum(e, -1, keepdims=True)
            if has_sinks:
                den = den + jnp.exp(sink - m)
            p = (e * (1.0 / den)).astype(BF16)
            o_ref[0, :, hcols] = _dot(p, v_w).astype(o_ref.dtype)


def _band_attn(q, k, v, *, heads, kv_heads, band, tq, past, bias=None, sinks=None):
    (qa, qw, qi), (ka, kw, ki), (va, vw, vi) = q, k, v
    b, t, _ = qa.shape
    s_len = ka.shape[1]
    assert t % tq == 0 and s_len == past + t and qw == heads * HEAD_DIM and kw == kv_heads * HEAD_DIM
    assert past >= band or (past == 0 and tq % LANES == 0 and band % LANES == 0)
    in_specs = [
        pl.BlockSpec((1, tq, qw), lambda bi, i: (bi, i, qi)),
        pl.BlockSpec((1, s_len, kw), lambda bi, i: (bi, 0, ki)),
        pl.BlockSpec((1, s_len, vw), lambda bi, i: (bi, 0, vi)),
    ]
    args = [qa, ka, va]
    if bias is not None:
        in_specs.append(_resident(bias.shape))
        args.append(bias)
    if sinks is not None:
        in_specs.append(pl.BlockSpec(memory_space=pltpu.SMEM))
        args.append(sinks)
    return pl.pallas_call(
        functools.partial(_band_attn_kernel, heads=heads, kv_heads=kv_heads, band=band, tq=tq, past=past,
                          has_bias=bias is not None, has_sinks=sinks is not None),
        grid=(b, t // tq),
        in_specs=in_specs,
        out_specs=pl.BlockSpec((1, tq, qw), lambda bi, i: (bi, i, 0)),
        out_shape=jax.ShapeDtypeStruct((b, t, qw), BF16),
        compiler_params=_params(2),
        name="band_attn",
    )(*args)


def _outproj_ln_kernel(*refs, n_in):
    a_refs = refs[:n_in]
    w_ref, x_ref, g_ref, b_ref, o_ref = refs[n_in:]
    d_out = o_ref.shape[1]
    for j in range(d_out // COL_CHUNK):
        cols = slice(j * COL_CHUNK, (j + 1) * COL_CHUNK)
        y = DEEPNORM_ALPHA * x_ref[:, cols]
        r0 = 0
        for a_ref in a_refs:
            kk = a_ref.shape[1]
            y = y + _dot(a_ref[...], w_ref[r0:r0 + kk, cols])
            r0 += kk
        o_ref[:, cols] = y
    o_ref[...] = _layer_norm(o_ref[...], g_ref[...], b_ref[...])


def _outproj_ln(a_list, w, xf, g, b):
    n, d = xf.shape
    tm = _row_tile(n)
    assert sum(a.shape[1] for a in a_list) == w.shape[0]
    return pl.pallas_call(
        functools.partial(_outproj_ln_kernel, n_in=len(a_list)),
        grid=(n // tm,),
        in_specs=[pl.BlockSpec((tm, a.shape[1]), lambda i: (i, 0)) for a in a_list] + [
            _resident(w.shape),
            pl.BlockSpec((tm, d), lambda i: (i, 0)),
            _resident(g.shape),
            _resident(b.shape),
        ],
        out_specs=pl.BlockSpec((tm, d), lambda i: (i, 0)),
        out_shape=jax.ShapeDtypeStruct((n, d), F32),
        compiler_params=_params(1),
        name="outproj_ln",
    )(*a_list, w, xf, g, b)


def _mlp_ln_kernel(x_ref, wu_ref, wd_ref, g_ref, b_ref, o_ref, xb_ref):
    f = pl.program_id(1)

    @pl.when(f == 0)
    def _():
        xb_ref[...] = x_ref[...].astype(BF16)

    hid = _dot(xb_ref[...], wu_ref[...])
    hid = jnp.square(jnp.maximum(hid, 0.0)).astype(BF16)
    part = _dot(hid, wd_ref[...])

    @pl.when(f == 0)
    def _():
        o_ref[...] = part

    @pl.when(f > 0)
    def _():
        o_ref[...] += part

    @pl.when(f == pl.num_programs(1) - 1)
    def _():
        o_ref[...] = _layer_norm(DEEPNORM_ALPHA * x_ref[...] + o_ref[...], g_ref[...], b_ref[...])


def _mlp_ln(xf, w_up, w_down, g, b, tf=1024):
    n, d = xf.shape
    d_ff = w_up.shape[1]
    tm = _row_tile(n)
    return pl.pallas_call(
        _mlp_ln_kernel,
        grid=(n // tm, d_ff // tf),
        in_specs=[
            pl.BlockSpec((tm, d), lambda i, f: (i, 0)),
            pl.BlockSpec((d, tf), lambda i, f: (0, f)),
            pl.BlockSpec((tf, d), lambda i, f: (f, 0)),
            _resident(g.shape),
            _resident(b.shape),
        ],
        out_specs=pl.BlockSpec((tm, d), lambda i, f: (i, 0)),
        out_shape=jax.ShapeDtypeStruct((n, d), F32),
        scratch_shapes=[pltpu.VMEM((tm, d), BF16)],
        compiler_params=_params(2),
        name="mlp_ln",
    )(xf, w_up, w_down, g, b)


def _ple_kernel(x_ref, p_ref, wg_ref, bg_ref, wp_ref, o_ref):
    xb = x_ref[...].astype(BF16)
    pb = p_ref[...].astype(BF16)
    for j in range(o_ref.shape[1] // COL_CHUNK):
        cols = slice(j * COL_CHUNK, (j + 1) * COL_CHUNK)
        gate = jax.nn.sigmoid(_dot(xb, wg_ref[:, cols]) + bg_ref[:, cols])
        o_ref[:, cols] = x_ref[:, cols] + gate * _dot(pb, wp_ref[:, cols])


def _ple(xf, pf, wg, bg, wp):
    n, d = xf.shape
    tm = _row_tile(n)
    return pl.pallas_call(
        _ple_kernel,
        grid=(n // tm,),
        in_specs=[
            pl.BlockSpec((tm, d), lambda i: (i, 0)),
            pl.BlockSpec((tm, pf.shape[1]), lambda i: (i, 0)),
            _resident(wg.shape),
            _resident(bg.shape),
            _resident(wp.shape),
        ],
        out_specs=pl.BlockSpec((tm, d), lambda i: (i, 0)),
        out_shape=jax.ShapeDtypeStruct((n, d), F32),
        compiler_params=_params(1),
        name="ple",
    )(xf, pf, wg, bg, wp)


def _proj_c_kernel(x_ref, w_ref, gq_ref, gkv_ref, cos_ref, sin_ref, cq_ref, ckv_ref, kr_ref, krp_ref):
    h = _dot(x_ref[...].astype(BF16), w_ref[...])
    cq_ref[...] = _rms_norm(h[:, :C_Q_RANK], gq_ref[...]).astype(cq_ref.dtype)
    ckv_ref[...] = _rms_norm(h[:, C_Q_RANK:C_Q_RANK + C_KV_RANK], gkv_ref[...])
    t = h[:, C_Q_RANK + C_KV_RANK:]
    rot = _rope_tile(t, cos_ref[...], sin_ref[...], C_ROPE)
    lane = lax.broadcasted_iota(jnp.int32, rot.shape, 1)
    rot = jnp.where(lane < C_ROPE, rot, 0.0)
    kr_ref[...] = rot[:, :C_ROPE]
    krp_ref[...] = rot.astype(krp_ref.dtype)


def _proj_c(xf, w, gq, gkv, cos, sin):
    n, d = xf.shape
    tm = _row_tile(n)
    n_tab = cos.shape[0] // tm
    rows = lambda i: (i, 0)
    return pl.pallas_call(
        _proj_c_kernel,
        grid=(n // tm,),
        in_specs=[
            pl.BlockSpec((tm, d), rows),
            _resident(w.shape),
            _resident(gq.shape),
            _resident(gkv.shape),
            pl.BlockSpec((tm, LANES), lambda i: (i % n_tab, 0)),
            pl.BlockSpec((tm, LANES), lambda i: (i % n_tab, 0)),
        ],
        out_specs=[
            pl.BlockSpec((tm, C_Q_RANK), rows),
            pl.BlockSpec((tm, C_KV_RANK), rows),
            pl.BlockSpec((tm, C_ROPE), rows),
            pl.BlockSpec((tm, LANES), rows),
        ],
        out_shape=[
            jax.ShapeDtypeStruct((n, C_Q_RANK), BF16),
            jax.ShapeDtypeStruct((n, C_KV_RANK), F32),
            jax.ShapeDtypeStruct((n, C_ROPE), F32),
            jax.ShapeDtypeStruct((n, LANES), BF16),
        ],
        compiler_params=_params(1),
        name="proj_c",
    )(xf, w, gq, gkv, cos, sin)


def _q_c_kernel(cq_ref, w_ref, cos_ref, sin_ref, q_ref):
    cq = cq_ref[...]
    cos = cos_ref[...]
    sin = sin_ref[...]
    pair_w = 2 * C_NOPE + 2 * C_ROPE
    for p in range(C_HEADS // 2):
        acc = _dot(cq, w_ref[:, p * pair_w:(p + 1) * pair_w])
        rot = _rope_tile(acc[:, 2 * C_NOPE:], cos, sin, C_ROPE)
        lane = lax.broadcasted_iota(jnp.int32, rot.shape, 1)
        o0 = 2 * p * C_QK
        q_ref[:, o0:o0 + C_NOPE] = acc[:, :C_NOPE].astype(q_ref.dtype)
        q_ref[:, o0 + C_NOPE:o0 + C_QK] = jnp.where(lane < C_ROPE, rot, 0.0).astype(q_ref.dtype)
        q_ref[:, o0 + C_QK:o0 + C_QK + C_NOPE] = acc[:, C_NOPE:2 * C_NOPE].astype(q_ref.dtype)
        q_ref[:, o0 + C_QK + C_NOPE:o0 + 2 * C_QK] = jnp.where(
            lane < C_ROPE, pltpu.roll(rot, C_ROPE, 1), 0.0).astype(q_ref.dtype)


def _q_c(cq, w, cos, sin):
    n = cq.shape[0]
    tm = _row_tile(n)
    n_tab = cos.shape[0] // tm
    return pl.pallas_call(
        _q_c_kernel,
        grid=(n // tm,),
        in_specs=[
            pl.BlockSpec((tm, C_Q_RANK), lambda i: (i, 0)),
            _resident(w.shape),
            pl.BlockSpec((tm, LANES), lambda i: (i % n_tab, 0)),
            pl.BlockSpec((tm, LANES), lambda i: (i % n_tab, 0)),
        ],
        out_specs=pl.BlockSpec((tm, C_HEADS * C_QK), lambda i: (i, 0)),
        out_shape=jax.ShapeDtypeStruct((n, C_HEADS * C_QK), BF16),
        compiler_params=_params(1),
        name="q_c",
    )(cq, w, cos, sin)


def _kv_c_kernel(ckv_ref, krp_ref, wkt_ref, wv_ref, kt_ref, v_ref):
    cb = ckv_ref[...].astype(BF16)
    eye = (lax.broadcasted_iota(jnp.int32, (LANES, LANES), 0)
           == lax.broadcasted_iota(jnp.int32, (LANES, LANES), 1)).astype(F32).astype(BF16)
    kr_t = _dot_t(eye, krp_ref[...]).astype(kt_ref.dtype)
    k_t = _dot_t(wkt_ref[...], cb).astype(kt_ref.dtype)
    for h in range(C_HEADS):
        kt_ref[0, 0, h * C_QK:h * C_QK + C_NOPE, :] = k_t[h * C_NOPE:(h + 1) * C_NOPE, :]
        kt_ref[0, 0, h * C_QK + C_NOPE:(h + 1) * C_QK, :] = kr_t
    for j in range(C_HEADS * C_V // COL_CHUNK):
        cols = slice(j * COL_CHUNK, (j + 1) * COL_CHUNK)
        v_ref[:, cols] = _dot(cb, wv_ref[:, cols]).astype(v_ref.dtype)


def _kv_c(ckv, krp, wkt, wv, batch, tk):
    n = ckv.shape[0]
    nkb = n // batch // tk
    assert n == batch * nkb * tk
    rows = lambda i: (i, 0)
    return pl.pallas_call(
        _kv_c_kernel,
        grid=(n // tk,),
        in_specs=[pl.BlockSpec((tk, C_KV_RANK), rows), pl.BlockSpec((tk, LANES), rows), _resident(wkt.shape),
                  _resident(wv.shape)],
        out_specs=[pl.BlockSpec((1, 1, C_HEADS * C_QK, tk), lambda i: (i // nkb, i % nkb, 0, 0)),
                   pl.BlockSpec((tk, C_HEADS * C_V), rows)],
        out_shape=[
            jax.ShapeDtypeStruct((batch, nkb, C_HEADS * C_QK, tk), BF16),
            jax.ShapeDtypeStruct((n, C_HEADS * C_V), BF16),
        ],
        compiler_params=_params(1),
        name="kv_c",
    )(ckv, krp, wkt, wv)


LOG2_E = 1.4426950408889634
MLA_SCALE = (C_NOPE + C_ROPE) ** -0.5
MLA_LONG_BLOCK = 512
MLA_SHORT_BLOCK = 256


def _mla_attn_kernel(q_ref, kt_ref, v_ref, o_ref, *, pos0, heads):
    t = q_ref.shape[1]
    tq = tk = kt_ref.shape[3]
    row = lax.broadcasted_iota(jnp.int32, (tq, tk), 0) // CHUNK
    col = lax.broadcasted_iota(jnp.int32, (tq, tk), 1) // CHUNK
    diag_ok = col <= row
    ones = jnp.ones((tk, C_V), BF16)

    def attend(qs, kb, masked, carry):
        k0 = pl.multiple_of(kb * tk, tk)
        out = []
        for g in range(heads):
            m, acc = carry[g]
            s = _dot(qs[g], kt_ref[0, kb, g * C_QK:(g + 1) * C_QK, :]) * (MLA_SCALE * LOG2_E)
            if masked:
                s = jnp.where(diag_ok, s, NEG_INF)
            m_new = jnp.maximum(m, jnp.max(s, -1, keepdims=True))
            alpha = jnp.exp2(m - m_new)
            p = jnp.exp2(s - m_new).astype(BF16)
            v_ext = jnp.concatenate([v_ref[0, pl.ds(k0, tk), g * C_V:(g + 1) * C_V], ones], axis=1)
            out.append((m_new, alpha * acc + _dot(p, v_ext)))
        return tuple(out)

    def q_block(qi, carry):
        q0 = pl.multiple_of(qi * tq, tq)
        qs = [q_ref[0, pl.ds(q0, tq), g * C_QK:(g + 1) * C_QK] for g in range(heads)]
        n_full = (pos0 + q0) // tk
        init = tuple((jnp.full((tq, 1), NEG_INF, F32), jnp.zeros((tq, 2 * C_V), F32)) for _ in range(heads))
        state = lax.fori_loop(0, n_full, lambda kb, c: attend(qs, kb, False, c), init)
        state = attend(qs, n_full, True, state)
        for g in range(heads):
            acc = state[g][1]
            o_ref[0, pl.ds(q0, tq), g * C_V:(g + 1) * C_V] = (acc[:, :C_V] * (1.0 / acc[:, C_V:])).astype(o_ref.dtype)
        return carry

    lax.fori_loop(0, t // tq, q_block, 0)


def _mla_attn(q, kt, v, *, pos0, heads=2):
    b, t, _ = q.shape
    _, nkb, _, tk = kt.shape
    s_len = nkb * tk
    assert t % tk == 0 and pos0 % tk == 0 and pos0 + t <= s_len and v.shape[1] == s_len
    return pl.pallas_call(
        functools.partial(_mla_attn_kernel, pos0=pos0, heads=heads),
        grid=(b, C_HEADS // heads),
        in_specs=[
            pl.BlockSpec((1, t, heads * C_QK), lambda bi, h: (bi, 0, h)),
            pl.BlockSpec((1, nkb, heads * C_QK, tk), lambda bi, h: (bi, 0, h, 0)),
            pl.BlockSpec((1, s_len, heads * C_V), lambda bi, h: (bi, 0, h)),
        ],
        out_specs=pl.BlockSpec((1, t, heads * C_V), lambda bi, h: (bi, 0, h)),
        out_shape=jax.ShapeDtypeStruct((b, t, C_HEADS * C_V), BF16),
        compiler_params=_params(2),
        name="mla_attn",
    )(q, kt, v)


def _mla_attn_short_kernel(q_ref, kt_ref, v_ref, o_ref, *, pos0, heads):
    t, nkb, tk = q_ref.shape[1], kt_ref.shape[1], kt_ref.shape[3]
    s_len = nkb * tk
    row = (lax.broadcasted_iota(jnp.int32, (t, s_len), 0) + pos0) // CHUNK
    col = lax.broadcasted_iota(jnp.int32, (t, s_len), 1) // CHUNK
    allowed = col <= row
    for g in range(heads):
        q = q_ref[0, :, g * C_QK:(g + 1) * C_QK]
        s = jnp.concatenate([_dot(q, kt_ref[0, kb, g * C_QK:(g + 1) * C_QK, :]) for kb in range(nkb)], axis=1)
        s = jnp.where(allowed, s * MLA_SCALE, NEG_INF)
        e = jnp.exp(s - jnp.max(s, -1, keepdims=True))
        p = (e * (1.0 / jnp.sum(e, -1, keepdims=True))).astype(BF16)
        o_ref[0, :, g * C_V:(g + 1) * C_V] = _dot(p, v_ref[0, :, g * C_V:(g + 1) * C_V]).astype(o_ref.dtype)


def _mla_attn_short(q, kt, v, *, pos0, heads=4):
    b, t, _ = q.shape
    _, nkb, _, tk = kt.shape
    s_len = nkb * tk
    assert v.shape[1] == s_len and pos0 + t <= s_len
    return pl.pallas_call(
        functools.partial(_mla_attn_short_kernel, pos0=pos0, heads=heads),
        grid=(b, C_HEADS // heads),
        in_specs=[
            pl.BlockSpec((1, t, heads * C_QK), lambda bi, h: (bi, 0, h)),
            pl.BlockSpec((1, nkb, heads * C_QK, tk), lambda bi, h: (bi, 0, h, 0)),
            pl.BlockSpec((1, s_len, heads * C_V), lambda bi, h: (bi, 0, h)),
        ],
        out_specs=pl.BlockSpec((1, t, heads * C_V), lambda bi, h: (bi, 0, h)),
        out_shape=jax.ShapeDtypeStruct((b, t, C_HEADS * C_V), BF16),
        compiler_params=_params(2),
        name="mla_attn_short",
    )(q, kt, v)


def _prepare_weights(w):
    o1 = A_Q_W
    o2 = o1 + A_KV_W
    o3 = o2 + A_KV_W
    o4 = o3 + B_W
    o5 = o4 + B_W
    w_ab = w['w_in_ab'][0]
    w_ab = jnp.concatenate([w_ab[:, :o1], w_ab[:, o3:o4], w_ab[:, o4:o5], w_ab[:, o5:], w_ab[:, o1:o2],
                            w_ab[:, o2:o3]], axis=1)
    w_c = jnp.pad(w['w_in_c'][0], ((0, 0), (0, LANES - C_ROPE)))
    hq = C_NOPE + C_ROPE
    q_cols = []
    for p in range(C_HEADS // 2):
        h0, h1 = 2 * p, 2 * p + 1
        q_cols += [jnp.arange(h0 * hq, h0 * hq + C_NOPE), jnp.arange(h1 * hq, h1 * hq + C_NOPE),
                   jnp.arange(h0 * hq + C_NOPE, (h0 + 1) * hq), jnp.arange(h1 * hq + C_NOPE, (h1 + 1) * hq)]
    w_q = w['w_q_b_c'][0][:, jnp.concatenate(q_cols)]
    hkv = C_NOPE + C_V
    k_cols = jnp.concatenate([jnp.arange(h * hkv, h * hkv + C_NOPE) for h in range(C_HEADS)])
    v_cols = jnp.concatenate([jnp.arange(h * hkv + C_NOPE, (h + 1) * hkv) for h in range(C_HEADS)])
    w_k_t = w['w_kv_b_c'][0][:, k_cols].T
    w_v = w['w_kv_b_c'][0][:, v_cols]
    row = lambda a: a.reshape(1, -1)
    return {
        'w_in_ab': w_ab.astype(BF16), 'w_out_ab': w['w_out_ab'][0].astype(BF16),
        'w_in_c': w_c.astype(BF16), 'w_q_b_c': w_q.astype(BF16), 'w_k_t_c': w_k_t.astype(BF16),
        'w_v_c': w_v.astype(BF16),
        'w_out_c': w['w_out_c'][0].astype(BF16),
        'g_q_c': row(w['g_q_c'][0]), 'g_kv_c': row(w['g_kv_c'][0]),
        'sinks_a': w['sinks_a'][0], 'rel_bias_b': w['rel_bias_b'][0],
        'ln1_g': [row(w['ln1_g'][i]) for i in range(DEPTH)], 'ln1_b': [row(w['ln1_b'][i]) for i in range(DEPTH)],
        'ln2_g': [row(w['ln2_g'][i]) for i in range(DEPTH)], 'ln2_b': [row(w['ln2_b'][i]) for i in range(DEPTH)],
        'w_mlp_up': [w['w_mlp_up'][i].astype(BF16) for i in range(DEPTH)],
        'w_mlp_down': [w['w_mlp_down'][i].astype(BF16) for i in range(DEPTH)],
        'w_ple_gate': [w['w_ple_gate'][i].astype(BF16) for i in range(DEPTH)],
        'b_ple_gate': [row(w['b_ple_gate'][i]) for i in range(DEPTH)],
        'w_ple': [w['w_ple'][i].astype(BF16) for i in range(DEPTH)],
    }


def _channel_mix(xf, pf, pw, i):
    xf = _mlp_ln(xf, pw['w_mlp_up'][i], pw['w_mlp_down'][i], pw['ln2_g'][i], pw['ln2_b'][i])
    return _ple(xf, pf, pw['w_ple_gate'][i], pw['b_ple_gate'][i], pw['w_ple'][i])


def _trunk(x, p, pos0, past, pw):
    b, t, d = x.shape
    n = b * t
    tm = _row_tile(n)
    xf = x.reshape(n, d)

    h, kv_state = _proj_ab(xf, pw['w_in_ab'], t, pos0)
    h3 = h.reshape(b, t, AB_IN_W)
    s_rows = kv_state.shape[0] // b
    kv_state = kv_state.reshape(b, s_rows, AB_KV_W)
    kb_new = kv_state[:, :, :B_W]
    vb_new = kv_state[:, :, B_W:2 * B_W]
    ka_new = kv_state[:, :, 2 * B_W:2 * B_W + A_KV_W]
    va_new = kv_state[:, :, 2 * B_W + A_KV_W:]
    q_a = (h3, A_Q_W, 0)
    q_b = (h3, B_W, 1)
    if past is None:
        tq = 2 * CHUNK
        k_b, v_b = (h3, B_W, 2), (h3, B_W, 3)
        k_a, v_a = (h3, A_KV_W, (AB_IN_W - 2 * A_KV_W) // A_KV_W), (h3, A_KV_W, (AB_IN_W - A_KV_W) // A_KV_W)
        n_past_a = n_past_b = 0
        ak, av, bk, bv = ka_new[:, -WINDOW:], va_new[:, -WINDOW:], kb_new, vb_new
    else:
        tq = CHUNK
        n_past_a, n_past_b = past[0].shape[1], past[2].shape[1]
        full = [jnp.concatenate([c.reshape(b, c.shape[1], -1), new], axis=1)
                for c, new in zip(past[:4], (ka_new, va_new, kb_new, vb_new))]
        ak, av = full[0][:, -WINDOW:], full[1][:, -WINDOW:]
        bk, bv = full[2][:, -B_BAND_PAST:], full[3][:, -B_BAND_PAST:]
        k_a, v_a, k_b, v_b = [(f.astype(BF16), f.shape[2], 0) for f in full]
    attn_a = _band_attn(q_a, k_a, v_a, heads=A_HEADS, kv_heads=A_KV_HEADS, band=WINDOW, tq=tq, past=n_past_a,
                        sinks=pw['sinks_a'])
    bias_b = _rel_bias(pw['rel_bias_b'], B_BAND_PAST, tq)
    attn_b = _band_attn(q_b, k_b, v_b, heads=B_HEADS, kv_heads=B_HEADS, band=B_BAND_PAST, tq=tq, past=n_past_b,
                        bias=bias_b)
    xf = _outproj_ln([attn_a.reshape(n, A_Q_W), attn_b.reshape(n, B_W)], pw['w_out_ab'], xf,
                     pw['ln1_g'][0], pw['ln1_b'][0])
    xf = _channel_mix(xf, p[0].reshape(n, -1), pw, 0)

    cos_c, sin_c = _rope_tables(t, pos0, C_ROPE, max(t, tm))
    cq, ckv, kr, krp = _proj_c(xf, pw['w_in_c'], pw['g_q_c'], pw['g_kv_c'], cos_c, sin_c)
    q = _q_c(cq, pw['w_q_b_c'], cos_c, sin_c).reshape(b, t, C_HEADS * C_QK)
    if past is None:
        s_len = t
        tk = min(MLA_LONG_BLOCK, t)
        ckv_all, krp_all = ckv, krp
    else:
        past_ckv, past_kr = past[4], past[5]
        tk = MLA_SHORT_BLOCK
        s_len = -(-(past_ckv.shape[1] + t) // tk) * tk
        pad = s_len - past_ckv.shape[1] - t
        ckv_all = jnp.pad(jnp.concatenate([past_ckv, ckv.reshape(b, t, C_KV_RANK)], axis=1),
                          ((0, 0), (0, pad), (0, 0))).reshape(b * s_len, C_KV_RANK)
        past_krp = jnp.pad(past_kr, ((0, 0), (0, 0), (0, LANES - C_ROPE))).astype(BF16)
        krp_all = jnp.pad(jnp.concatenate([past_krp, krp.reshape(b, t, LANES)], axis=1),
                          ((0, 0), (0, pad), (0, 0))).reshape(b * s_len, LANES)
    kt_c, v_c = _kv_c(ckv_all, krp_all, pw['w_k_t_c'], pw['w_v_c'], b, tk)
    v_c = v_c.reshape(b, s_len, C_HEADS * C_V)
    if past is None:
        attn_c = _mla_attn(q, kt_c, v_c, pos0=pos0)
    else:
        attn_c = _mla_attn_short(q, kt_c, v_c, pos0=pos0)
    xf = _outproj_ln([attn_c.reshape(n, C_HEADS * C_V)], pw['w_out_c'], xf, pw['ln1_g'][1], pw['ln1_b'][1])
    xf = _channel_mix(xf, p[1].reshape(n, -1), pw, 1)

    heads4 = lambda a, hh: a.reshape(1, b, a.shape[1], hh, HEAD_DIM)
    return (xf.reshape(b, t, d), heads4(ak, A_KV_HEADS), heads4(av, A_KV_HEADS), heads4(bk, B_HEADS),
            heads4(bv, B_HEADS), ckv.reshape(1, b, t, C_KV_RANK), kr.reshape(1, b, t, C_ROPE))


def kernel(x_prompt, x_sample, cache_a_k, cache_a_v, cache_b_k, cache_b_v, cache_c_kv, cache_c_krope, p_prompt,
           p_sample, w_in_ab, sinks_a, rel_bias_b, w_out_ab, w_in_c, g_q_c, w_q_b_c, g_kv_c, w_kv_b_c, w_out_c,
           ln1_g, ln1_b, ln2_g, ln2_b, w_mlp_up, w_mlp_down, w_ple_gate, b_ple_gate, w_ple):
    pw = _prepare_weights({
        'w_in_ab': w_in_ab, 'sinks_a': sinks_a, 'rel_bias_b': rel_bias_b, 'w_out_ab': w_out_ab,
        'w_in_c': w_in_c, 'g_q_c': g_q_c, 'w_q_b_c': w_q_b_c, 'g_kv_c': g_kv_c, 'w_kv_b_c': w_kv_b_c,
        'w_out_c': w_out_c, 'ln1_g': ln1_g, 'ln1_b': ln1_b, 'ln2_g': ln2_g, 'ln2_b': ln2_b,
        'w_mlp_up': w_mlp_up, 'w_mlp_down': w_mlp_down, 'w_ple_gate': w_ple_gate, 'b_ple_gate': b_ple_gate,
        'w_ple': w_ple,
    })
    prompt = _trunk(x_prompt, p_prompt, 0, None, pw)
    past = (cache_a_k[0], cache_a_v[0], cache_b_k[0], cache_b_v[0], cache_c_kv[0], cache_c_krope[0])
    sample = _trunk(x_sample, p_sample, cache_c_kv.shape[2], past, pw)
    return (prompt[0], sample[0]) + prompt[1:] + sample[1:]
```

```python
import functools

import jax
import jax.numpy as jnp
from jax import lax
from jax.experimental import pallas as pl
from jax.experimental.pallas import tpu as pltpu

F32 = jnp.float32
BF16 = jnp.bfloat16

CHUNK = 64
HEAD_DIM = 128
A_HEADS = 8
A_KV_HEADS = 2
WINDOW = 128
B_HEADS = 8
B_BAND_PAST = 512
REL_CLIP = 128
C_HEADS = 16
C_Q_RANK = 768
C_KV_RANK = 512
C_NOPE = 128
C_ROPE = 64
C_V = 128
DEPTH = 2
ROPE_THETA = 10000.0
LN_EPS = 1e-5
RMS_EPS = 1e-6
NEG_INF = -1e30
DEEPNORM_ALPHA = (2 * DEPTH) ** 0.25

A_Q_W = A_HEADS * HEAD_DIM
A_KV_W = A_KV_HEADS * HEAD_DIM
B_W = B_HEADS * HEAD_DIM
AB_IN_W = A_Q_W + 2 * A_KV_W + 3 * B_W
AB_KV_COL0 = A_Q_W + B_W
AB_KV_W = AB_IN_W - AB_KV_COL0
C_IN_W = C_Q_RANK + C_KV_RANK + C_ROPE
C_QK = 256

LANES = 128
V7X_VMEM_BYTES = 64 * 1024 * 1024
VMEM_LIMIT = V7X_VMEM_BYTES - 8 * 1024 * 1024

ROW_TILE = 512
COL_CHUNK = 512


def _params(n_axes):
    return pltpu.CompilerParams(dimension_semantics=("arbitrary",) * n_axes, vmem_limit_bytes=VMEM_LIMIT)


def _resident(shape):
    nd = len(shape)
    return pl.BlockSpec(shape, lambda *_: (0,) * nd, pipeline_mode=pl.Buffered(1))


def _row_tile(n):
    return ROW_TILE if n % ROW_TILE == 0 else n


def _dot(a, b):
    return jnp.dot(a, b, preferred_element_type=F32)


def _dot_t(a, b):
    return lax.dot_general(a, b, (((1,), (1,)), ((), ())), preferred_element_type=F32)


def _layer_norm(y, g, b):
    mu = jnp.mean(y, -1, keepdims=True)
    var = jnp.mean(jnp.square(y - mu), -1, keepdims=True)
    return (y - mu) * lax.rsqrt(var + LN_EPS) * g + b


def _rms_norm(y, g):
    return y * lax.rsqrt(jnp.mean(jnp.square(y), -1, keepdims=True) + RMS_EPS) * g


def _rope_tile(t, cos, sin, d):
    if d == LANES:
        swapped = pltpu.roll(t, LANES // 2, 1)
    else:
        lane = lax.broadcasted_iota(jnp.int32, t.shape, 1)
        swapped = jnp.where((lane % d) < d // 2, pltpu.roll(t, LANES - d // 2, 1), pltpu.roll(t, d // 2, 1))
    return t * cos + swapped * sin


def _rope_tables(t, pos0, d, rows):
    half = d // 2
    inv = ROPE_THETA ** (-jnp.arange(half, dtype=F32) * (2.0 / d))
    ang = (jnp.arange(t, dtype=F32) + pos0)[:, None] * inv[None, :]
    cos = jnp.cos(ang)
    sin = jnp.sin(ang)
    reps = (rows // t, LANES // d)
    return jnp.tile(jnp.concatenate([cos, cos], 1), reps), jnp.tile(jnp.concatenate([-sin, sin], 1), reps)


def _proj_ab_kernel(x_ref, w_ref, cos_ref, sin_ref, h_ref, kb_ref, vb_ref, ka_ref, va_ref, *, kv_period):
    xb = x_ref[...].astype(BF16)
    cos = cos_ref[...]
    sin = sin_ref[...]
    keep_state = (pl.program_id(0) % kv_period) == kv_period - 1
    rope_tiles = set(range(A_HEADS)) | {(AB_IN_W - 2 * A_KV_W) // LANES + u for u in range(A_KV_HEADS)}
    state_refs = ((AB_KV_COL0, kb_ref), (AB_KV_COL0 + B_W, vb_ref), (AB_KV_COL0 + 2 * B_W, ka_ref),
                  (AB_KV_COL0 + 2 * B_W + A_KV_W, va_ref))
    for j in range(AB_IN_W // COL_CHUNK):
        c0 = j * COL_CHUNK
        acc = _dot(xb, w_ref[:, c0:c0 + COL_CHUNK])
        parts = []
        for u in range(COL_CHUNK // LANES):
            part = acc[:, u * LANES:(u + 1) * LANES]
            if c0 // LANES + u in rope_tiles:
                part = _rope_tile(part, cos, sin, HEAD_DIM)
            parts.append(part)
        acc = jnp.concatenate(parts, axis=1)
        h_ref[:, c0:c0 + COL_CHUNK] = acc.astype(h_ref.dtype)
        for s0, ref in state_refs:
            lo, hi = max(c0, s0), min(c0 + COL_CHUNK, s0 + ref.shape[1])
            if lo < hi:
                @pl.when(keep_state)
                def _(acc=acc, ref=ref, lo=lo, hi=hi, s0=s0, c0=c0):
                    ref[:, lo - s0:hi - s0] = acc[:, lo - c0:hi - c0]


def _proj_ab(xf, w, seq, pos0):
    n, d = xf.shape
    tm = _row_tile(n)
    assert seq % tm == 0 or tm % seq == 0
    kv_period = max(seq // tm, 1)
    assert min(seq, B_BAND_PAST) == min(seq, tm)
    cos, sin = _rope_tables(seq, pos0, HEAD_DIM, max(seq, tm))
    n_tab = cos.shape[0] // tm
    state_widths = (B_W, B_W, A_KV_W, A_KV_W)
    return pl.pallas_call(
        functools.partial(_proj_ab_kernel, kv_period=kv_period),
        grid=(n // tm,),
        in_specs=[
            pl.BlockSpec((tm, d), lambda i: (i, 0)),
            _resident(w.shape),
            pl.BlockSpec((tm, LANES), lambda i: (i % n_tab, 0)),
            pl.BlockSpec((tm, LANES), lambda i: (i % n_tab, 0)),
        ],
        out_specs=[pl.BlockSpec((tm, AB_IN_W), lambda i: (i, 0))] + [
            pl.BlockSpec((tm, sw), lambda i: (i // kv_period, 0)) for sw in state_widths],
        out_shape=[jax.ShapeDtypeStruct((n, AB_IN_W), BF16)] + [
            jax.ShapeDtypeStruct((n // kv_period, sw), F32) for sw in state_widths],
        compiler_params=_params(1),
        name="proj_ab",
    )(xf, w, cos, sin)


def _rel_bias_kernel(tab_ref, o_ref, *, band):
    h = pl.program_id(0)
    shape = o_ref.shape[1:]
    r = lax.broadcasted_iota(jnp.int32, shape, 0)
    w = lax.broadcasted_iota(jnp.int32, shape, 1)
    idx = jnp.clip(band + r - w, -REL_CLIP, REL_CLIP) + REL_CLIP

    def body(d, acc):
        return jnp.where(idx == d, tab_ref[h, d], acc)

    o_ref[0] = lax.fori_loop(0, 2 * REL_CLIP + 1, body, jnp.zeros(shape, F32))


def _rel_bias(table, band, tq):
    heads = table.shape[0]
    return pl.pallas_call(
        functools.partial(_rel_bias_kernel, band=band),
        grid=(heads,),
        in_specs=[pl.BlockSpec(memory_space=pltpu.SMEM)],
        out_specs=pl.BlockSpec((1, tq, band + tq), lambda h: (h, 0, 0)),
        out_shape=jax.ShapeDtypeStruct((heads, tq, band + tq), F32),
        compiler_params=_params(1),
        name="rel_bias",
    )(table)


def _band_attn_kernel(*refs, heads, kv_heads, band, tq, past, has_bias, has_sinks):
    q_ref, k_ref, v_ref = refs[:3]
    rest = list(refs[3:])
    bias_ref = rest.pop(0) if has_bias else None
    sink_ref = rest.pop(0) if has_sinks else None
    o_ref = rest.pop(0)

    width = band + tq
    scale = HEAD_DIM ** -0.5
    ws = past + pl.program_id(1) * tq - band
    r = lax.broadcasted_iota(jnp.int32, (tq, width), 0) // CHUNK
    w = lax.broadcasted_iota(jnp.int32, (tq, width), 1)
    wc = w // CHUNK
    allowed = (wc >= r) & (wc <= r + band // CHUNK) & (w + ws >= 0)

    def window(ref, cols):
        if past >= band:
            return ref[0, pl.ds(pl.multiple_of(ws, CHUNK), width), cols]
        pieces = [ref[0, pl.ds(pl.multiple_of(jnp.maximum(ws + c * LANES, 0), LANES), LANES), cols]
                  for c in range(width // LANES)]
        return jnp.concatenate(pieces, axis=0)

    group = heads // kv_heads
    outs = []
    for kh in range(kv_heads):
        cols = slice(kh * HEAD_DIM, (kh + 1) * HEAD_DIM)
        k_w = window(k_ref, cols)
        v_w = window(v_ref, cols)
        for g in range(group):
            h = kh * group + g
            hcols = slice(h * HEAD_DIM, (h + 1) * HEAD_DIM)
            s = _dot_t(q_ref[0, :, hcols], k_w) * scale
            if has_bias:
                s = s + bias_ref[h]
            s = jnp.where(allowed, s, NEG_INF)
            m = jnp.max(s, -1, keepdims=True)
            if has_sinks:
                sink = sink_ref[h]
                m = jnp.maximum(m, sink)
            e = jnp.exp(s - m)
            den = jnp.sum(e, -1, keepdims=True)
            if has_sinks:
                den = den + jnp.exp(sink - m)
            p = (e * (1.0 / den)).astype(BF16)
            outs.append(_dot(p, v_w).astype(o_ref.dtype))
    o_ref[0] = jnp.concatenate(outs, axis=1)


def _band_attn(q, k, v, *, heads, kv_heads, band, tq, past, bias=None, sinks=None):
    (qa, qw, qi), (ka, kw, ki), (va, vw, vi) = q, k, v
    b, t, _ = qa.shape
    s_len = ka.shape[1]
    assert t % tq == 0 and s_len == past + t and qw == heads * HEAD_DIM and kw == kv_heads * HEAD_DIM
    assert past >= band or (past == 0 and tq % LANES == 0 and band % LANES == 0)
    in_specs = [
        pl.BlockSpec((1, tq, qw), lambda bi, i: (bi, i, qi)),
        pl.BlockSpec((1, s_len, kw), lambda bi, i: (bi, 0, ki)),
        pl.BlockSpec((1, s_len, vw), lambda bi, i: (bi, 0, vi)),
    ]
    args = [qa, ka, va]
    if bias is not None:
        in_specs.append(_resident(bias.shape))
        args.append(bias)
    if sinks is not None:
        in_specs.append(pl.BlockSpec(memory_space=pltpu.SMEM))
        args.append(sinks)
    return pl.pallas_call(
        functools.partial(_band_attn_kernel, heads=heads, kv_heads=kv_heads, band=band, tq=tq, past=past,
                          has_bias=bias is not None, has_sinks=sinks is not None),
        grid=(b, t // tq),
        in_specs=in_specs,
        out_specs=pl.BlockSpec((1, tq, qw), lambda bi, i: (bi, i, 0)),
        out_shape=jax.ShapeDtypeStruct((b, t, qw), BF16),
        compiler_params=_params(2),
        name="band_attn",
    )(*args)


def _outproj_ln_kernel(*refs, n_in):
    a_refs = refs[:n_in]
    w_ref, x_ref, g_ref, b_ref, o_ref = refs[n_in:]
    d_out = o_ref.shape[1]
    for j in range(d_out // COL_CHUNK):
        cols = slice(j * COL_CHUNK, (j + 1) * COL_CHUNK)
        y = DEEPNORM_ALPHA * x_ref[:, cols]
        r0 = 0
        for a_ref in a_refs:
            kk = a_ref.shape[1]
            y = y + _dot(a_ref[...], w_ref[r0:r0 + kk, cols])
            r0 += kk
        o_ref[:, cols] = y
    o_ref[...] = _layer_norm(o_ref[...], g_ref[...], b_ref[...])


def _outproj_ln(a_list, w, xf, g, b):
    n, d = xf.shape
    tm = _row_tile(n)
    assert sum(a.shape[1] for a in a_list) == w.shape[0]
    return pl.pallas_call(
        functools.partial(_outproj_ln_kernel, n_in=len(a_list)),
        grid=(n // tm,),
        in_specs=[pl.BlockSpec((tm, a.shape[1]), lambda i: (i, 0)) for a in a_list] + [
            _resident(w.shape),
            pl.BlockSpec((tm, d), lambda i: (i, 0)),
            _resident(g.shape),
            _resident(b.shape),
        ],
        out_specs=pl.BlockSpec((tm, d), lambda i: (i, 0)),
        out_shape=jax.ShapeDtypeStruct((n, d), F32),
        compiler_params=_params(1),
        name="outproj_ln",
    )(*a_list, w, xf, g, b)


def _mlp_ln_kernel(x_ref, wu_ref, wd_ref, g_ref, b_ref, o_ref, xb_ref):
    f = pl.program_id(1)

    @pl.when(f == 0)
    def _():
        xb_ref[...] = x_ref[...].astype(BF16)
        o_ref[...] = jnp.zeros(o_ref.shape, o_ref.dtype)

    hid = _dot(xb_ref[...], wu_ref[...])
    hid = jnp.square(jnp.maximum(hid, 0.0)).astype(BF16)
    for j in range(o_ref.shape[1] // COL_CHUNK):
        cols = slice(j * COL_CHUNK, (j + 1) * COL_CHUNK)
        o_ref[:, cols] += _dot(hid, wd_ref[:, cols])

    @pl.when(f == pl.num_programs(1) - 1)
    def _():
        o_ref[...] = _layer_norm(DEEPNORM_ALPHA * x_ref[...] + o_ref[...], g_ref[...], b_ref[...])


def _mlp_ln(xf, w_up, w_down, g, b, tf=1024):
    n, d = xf.shape
    d_ff = w_up.shape[1]
    tm = _row_tile(n)
    return pl.pallas_call(
        _mlp_ln_kernel,
        grid=(n // tm, d_ff // tf),
        in_specs=[
            pl.BlockSpec((tm, d), lambda i, f: (i, 0)),
            pl.BlockSpec((d, tf), lambda i, f: (0, f)),
            pl.BlockSpec((tf, d), lambda i, f: (f, 0)),
            _resident(g.shape),
            _resident(b.shape),
        ],
        out_specs=pl.BlockSpec((tm, d), lambda i, f: (i, 0)),
        out_shape=jax.ShapeDtypeStruct((n, d), F32),
        scratch_shapes=[pltpu.VMEM((tm, d), BF16)],
        compiler_params=_params(2),
        name="mlp_ln",
    )(xf, w_up, w_down, g, b)


def _ple_kernel(x_ref, p_ref, wg_ref, bg_ref, wp_ref, o_ref):
    xb = x_ref[...].astype(BF16)
    pb = p_ref[...].astype(BF16)
    for j in range(o_ref.shape[1] // COL_CHUNK):
        cols = slice(j * COL_CHUNK, (j + 1) * COL_CHUNK)
        gate = jax.nn.sigmoid(_dot(xb, wg_ref[:, cols]) + bg_ref[:, cols])
        o_ref[:, cols] = x_ref[:, cols] + gate * _dot(pb, wp_ref[:, cols])


def _ple(xf, p_all, layer, wg, bg, wp):
    n, d = xf.shape
    tm = _row_tile(n)
    return pl.pallas_call(
        _ple_kernel,
        grid=(n // tm,),
        in_specs=[
            pl.BlockSpec((tm, d), lambda i: (i, 0)),
            pl.BlockSpec((None, tm, p_all.shape[2]), lambda i: (layer, i, 0)),
            _resident(wg.shape),
            _resident(bg.shape),
            _resident(wp.shape),
        ],
        out_specs=pl.BlockSpec((tm, d), lambda i: (i, 0)),
        out_shape=jax.ShapeDtypeStruct((n, d), F32),
        compiler_params=_params(1),
        name="ple",
    )(xf, p_all, wg, bg, wp)


def _proj_c_kernel(x_ref, w_ref, gq_ref, gkv_ref, cos_ref, sin_ref, cq_ref, ckv_ref, kr_ref, krp_ref):
    h = _dot(x_ref[...].astype(BF16), w_ref[...])
    cq_ref[...] = _rms_norm(h[:, :C_Q_RANK], gq_ref[...]).astype(cq_ref.dtype)
    ckv_ref[...] = _rms_norm(h[:, C_Q_RANK:C_Q_RANK + C_KV_RANK], gkv_ref[...])
    t = h[:, C_Q_RANK + C_KV_RANK:]
    rot = _rope_tile(t, cos_ref[...], sin_ref[...], C_ROPE)
    lane = lax.broadcasted_iota(jnp.int32, rot.shape, 1)
    rot = jnp.where(lane < C_ROPE, rot, 0.0)
    kr_ref[...] = rot[:, :C_ROPE]
    krp_ref[...] = rot.astype(krp_ref.dtype)


def _proj_c(xf, w, gq, gkv, cos, sin):
    n, d = xf.shape
    tm = _row_tile(n)
    n_tab = cos.shape[0] // tm
    rows = lambda i: (i, 0)
    return pl.pallas_call(
        _proj_c_kernel,
        grid=(n // tm,),
        in_specs=[
            pl.BlockSpec((tm, d), rows),
            _resident(w.shape),
            _resident(gq.shape),
            _resident(gkv.shape),
            pl.BlockSpec((tm, LANES), lambda i: (i % n_tab, 0)),
            pl.BlockSpec((tm, LANES), lambda i: (i % n_tab, 0)),
        ],
        out_specs=[
            pl.BlockSpec((tm, C_Q_RANK), rows),
            pl.BlockSpec((tm, C_KV_RANK), rows),
            pl.BlockSpec((tm, C_ROPE), rows),
            pl.BlockSpec((tm, LANES), rows),
        ],
        out_shape=[
            jax.ShapeDtypeStruct((n, C_Q_RANK), BF16),
            jax.ShapeDtypeStruct((n, C_KV_RANK), F32),
            jax.ShapeDtypeStruct((n, C_ROPE), F32),
            jax.ShapeDtypeStruct((n, LANES), BF16),
        ],
        compiler_params=_params(1),
        name="proj_c",
    )(xf, w, gq, gkv, cos, sin)


def _q_c_kernel(cq_ref, w_ref, cos_ref, sin_ref, q_ref):
    cq = cq_ref[...]
    cos = cos_ref[...]
    sin = sin_ref[...]
    pair_w = 2 * C_NOPE + 2 * C_ROPE
    for p in range(C_HEADS // 2):
        acc = _dot(cq, w_ref[:, p * pair_w:(p + 1) * pair_w])
        rot = _rope_tile(acc[:, 2 * C_NOPE:], cos, sin, C_ROPE)
        lane = lax.broadcasted_iota(jnp.int32, rot.shape, 1)
        o0 = 2 * p * C_QK
        q_ref[:, o0:o0 + C_NOPE] = acc[:, :C_NOPE].astype(q_ref.dtype)
        q_ref[:, o0 + C_NOPE:o0 + C_QK] = jnp.where(lane < C_ROPE, rot, 0.0).astype(q_ref.dtype)
        q_ref[:, o0 + C_QK:o0 + C_QK + C_NOPE] = acc[:, C_NOPE:2 * C_NOPE].astype(q_ref.dtype)
        q_ref[:, o0 + C_QK + C_NOPE:o0 + 2 * C_QK] = jnp.where(
            lane < C_ROPE, pltpu.roll(rot, C_ROPE, 1), 0.0).astype(q_ref.dtype)


def _q_c(cq, w, cos, sin):
    n = cq.shape[0]
    tm = _row_tile(n)
    n_tab = cos.shape[0] // tm
    return pl.pallas_call(
        _q_c_kernel,
        grid=(n // tm,),
        in_specs=[
            pl.BlockSpec((tm, C_Q_RANK), lambda i: (i, 0)),
            _resident(w.shape),
            pl.BlockSpec((tm, LANES), lambda i: (i % n_tab, 0)),
            pl.BlockSpec((tm, LANES), lambda i: (i % n_tab, 0)),
        ],
        out_specs=pl.BlockSpec((tm, C_HEADS * C_QK), lambda i: (i, 0)),
        out_shape=jax.ShapeDtypeStruct((n, C_HEADS * C_QK), BF16),
        compiler_params=_params(1),
        name="q_c",
    )(cq, w, cos, sin)


def _kv_c_kernel(ckv_ref, krp_ref, wkt_ref, wv_ref, kt_ref, v_ref):
    cb = ckv_ref[...].astype(BF16)
    eye = (lax.broadcasted_iota(jnp.int32, (LANES, LANES), 0)
           == lax.broadcasted_iota(jnp.int32, (LANES, LANES), 1)).astype(F32).astype(BF16)
    kr_t = _dot_t(eye, krp_ref[...]).astype(kt_ref.dtype)
    k_t = _dot_t(wkt_ref[...], cb).astype(kt_ref.dtype)
    for h in range(C_HEADS):
        kt_ref[0, 0, h * C_QK:h * C_QK + C_NOPE, :] = k_t[h * C_NOPE:(h + 1) * C_NOPE, :]
        kt_ref[0, 0, h * C_QK + C_NOPE:(h + 1) * C_QK, :] = kr_t
    for j in range(C_HEADS * C_V // COL_CHUNK):
        cols = slice(j * COL_CHUNK, (j + 1) * COL_CHUNK)
        v_ref[:, cols] = _dot(cb, wv_ref[:, cols]).astype(v_ref.dtype)


def _kv_c(ckv, krp, wkt, wv, batch, tk):
    n = ckv.shape[0]
    nkb = n // batch // tk
    assert n == batch * nkb * tk
    rows = lambda i: (i, 0)
    return pl.pallas_call(
        _kv_c_kernel,
        grid=(n // tk,),
        in_specs=[pl.BlockSpec((tk, C_KV_RANK), rows), pl.BlockSpec((tk, LANES), rows), _resident(wkt.shape),
                  _resident(wv.shape)],
        out_specs=[pl.BlockSpec((1, 1, C_HEADS * C_QK, tk), lambda i: (i // nkb, i % nkb, 0, 0)),
                   pl.BlockSpec((tk, C_HEADS * C_V), rows)],
        out_shape=[
            jax.ShapeDtypeStruct((batch, nkb, C_HEADS * C_QK, tk), BF16),
            jax.ShapeDtypeStruct((n, C_HEADS * C_V), BF16),
        ],
        compiler_params=_params(1),
        name="kv_c",
    )(ckv, krp, wkt, wv)


LOG2_E = 1.4426950408889634
MLA_SCALE = (C_NOPE + C_ROPE) ** -0.5
MLA_LONG_BLOCK = 512
MLA_SHORT_BLOCK = 256


def _mla_attn_kernel(q_ref, kt_ref, v_ref, o_ref, *, pos0, heads):
    t = q_ref.shape[1]
    tq = tk = kt_ref.shape[3]
    row = lax.broadcasted_iota(jnp.int32, (tq, tk), 0) // CHUNK
    col = lax.broadcasted_iota(jnp.int32, (tq, tk), 1) // CHUNK
    diag_ok = col <= row
    ones = jnp.ones((tk, C_V), BF16)

    def scores(qs, kb):
        return tuple(_dot(qs[g], kt_ref[0, kb, g * C_QK:(g + 1) * C_QK, :]) for g in range(heads))

    def update(raw, kb, masked, carry):
        k0 = kb * tk
        out = []
        for g in range(heads):
            m, acc = carry[g]
            s = raw[g] * (MLA_SCALE * LOG2_E)
            if masked:
                s = jnp.where(diag_ok, s, NEG_INF)
            m_new = jnp.maximum(m, jnp.max(s, -1, keepdims=True))
            alpha = jnp.exp2(m - m_new)
            p = jnp.exp2(s - m_new).astype(BF16)
            v_ext = jnp.concatenate([v_ref[0, pl.ds(k0, tk), g * C_V:(g + 1) * C_V], ones], axis=1)
            out.append((m_new, alpha * acc + _dot(p, v_ext)))
        return tuple(out)

    for qi in range(t // tq):
        q0 = qi * tq
        qs = [q_ref[0, q0:q0 + tq, g * C_QK:(g + 1) * C_QK] for g in range(heads)]
        n_full = (pos0 + q0) // tk
        state = tuple((jnp.full((tq, 1), NEG_INF, F32), jnp.zeros((tq, 2 * C_V), F32)) for _ in range(heads))
        for kb in range(n_full):
            state = update(scores(qs, kb), kb, False, state)
        state = update(scores(qs, n_full), n_full, True, state)
        for g in range(heads):
            acc = state[g][1]
            o_ref[0, q0:q0 + tq, g * C_V:(g + 1) * C_V] = (acc[:, :C_V] * (1.0 / acc[:, C_V:])).astype(o_ref.dtype)


def _mla_attn(q, kt, v, *, pos0, heads=2):
    b, t, _ = q.shape
    _, nkb, _, tk = kt.shape
    s_len = nkb * tk
    assert t % tk == 0 and pos0 % tk == 0 and pos0 + t <= s_len and v.shape[1] == s_len
    return pl.pallas_call(
        functools.partial(_mla_attn_kernel, pos0=pos0, heads=heads),
        grid=(b, C_HEADS // heads),
        in_specs=[
            pl.BlockSpec((1, t, heads * C_QK), lambda bi, h: (bi, 0, h)),
            pl.BlockSpec((1, nkb, heads * C_QK, tk), lambda bi, h: (bi, 0, h, 0)),
            pl.BlockSpec((1, s_len, heads * C_V), lambda bi, h: (bi, 0, h)),
        ],
        out_specs=pl.BlockSpec((1, t, heads * C_V), lambda bi, h: (bi, 0, h)),
        out_shape=jax.ShapeDtypeStruct((b, t, C_HEADS * C_V), BF16),
        compiler_params=_params(2),
        name="mla_attn",
    )(q, kt, v)


def _mla_attn_short_kernel(q_ref, kt_ref, v_ref, o_ref, *, pos0, heads):
    t, nkb, tk = q_ref.shape[1], kt_ref.shape[1], kt_ref.shape[3]
    s_len = nkb * tk
    row = (lax.broadcasted_iota(jnp.int32, (t, s_len), 0) + pos0) // CHUNK
    col = lax.broadcasted_iota(jnp.int32, (t, s_len), 1) // CHUNK
    allowed = col <= row
    outs = []
    for g in range(heads):
        q = q_ref[0, :, g * C_QK:(g + 1) * C_QK]
        s = jnp.concatenate([_dot(q, kt_ref[0, kb, g * C_QK:(g + 1) * C_QK, :]) for kb in range(nkb)], axis=1)
        s = jnp.where(allowed, s * MLA_SCALE, NEG_INF)
        e = jnp.exp(s - jnp.max(s, -1, keepdims=True))
        p = (e * (1.0 / jnp.sum(e, -1, keepdims=True))).astype(BF16)
        outs.append(_dot(p, v_ref[0, :, g * C_V:(g + 1) * C_V]).astype(o_ref.dtype))
    o_ref[0] = jnp.concatenate(outs, axis=1)


def _mla_attn_short(q, kt, v, *, pos0, heads=4):
    b, t, _ = q.shape
    _, nkb, _, tk = kt.shape
    s_len = nkb * tk
    assert v.shape[1] == s_len and pos0 + t <= s_len
    return pl.pallas_call(
        functools.partial(_mla_attn_short_kernel, pos0=pos0, heads=heads),
        grid=(b, C_HEADS // heads),
        in_specs=[
            pl.BlockSpec((1, t, heads * C_QK), lambda bi, h: (bi, 0, h)),
            pl.BlockSpec((1, nkb, heads * C_QK, tk), lambda bi, h: (bi, 0, h, 0)),
            pl.BlockSpec((1, s_len, heads * C_V), lambda bi, h: (bi, 0, h)),
        ],
        out_specs=pl.BlockSpec((1, t, heads * C_V), lambda bi, h: (bi, 0, h)),
        out_shape=jax.ShapeDtypeStruct((b, t, C_HEADS * C_V), BF16),
        compiler_params=_params(2),
        name="mla_attn_short",
    )(q, kt, v)


def _prepare_weights(w):
    o1 = A_Q_W
    o2 = o1 + A_KV_W
    o3 = o2 + A_KV_W
    o4 = o3 + B_W
    o5 = o4 + B_W
    w_ab = w['w_in_ab'][0]
    w_ab = jnp.concatenate([w_ab[:, :o1], w_ab[:, o3:o4], w_ab[:, o4:o5], w_ab[:, o5:], w_ab[:, o1:o2],
                            w_ab[:, o2:o3]], axis=1)
    w_c = jnp.pad(w['w_in_c'][0], ((0, 0), (0, LANES - C_ROPE)))
    hq = C_NOPE + C_ROPE
    q_cols = []
    for p in range(C_HEADS // 2):
        h0, h1 = 2 * p, 2 * p + 1
        q_cols += [jnp.arange(h0 * hq, h0 * hq + C_NOPE), jnp.arange(h1 * hq, h1 * hq + C_NOPE),
                   jnp.arange(h0 * hq + C_NOPE, (h0 + 1) * hq), jnp.arange(h1 * hq + C_NOPE, (h1 + 1) * hq)]
    w_q = w['w_q_b_c'][0][:, jnp.concatenate(q_cols)]
    hkv = C_NOPE + C_V
    k_cols = jnp.concatenate([jnp.arange(h * hkv, h * hkv + C_NOPE) for h in range(C_HEADS)])
    v_cols = jnp.concatenate([jnp.arange(h * hkv + C_NOPE, (h + 1) * hkv) for h in range(C_HEADS)])
    w_k_t = w['w_kv_b_c'][0][:, k_cols].T
    w_v = w['w_kv_b_c'][0][:, v_cols]
    row = lambda a: a.reshape(1, -1)
    return {
        'w_in_ab': w_ab.astype(BF16), 'w_out_ab': w['w_out_ab'][0].astype(BF16),
        'w_in_c': w_c.astype(BF16), 'w_q_b_c': w_q.astype(BF16), 'w_k_t_c': w_k_t.astype(BF16),
        'w_v_c': w_v.astype(BF16),
        'w_out_c': w['w_out_c'][0].astype(BF16),
        'g_q_c': row(w['g_q_c'][0]), 'g_kv_c': row(w['g_kv_c'][0]),
        'sinks_a': w['sinks_a'][0], 'rel_bias_b': w['rel_bias_b'][0],
        'ln1_g': [row(w['ln1_g'][i]) for i in range(DEPTH)], 'ln1_b': [row(w['ln1_b'][i]) for i in range(DEPTH)],
        'ln2_g': [row(w['ln2_g'][i]) for i in range(DEPTH)], 'ln2_b': [row(w['ln2_b'][i]) for i in range(DEPTH)],
        'w_mlp_up': [w['w_mlp_up'][i].astype(BF16) for i in range(DEPTH)],
        'w_mlp_down': [w['w_mlp_down'][i].astype(BF16) for i in range(DEPTH)],
        'w_ple_gate': [w['w_ple_gate'][i].astype(BF16) for i in range(DEPTH)],
        'b_ple_gate': [row(w['b_ple_gate'][i]) for i in range(DEPTH)],
        'w_ple': [w['w_ple'][i].astype(BF16) for i in range(DEPTH)],
    }


def _channel_mix(xf, p_all, pw, i):
    xf = _mlp_ln(xf, pw['w_mlp_up'][i], pw['w_mlp_down'][i], pw['ln2_g'][i], pw['ln2_b'][i])
    return _ple(xf, p_all, i, pw['w_ple_gate'][i], pw['b_ple_gate'][i], pw['w_ple'][i])


def _trunk(x, p, pos0, past, pw):
    b, t, d = x.shape
    n = b * t
    tm = _row_tile(n)
    xf = x.reshape(n, d)

    h, kb_new, vb_new, ka_new, va_new = _proj_ab(xf, pw['w_in_ab'], t, pos0)
    h3 = h.reshape(b, t, AB_IN_W)
    kb_new, vb_new, ka_new, va_new = [a.reshape(b, a.shape[0] // b, a.shape[1])
                                      for a in (kb_new, vb_new, ka_new, va_new)]
    q_a = (h3, A_Q_W, 0)
    q_b = (h3, B_W, 1)
    if past is None:
        tq = 2 * CHUNK
        k_b, v_b = (h3, B_W, 2), (h3, B_W, 3)
        k_a, v_a = (h3, A_KV_W, (AB_IN_W - 2 * A_KV_W) // A_KV_W), (h3, A_KV_W, (AB_IN_W - A_KV_W) // A_KV_W)
        n_past_a = n_past_b = 0
        ak, av, bk, bv = ka_new[:, -WINDOW:], va_new[:, -WINDOW:], kb_new, vb_new
    else:
        tq = CHUNK
        n_past_a, n_past_b = past[0].shape[1], past[2].shape[1]
        full = [jnp.concatenate([c.reshape(b, c.shape[1], -1), new], axis=1)
                for c, new in zip(past[:4], (ka_new, va_new, kb_new, vb_new))]
        ak, av = full[0][:, -WINDOW:], full[1][:, -WINDOW:]
        bk, bv = full[2][:, -B_BAND_PAST:], full[3][:, -B_BAND_PAST:]
        k_a, v_a, k_b, v_b = [(f.astype(BF16), f.shape[2], 0) for f in full]
    attn_a = _band_attn(q_a, k_a, v_a, heads=A_HEADS, kv_heads=A_KV_HEADS, band=WINDOW, tq=tq, past=n_past_a,
                        sinks=pw['sinks_a'])
    bias_b = _rel_bias(pw['rel_bias_b'], B_BAND_PAST, tq)
    attn_b = _band_attn(q_b, k_b, v_b, heads=B_HEADS, kv_heads=B_HEADS, band=B_BAND_PAST, tq=tq, past=n_past_b,
                        bias=bias_b)
    xf = _outproj_ln([attn_a.reshape(n, A_Q_W), attn_b.reshape(n, B_W)], pw['w_out_ab'], xf,
                     pw['ln1_g'][0], pw['ln1_b'][0])
    p_all = p.reshape(p.shape[0], n, p.shape[3])
    xf = _channel_mix(xf, p_all, pw, 0)

    cos_c, sin_c = _rope_tables(t, pos0, C_ROPE, max(t, tm))
    cq, ckv, kr, krp = _proj_c(xf, pw['w_in_c'], pw['g_q_c'], pw['g_kv_c'], cos_c, sin_c)
    q = _q_c(cq, pw['w_q_b_c'], cos_c, sin_c).reshape(b, t, C_HEADS * C_QK)
    if past is None:
        s_len = t
        tk = min(MLA_LONG_BLOCK, t)
        ckv_all, krp_all = ckv, krp
    else:
        past_ckv, past_kr = past[4], past[5]
        tk = MLA_SHORT_BLOCK
        s_len = -(-(past_ckv.shape[1] + t) // tk) * tk
        pad = s_len - past_ckv.shape[1] - t
        ckv_all = jnp.pad(jnp.concatenate([past_ckv, ckv.reshape(b, t, C_KV_RANK)], axis=1),
                          ((0, 0), (0, pad), (0, 0))).reshape(b * s_len, C_KV_RANK)
        past_krp = jnp.pad(past_kr, ((0, 0), (0, 0), (0, LANES - C_ROPE))).astype(BF16)
        krp_all = jnp.pad(jnp.concatenate([past_krp, krp.reshape(b, t, LANES)], axis=1),
                          ((0, 0), (0, pad), (0, 0))).reshape(b * s_len, LANES)
    kt_c, v_c = _kv_c(ckv_all, krp_all, pw['w_k_t_c'], pw['w_v_c'], b, tk)
    v_c = v_c.reshape(b, s_len, C_HEADS * C_V)
    if past is None:
        attn_c = _mla_attn(q, kt_c, v_c, pos0=pos0)
    else:
        attn_c = _mla_attn_short(q, kt_c, v_c, pos0=pos0)
    xf = _outproj_ln([attn_c.reshape(n, C_HEADS * C_V)], pw['w_out_c'], xf, pw['ln1_g'][1], pw['ln1_b'][1])
    xf = _channel_mix(xf, p_all, pw, 1)

    heads4 = lambda a, hh: a.reshape(1, b, a.shape[1], hh, HEAD_DIM)
    return (xf.reshape(b, t, d), heads4(ak, A_KV_HEADS), heads4(av, A_KV_HEADS), heads4(bk, B_HEADS),
            heads4(bv, B_HEADS), ckv.reshape(1, b, t, C_KV_RANK), kr.reshape(1, b, t, C_ROPE))


def kernel(x_prompt, x_sample, cache_a_k, cache_a_v, cache_b_k, cache_b_v, cache_c_kv, cache_c_krope, p_prompt,
           p_sample, w_in_ab, sinks_a, rel_bias_b, w_out_ab, w_in_c, g_q_c, w_q_b_c, g_kv_c, w_kv_b_c, w_out_c,
           ln1_g, ln1_b, ln2_g, ln2_b, w_mlp_up, w_mlp_down, w_ple_gate, b_ple_gate, w_ple):
    pw = _prepare_weights({
        'w_in_ab': w_in_ab, 'sinks_a': sinks_a, 'rel_bias_b': rel_bias_b, 'w_out_ab': w_out_ab,
        'w_in_c': w_in_c, 'g_q_c': g_q_c, 'w_q_b_c': w_q_b_c, 'g_kv_c': g_kv_c, 'w_kv_b_c': w_kv_b_c,
        'w_out_c': w_out_c, 'ln1_g': ln1_g, 'ln1_b': ln1_b, 'ln2_g': ln2_g, 'ln2_b': ln2_b,
        'w_mlp_up': w_mlp_up, 'w_mlp_down': w_mlp_down, 'w_ple_gate': w_ple_gate, 'b_ple_gate': b_ple_gate,
        'w_ple': w_ple,
    })
    prompt = _trunk(x_prompt, p_prompt, 0, None, pw)
    past = (cache_a_k[0], cache_a_v[0], cache_b_k[0], cache_b_v[0], cache_c_kv[0], cache_c_krope[0])
    sample = _trunk(x_sample, p_sample, cache_c_kv.shape[2], past, pw)
    return (prompt[0], sample[0]) + prompt[1:] + sample[1:]
```

```python
import functools

import jax
import jax.numpy as jnp
from jax import lax
from jax.experimental import pallas as pl
from jax.experimental.pallas import tpu as pltpu

F32 = jnp.float32
BF16 = jnp.bfloat16

CHUNK = 64
HEAD_DIM = 128
A_HEADS = 8
A_KV_HEADS = 2
WINDOW = 128
B_HEADS = 8
B_BAND_PAST = 512
REL_CLIP = 128
C_HEADS = 16
C_Q_RANK = 768
C_KV_RANK = 512
C_NOPE = 128
C_ROPE = 64
C_V = 128
DEPTH = 2
ROPE_THETA = 10000.0
LN_EPS = 1e-5
RMS_EPS = 1e-6
NEG_INF = -1e30
DEEPNORM_ALPHA = (2 * DEPTH) ** 0.25
LOG2_E = 1.4426950408889634

A_Q_W = A_HEADS * HEAD_DIM
A_KV_W = A_KV_HEADS * HEAD_DIM
B_W = B_HEADS * HEAD_DIM
AB_IN_W = A_Q_W + 2 * A_KV_W + 3 * B_W
AB_KV_COL0 = A_Q_W + B_W
AB_KV_W = AB_IN_W - AB_KV_COL0
C_IN_W = C_Q_RANK + C_KV_RANK + C_ROPE
C_QK = 256

LANES = 128
V7X_VMEM_BYTES = 64 * 1024 * 1024
VMEM_LIMIT = V7X_VMEM_BYTES - 8 * 1024 * 1024

ROW_TILE = 512
COL_CHUNK = 512
MLP_ROW_TILE = 1024
MLP_FF_TILE = 512
BAND_T_BLOCK = 256


def _params(n_axes):
    return pltpu.CompilerParams(dimension_semantics=("arbitrary",) * n_axes, vmem_limit_bytes=VMEM_LIMIT)


def _resident(shape):
    nd = len(shape)
    return pl.BlockSpec(shape, lambda *_: (0,) * nd, pipeline_mode=pl.Buffered(1))


def _row_tile(n):
    return ROW_TILE if n % ROW_TILE == 0 else n


def _dot(a, b):
    return jnp.dot(a, b, preferred_element_type=F32)


def _dot_t(a, b):
    return lax.dot_general(a, b, (((1,), (1,)), ((), ())), preferred_element_type=F32)


def _layer_norm(y, g, b):
    mu = jnp.mean(y, -1, keepdims=True)
    var = jnp.mean(jnp.square(y - mu), -1, keepdims=True)
    return (y - mu) * lax.rsqrt(var + LN_EPS) * g + b


def _rms_norm(y, g):
    return y * lax.rsqrt(jnp.mean(jnp.square(y), -1, keepdims=True) + RMS_EPS) * g


def _rope_tile(t, cos, sin, d):
    if d == LANES:
        swapped = pltpu.roll(t, LANES // 2, 1)
    else:
        lane = lax.broadcasted_iota(jnp.int32, t.shape, 1)
        swapped = jnp.where((lane % d) < d // 2, pltpu.roll(t, LANES - d // 2, 1), pltpu.roll(t, d // 2, 1))
    return t * cos + swapped * sin


def _rope_tables(t, pos0, d, rows):
    half = d // 2
    inv = ROPE_THETA ** (-jnp.arange(half, dtype=F32) * (2.0 / d))
    ang = (jnp.arange(t, dtype=F32) + pos0)[:, None] * inv[None, :]
    cos = jnp.cos(ang)
    sin = jnp.sin(ang)
    reps = (rows // t, LANES // d)
    return jnp.tile(jnp.concatenate([cos, cos], 1), reps), jnp.tile(jnp.concatenate([-sin, sin], 1), reps)


def _proj_ab_kernel(*refs, kv_period, kb_transposed):
    refs = list(refs)
    x_ref, w_ref = refs[:2]
    wkbt_ref = refs.pop(2) if kb_transposed else None
    cos_ref, sin_ref, h_ref, kb_ref, vb_ref, ka_ref, va_ref = refs[2:9]
    kbt_ref = refs[9] if kb_transposed else None
    xb = x_ref[...].astype(BF16)
    cos = cos_ref[...]
    sin = sin_ref[...]
    keep_state = (pl.program_id(0) % kv_period) == kv_period - 1
    kv0 = AB_KV_COL0
    state_refs = []
    if not kb_transposed:
        state_refs.append((kv0, kb_ref))
        kv0 += B_W
    state_refs += [(kv0, vb_ref), (kv0 + B_W, ka_ref), (kv0 + B_W + A_KV_W, va_ref)]
    rope_tiles = set(range(A_HEADS)) | {(kv0 + B_W) // LANES + u for u in range(A_KV_HEADS)}
    for j in range(w_ref.shape[1] // COL_CHUNK):
        c0 = j * COL_CHUNK
        acc = _dot(xb, w_ref[:, c0:c0 + COL_CHUNK])
        parts = []
        for u in range(COL_CHUNK // LANES):
            part = acc[:, u * LANES:(u + 1) * LANES]
            if c0 // LANES + u in rope_tiles:
                part = _rope_tile(part, cos, sin, HEAD_DIM)
            parts.append(part)
        acc = jnp.concatenate(parts, axis=1)
        h_ref[:, c0:c0 + COL_CHUNK] = acc.astype(h_ref.dtype)
        for s0, ref in state_refs:
            lo, hi = max(c0, s0), min(c0 + COL_CHUNK, s0 + ref.shape[1])
            if lo < hi:
                @pl.when(keep_state)
                def _(acc=acc, ref=ref, lo=lo, hi=hi, s0=s0, c0=c0):
                    ref[:, lo - s0:hi - s0] = acc[:, lo - c0:hi - c0]
    if kb_transposed:
        kb_t = _dot_t(wkbt_ref[...], xb)
        blk = kbt_ref.shape[3]
        for c in range(kbt_ref.shape[1]):
            kbt_ref[0, c] = kb_t[:, c * blk:(c + 1) * blk].astype(kbt_ref.dtype)

        @pl.when(keep_state)
        def _():
            kb_ref[...] = kb_t.T


def _proj_ab(xf, w, seq, pos0, w_kb_t=None, kt_block=None):
    n, d = xf.shape
    tm = _row_tile(n)
    assert seq % tm == 0 or tm % seq == 0
    kv_period = max(seq // tm, 1)
    assert min(seq, B_BAND_PAST) == min(seq, tm)
    cos, sin = _rope_tables(seq, pos0, HEAD_DIM, max(seq, tm))
    n_tab = cos.shape[0] // tm
    state_widths = (B_W, B_W, A_KV_W, A_KV_W)
    kb_transposed = w_kb_t is not None
    rows = lambda i: (i, 0)
    tab = lambda i: (i % n_tab, 0)
    in_specs = [pl.BlockSpec((tm, d), rows), _resident(w.shape)]
    args = [xf, w]
    out_specs = [pl.BlockSpec((tm, w.shape[1]), rows)] + [
        pl.BlockSpec((tm, sw), lambda i: (i // kv_period, 0)) for sw in state_widths]
    out_shape = [jax.ShapeDtypeStruct((n, w.shape[1]), BF16)] + [
        jax.ShapeDtypeStruct((n // kv_period, sw), F32) for sw in state_widths]
    if kb_transposed:
        assert seq % tm == 0 and tm % kt_block == 0
        tiles = seq // tm
        in_specs.append(_resident(w_kb_t.shape))
        args.append(w_kb_t)
        out_specs.append(pl.BlockSpec((1, tm // kt_block, B_W, kt_block), lambda i: (i // tiles, i % tiles, 0, 0)))
        out_shape.append(jax.ShapeDtypeStruct((n // seq, seq // kt_block, B_W, kt_block), BF16))
    in_specs += [pl.BlockSpec((tm, LANES), tab), pl.BlockSpec((tm, LANES), tab)]
    args += [cos, sin]
    return pl.pallas_call(
        functools.partial(_proj_ab_kernel, kv_period=kv_period, kb_transposed=kb_transposed),
        grid=(n // tm,),
        in_specs=in_specs,
        out_specs=out_specs,
        out_shape=out_shape,
        compiler_params=_params(1),
        name="proj_ab",
    )(*args)


def _rel_bias_kernel(tab_ref, o_ref, *, band):
    h = pl.program_id(0)
    shape = o_ref.shape[1:]
    r = lax.broadcasted_iota(jnp.int32, shape, 0)
    w = lax.broadcasted_iota(jnp.int32, shape, 1)
    idx = jnp.clip(band + r - w, -REL_CLIP, REL_CLIP) + REL_CLIP

    def body(d, acc):
        return jnp.where(idx == d, tab_ref[h, d], acc)

    o_ref[0] = lax.fori_loop(0, 2 * REL_CLIP + 1, body, jnp.zeros(shape, F32))


def _rel_bias(table, band, tq):
    heads = table.shape[0]
    return pl.pallas_call(
        functools.partial(_rel_bias_kernel, band=band),
        grid=(heads,),
        in_specs=[pl.BlockSpec(memory_space=pltpu.SMEM)],
        out_specs=pl.BlockSpec((1, tq, band + tq), lambda h: (h, 0, 0)),
        out_shape=jax.ShapeDtypeStruct((heads, tq, band + tq), F32),
        compiler_params=_params(1),
        name="rel_bias",
    )(table)


def _band_attn_kernel(*refs, heads, kv_heads, band, tq, past, has_bias, has_sinks):
    q_ref, k_ref, v_ref = refs[:3]
    rest = list(refs[3:])
    bias_ref = rest.pop(0) if has_bias else None
    sink_ref = rest.pop(0) if has_sinks else None
    o_ref = rest.pop(0)

    width = band + tq
    scale = HEAD_DIM ** -0.5
    ws = past + pl.program_id(1) * tq - band
    r = lax.broadcasted_iota(jnp.int32, (tq, width), 0) // CHUNK
    w = lax.broadcasted_iota(jnp.int32, (tq, width), 1)
    wc = w // CHUNK
    allowed = (wc >= r) & (wc <= r + band // CHUNK) & (w + ws >= 0)

    def window(ref, cols):
        if past >= band:
            return ref[0, pl.ds(pl.multiple_of(ws, CHUNK), width), cols]
        pieces = [ref[0, pl.ds(pl.multiple_of(jnp.maximum(ws + c * LANES, 0), LANES), LANES), cols]
                  for c in range(width // LANES)]
        return jnp.concatenate(pieces, axis=0)

    group = heads // kv_heads
    outs = []
    for kh in range(kv_heads):
        cols = slice(kh * HEAD_DIM, (kh + 1) * HEAD_DIM)
        k_w = window(k_ref, cols)
        v_w = window(v_ref, cols)
        for g in range(group):
            h = kh * group + g
            hcols = slice(h * HEAD_DIM, (h + 1) * HEAD_DIM)
            s = _dot_t(q_ref[0, :, hcols], k_w) * scale
            if has_bias:
                s = s + bias_ref[h]
            s = jnp.where(allowed, s, NEG_INF)
            m = jnp.max(s, -1, keepdims=True)
            if has_sinks:
                sink = sink_ref[h]
                m = jnp.maximum(m, sink)
            e = jnp.exp(s - m)
            den = jnp.sum(e, -1, keepdims=True)
            if has_sinks:
                den = den + jnp.exp(sink - m)
            p = (e * (1.0 / den)).astype(BF16)
            outs.append(_dot(p, v_w).astype(o_ref.dtype))
    o_ref[0] = jnp.concatenate(outs, axis=1)


def _band_attn(q, k, v, *, heads, kv_heads, band, tq, past, bias=None, sinks=None):
    (qa, qw, qi), (ka, kw, ki), (va, vw, vi) = q, k, v
    b, t, _ = qa.shape
    s_len = ka.shape[1]
    assert t % tq == 0 and s_len == past + t and qw == heads * HEAD_DIM and kw == kv_heads * HEAD_DIM
    assert past >= band or (past == 0 and tq % LANES == 0 and band % LANES == 0)
    in_specs = [
        pl.BlockSpec((1, tq, qw), lambda bi, i: (bi, i, qi)),
        pl.BlockSpec((1, s_len, kw), lambda bi, i: (bi, 0, ki)),
        pl.BlockSpec((1, s_len, vw), lambda bi, i: (bi, 0, vi)),
    ]
    args = [qa, ka, va]
    if bias is not None:
        in_specs.append(_resident(bias.shape))
        args.append(bias)
    if sinks is not None:
        in_specs.append(pl.BlockSpec(memory_space=pltpu.SMEM))
        args.append(sinks)
    return pl.pallas_call(
        functools.partial(_band_attn_kernel, heads=heads, kv_heads=kv_heads, band=band, tq=tq, past=past,
                          has_bias=bias is not None, has_sinks=sinks is not None),
        grid=(b, t // tq),
        in_specs=in_specs,
        out_specs=pl.BlockSpec((1, tq, qw), lambda bi, i: (bi, i, 0)),
        out_shape=jax.ShapeDtypeStruct((b, t, qw), BF16),
        compiler_params=_params(2),
        name="band_attn",
    )(*args)


def _rel_bias_folded_kernel(tab_ref, o_ref, *, band):
    h = pl.program_id(0)
    tq, width = o_ref.shape[1:]
    period = tq + width
    j = lax.broadcasted_iota(jnp.int32, (8, period), 1)
    dist = jnp.where(j < width, band - j, band - (j - period))
    idx = jnp.clip(dist, -REL_CLIP, REL_CLIP) + REL_CLIP

    def body(d, acc):
        return jnp.where(idx == d, tab_ref[h, d], acc)

    g = lax.fori_loop(0, 2 * REL_CLIP + 1, body, jnp.zeros((8, period), F32))
    full = jnp.concatenate([g] * (tq // 8), axis=0)
    bias = pltpu.roll(full, 0, 1, stride=1, stride_axis=0)[:, :width]
    rc = lax.broadcasted_iota(jnp.int32, (tq, width), 0) // CHUNK
    wc = lax.broadcasted_iota(jnp.int32, (tq, width), 1) // CHUNK
    o_ref[0] = jnp.where((wc >= rc) & (wc <= rc + band // CHUNK), bias * LOG2_E, NEG_INF)


def _rel_bias_folded(table, band, tq):
    heads = table.shape[0]
    assert (band + 2 * tq) % LANES == 0 and tq % 8 == 0
    return pl.pallas_call(
        functools.partial(_rel_bias_folded_kernel, band=band),
        grid=(heads,),
        in_specs=[pl.BlockSpec(memory_space=pltpu.SMEM)],
        out_specs=pl.BlockSpec((1, tq, band + tq), lambda h: (h, 0, 0)),
        out_shape=jax.ShapeDtypeStruct((heads, tq, band + tq), F32),
        compiler_params=_params(1),
        name="rel_bias_folded",
    )(table)


def _band_attn_t_kernel(q_ref, kt_ref, v_ref, bias_ref, o_ref, *, heads, band):
    tq = kt_ref.shape[3]
    n_past = band // tq
    width = band + tq
    i = pl.program_id(1)
    ws = i * tq - band
    col_valid = (lax.broadcasted_iota(jnp.int32, (1, width), 1) + ws) >= 0
    ones = jnp.ones((width, HEAD_DIM), BF16)
    outs = []
    for h in range(heads):
        hc = slice(h * HEAD_DIM, (h + 1) * HEAD_DIM)
        q_h = q_ref[0, :, hc]
        s = jnp.concatenate([_dot(q_h, kt_ref[0, jnp.maximum(i - n_past + c, 0), hc, :])
                             for c in range(n_past + 1)], axis=1)
        s = s * (HEAD_DIM ** -0.5 * LOG2_E) + bias_ref[h]
        s = jnp.where(col_valid, s, NEG_INF)
        p = jnp.exp2(s - jnp.max(s, -1, keepdims=True)).astype(BF16)
        v_w = jnp.concatenate([v_ref[0, pl.ds(pl.multiple_of(jnp.maximum(ws + c * tq, 0), tq), tq), hc]
                               for c in range(n_past + 1)], axis=0)
        o_ext = _dot(p, jnp.concatenate([v_w, ones], axis=1))
        outs.append((o_ext[:, :HEAD_DIM] * (1.0 / o_ext[:, HEAD_DIM:])).astype(o_ref.dtype))
    o_ref[0] = jnp.concatenate(outs, axis=1)


def _band_attn_t(q, kt, v, bias, *, heads, band):
    (qa, qw, qi), (va, vw, vi) = q, v
    b, t, _ = qa.shape
    _, nkb, _, tq = kt.shape
    assert nkb * tq == t and band % tq == 0 and qw == vw == heads * HEAD_DIM
    return pl.pallas_call(
        functools.partial(_band_attn_t_kernel, heads=heads, band=band),
        grid=(b, t // tq),
        in_specs=[
            pl.BlockSpec((1, tq, qw), lambda bi, i: (bi, i, qi)),
            pl.BlockSpec((1, nkb, heads * HEAD_DIM, tq), lambda bi, i: (bi, 0, 0, 0)),
            pl.BlockSpec((1, t, vw), lambda bi, i: (bi, 0, vi)),
            _resident(bias.shape),
        ],
        out_specs=pl.BlockSpec((1, tq, qw), lambda bi, i: (bi, i, 0)),
        out_shape=jax.ShapeDtypeStruct((b, t, qw), BF16),
        compiler_params=_params(2),
        name="band_attn_t",
    )(qa, kt, va, bias)


def _outproj_ln_kernel(*refs, n_in):
    a_refs = refs[:n_in]
    w_ref, x_ref, g_ref, b_ref, o_ref = refs[n_in:]
    d_out = o_ref.shape[1]
    for j in range(d_out // COL_CHUNK):
        cols = slice(j * COL_CHUNK, (j + 1) * COL_CHUNK)
        y = DEEPNORM_ALPHA * x_ref[:, cols]
        r0 = 0
        for a_ref in a_refs:
            kk = a_ref.shape[1]
            y = y + _dot(a_ref[...], w_ref[r0:r0 + kk, cols])
            r0 += kk
        o_ref[:, cols] = y
    o_ref[...] = _layer_norm(o_ref[...], g_ref[...], b_ref[...])


def _outproj_ln(a_list, w, xf, g, b):
    n, d = xf.shape
    tm = _row_tile(n)
    assert sum(a.shape[1] for a in a_list) == w.shape[0]
    return pl.pallas_call(
        functools.partial(_outproj_ln_kernel, n_in=len(a_list)),
        grid=(n // tm,),
        in_specs=[pl.BlockSpec((tm, a.shape[1]), lambda i: (i, 0)) for a in a_list] + [
            _resident(w.shape),
            pl.BlockSpec((tm, d), lambda i: (i, 0)),
            _resident(g.shape),
            _resident(b.shape),
        ],
        out_specs=pl.BlockSpec((tm, d), lambda i: (i, 0)),
        out_shape=jax.ShapeDtypeStruct((n, d), F32),
        compiler_params=_params(1),
        name="outproj_ln",
    )(*a_list, w, xf, g, b)


def _mlp_ln_kernel(x_ref, wu_ref, wd_ref, g_ref, b_ref, o_ref, xb_ref):
    f = pl.program_id(1)

    @pl.when(f == 0)
    def _():
        xb_ref[...] = x_ref[...].astype(BF16)
        o_ref[...] = jnp.zeros(o_ref.shape, o_ref.dtype)

    hid = _dot(xb_ref[...], wu_ref[...])
    hid = jnp.square(jnp.maximum(hid, 0.0)).astype(BF16)
    for j in range(o_ref.shape[1] // COL_CHUNK):
        cols = slice(j * COL_CHUNK, (j + 1) * COL_CHUNK)
        o_ref[:, cols] += _dot(hid, wd_ref[:, cols])

    @pl.when(f == pl.num_programs(1) - 1)
    def _():
        o_ref[...] = _layer_norm(DEEPNORM_ALPHA * x_ref[...] + o_ref[...], g_ref[...], b_ref[...])


def _mlp_ln(xf, w_up, w_down, g, b):
    n, d = xf.shape
    d_ff = w_up.shape[1]
    tm = MLP_ROW_TILE if n % MLP_ROW_TILE == 0 else n
    tf = MLP_FF_TILE
    return pl.pallas_call(
        _mlp_ln_kernel,
        grid=(n // tm, d_ff // tf),
        in_specs=[
            pl.BlockSpec((tm, d), lambda i, f: (i, 0)),
            pl.BlockSpec((d, tf), lambda i, f: (0, f)),
            pl.BlockSpec((tf, d), lambda i, f: (f, 0)),
            _resident(g.shape),
            _resident(b.shape),
        ],
        out_specs=pl.BlockSpec((tm, d), lambda i, f: (i, 0)),
        out_shape=jax.ShapeDtypeStruct((n, d), F32),
        scratch_shapes=[pltpu.VMEM((tm, d), BF16)],
        compiler_params=_params(2),
        name="mlp_ln",
    )(xf, w_up, w_down, g, b)


def _ple_kernel(x_ref, p_ref, wg_ref, bg_ref, wp_ref, o_ref):
    xb = x_ref[...].astype(BF16)
    pb = p_ref[...].astype(BF16)
    for j in range(o_ref.shape[1] // COL_CHUNK):
        cols = slice(j * COL_CHUNK, (j + 1) * COL_CHUNK)
        gate = jax.nn.sigmoid(_dot(xb, wg_ref[:, cols]) + bg_ref[:, cols])
        o_ref[:, cols] = x_ref[:, cols] + gate * _dot(pb, wp_ref[:, cols])


def _ple(xf, p_all, layer, wg, bg, wp):
    n, d = xf.shape
    tm = _row_tile(n)
    return pl.pallas_call(
        _ple_kernel,
        grid=(n // tm,),
        in_specs=[
            pl.BlockSpec((tm, d), lambda i: (i, 0)),
            pl.BlockSpec((None, tm, p_all.shape[2]), lambda i: (layer, i, 0)),
            _resident(wg.shape),
            _resident(bg.shape),
            _resident(wp.shape),
        ],
        out_specs=pl.BlockSpec((tm, d), lambda i: (i, 0)),
        out_shape=jax.ShapeDtypeStruct((n, d), F32),
        compiler_params=_params(1),
        name="ple",
    )(xf, p_all, wg, bg, wp)


def _proj_c_kernel(x_ref, w_ref, gq_ref, gkv_ref, cos_ref, sin_ref, cq_ref, ckv_ref, kr_ref, krp_ref):
    h = _dot(x_ref[...].astype(BF16), w_ref[...])
    cq_ref[...] = _rms_norm(h[:, :C_Q_RANK], gq_ref[...]).astype(cq_ref.dtype)
    ckv_ref[...] = _rms_norm(h[:, C_Q_RANK:C_Q_RANK + C_KV_RANK], gkv_ref[...])
    t = h[:, C_Q_RANK + C_KV_RANK:]
    rot = _rope_tile(t, cos_ref[...], sin_ref[...], C_ROPE)
    lane = lax.broadcasted_iota(jnp.int32, rot.shape, 1)
    rot = jnp.where(lane < C_ROPE, rot, 0.0)
    kr_ref[...] = rot[:, :C_ROPE]
    krp_ref[...] = rot.astype(krp_ref.dtype)


def _proj_c(xf, w, gq, gkv, cos, sin):
    n, d = xf.shape
    tm = _row_tile(n)
    n_tab = cos.shape[0] // tm
    rows = lambda i: (i, 0)
    return pl.pallas_call(
        _proj_c_kernel,
        grid=(n // tm,),
        in_specs=[
            pl.BlockSpec((tm, d), rows),
            _resident(w.shape),
            _resident(gq.shape),
            _resident(gkv.shape),
            pl.BlockSpec((tm, LANES), lambda i: (i % n_tab, 0)),
            pl.BlockSpec((tm, LANES), lambda i: (i % n_tab, 0)),
        ],
        out_specs=[
            pl.BlockSpec((tm, C_Q_RANK), rows),
            pl.BlockSpec((tm, C_KV_RANK), rows),
            pl.BlockSpec((tm, C_ROPE), rows),
            pl.BlockSpec((tm, LANES), rows),
        ],
        out_shape=[
            jax.ShapeDtypeStruct((n, C_Q_RANK), BF16),
            jax.ShapeDtypeStruct((n, C_KV_RANK), F32),
            jax.ShapeDtypeStruct((n, C_ROPE), F32),
            jax.ShapeDtypeStruct((n, LANES), BF16),
        ],
        compiler_params=_params(1),
        name="proj_c",
    )(xf, w, gq, gkv, cos, sin)


def _q_c_kernel(cq_ref, w_ref, cos_ref, sin_ref, q_ref):
    cq = cq_ref[...]
    cos = cos_ref[...]
    sin = sin_ref[...]
    pair_w = 2 * C_NOPE + 2 * C_ROPE
    for p in range(C_HEADS // 2):
        acc = _dot(cq, w_ref[:, p * pair_w:(p + 1) * pair_w])
        rot = _rope_tile(acc[:, 2 * C_NOPE:], cos, sin, C_ROPE)
        lane = lax.broadcasted_iota(jnp.int32, rot.shape, 1)
        o0 = 2 * p * C_QK
        q_ref[:, o0:o0 + C_NOPE] = acc[:, :C_NOPE].astype(q_ref.dtype)
        q_ref[:, o0 + C_NOPE:o0 + C_QK] = jnp.where(lane < C_ROPE, rot, 0.0).astype(q_ref.dtype)
        q_ref[:, o0 + C_QK:o0 + C_QK + C_NOPE] = acc[:, C_NOPE:2 * C_NOPE].astype(q_ref.dtype)
        q_ref[:, o0 + C_QK + C_NOPE:o0 + 2 * C_QK] = jnp.where(
            lane < C_ROPE, pltpu.roll(rot, C_ROPE, 1), 0.0).astype(q_ref.dtype)


def _q_c(cq, w, cos, sin):
    n = cq.shape[0]
    tm = _row_tile(n)
    n_tab = cos.shape[0] // tm
    return pl.pallas_call(
        _q_c_kernel,
        grid=(n // tm,),
        in_specs=[
            pl.BlockSpec((tm, C_Q_RANK), lambda i: (i, 0)),
            _resident(w.shape),
            pl.BlockSpec((tm, LANES), lambda i: (i % n_tab, 0)),
            pl.BlockSpec((tm, LANES), lambda i: (i % n_tab, 0)),
        ],
        out_specs=pl.BlockSpec((tm, C_HEADS * C_QK), lambda i: (i, 0)),
        out_shape=jax.ShapeDtypeStruct((n, C_HEADS * C_QK), BF16),
        compiler_params=_params(1),
        name="q_c",
    )(cq, w, cos, sin)


def _kv_c_kernel(ckv_ref, krp_ref, wkt_ref, wv_ref, kt_ref, v_ref):
    cb = ckv_ref[...].astype(BF16)
    eye = (lax.broadcasted_iota(jnp.int32, (LANES, LANES), 0)
           == lax.broadcasted_iota(jnp.int32, (LANES, LANES), 1)).astype(F32).astype(BF16)
    kr_t = _dot_t(eye, krp_ref[...]).astype(kt_ref.dtype)
    k_t = _dot_t(wkt_ref[...], cb).astype(kt_ref.dtype)
    for h in range(C_HEADS):
        kt_ref[0, 0, h * C_QK:h * C_QK + C_NOPE, :] = k_t[h * C_NOPE:(h + 1) * C_NOPE, :]
        kt_ref[0, 0, h * C_QK + C_NOPE:(h + 1) * C_QK, :] = kr_t
    for j in range(C_HEADS * C_V // COL_CHUNK):
        cols = slice(j * COL_CHUNK, (j + 1) * COL_CHUNK)
        v_ref[:, cols] = _dot(cb, wv_ref[:, cols]).astype(v_ref.dtype)


def _kv_c(ckv, krp, wkt, wv, batch, tk):
    n = ckv.shape[0]
    nkb = n // batch // tk
    assert n == batch * nkb * tk
    rows = lambda i: (i, 0)
    return pl.pallas_call(
        _kv_c_kernel,
        grid=(n // tk,),
        in_specs=[pl.BlockSpec((tk, C_KV_RANK), rows), pl.BlockSpec((tk, LANES), rows), _resident(wkt.shape),
                  _resident(wv.shape)],
        out_specs=[pl.BlockSpec((1, 1, C_HEADS * C_QK, tk), lambda i: (i // nkb, i % nkb, 0, 0)),
                   pl.BlockSpec((tk, C_HEADS * C_V), rows)],
        out_shape=[
            jax.ShapeDtypeStruct((batch, nkb, C_HEADS * C_QK, tk), BF16),
            jax.ShapeDtypeStruct((n, C_HEADS * C_V), BF16),
        ],
        compiler_params=_params(1),
        name="kv_c",
    )(ckv, krp, wkt, wv)


MLA_SCALE = (C_NOPE + C_ROPE) ** -0.5
MLA_LONG_BLOCK = 512
MLA_SHORT_BLOCK = 256


def _mla_attn_kernel(q_ref, kt_ref, v_ref, o_ref, *, pos0, heads):
    t = q_ref.shape[1]
    tq = tk = kt_ref.shape[3]
    row = lax.broadcasted_iota(jnp.int32, (tq, tk), 0) // CHUNK
    col = lax.broadcasted_iota(jnp.int32, (tq, tk), 1) // CHUNK
    diag_ok = col <= row
    ones = jnp.ones((tk, C_V), BF16)

    def scores(qs, kb):
        return tuple(_dot(qs[g], kt_ref[0, kb, g * C_QK:(g + 1) * C_QK, :]) for g in range(heads))

    def update(raw, kb, masked, carry):
        k0 = kb * tk
        out = []
        for g in range(heads):
            m, acc = carry[g]
            s = raw[g] * (MLA_SCALE * LOG2_E)
            if masked:
                s = jnp.where(diag_ok, s, NEG_INF)
            m_new = jnp.maximum(m, jnp.max(s, -1, keepdims=True))
            alpha = jnp.exp2(m - m_new)
            p = jnp.exp2(s - m_new).astype(BF16)
            v_ext = jnp.concatenate([v_ref[0, pl.ds(k0, tk), g * C_V:(g + 1) * C_V], ones], axis=1)
            out.append((m_new, alpha * acc + _dot(p, v_ext)))
        return tuple(out)

    for qi in range(t // tq):
        q0 = qi * tq
        qs = [q_ref[0, q0:q0 + tq, g * C_QK:(g + 1) * C_QK] for g in range(heads)]
        n_full = (pos0 + q0) // tk
        state = tuple((jnp.full((tq, 1), NEG_INF, F32), jnp.zeros((tq, 2 * C_V), F32)) for _ in range(heads))
        for kb in range(n_full):
            state = update(scores(qs, kb), kb, False, state)
        state = update(scores(qs, n_full), n_full, True, state)
        for g in range(heads):
            acc = state[g][1]
            o_ref[0, q0:q0 + tq, g * C_V:(g + 1) * C_V] = (acc[:, :C_V] * (1.0 / acc[:, C_V:])).astype(o_ref.dtype)


def _mla_attn(q, kt, v, *, pos0, heads=2):
    b, t, _ = q.shape
    _, nkb, _, tk = kt.shape
    s_len = nkb * tk
    assert t % tk == 0 and pos0 % tk == 0 and pos0 + t <= s_len and v.shape[1] == s_len
    return pl.pallas_call(
        functools.partial(_mla_attn_kernel, pos0=pos0, heads=heads),
        grid=(b, C_HEADS // heads),
        in_specs=[
            pl.BlockSpec((1, t, heads * C_QK), lambda bi, h: (bi, 0, h)),
            pl.BlockSpec((1, nkb, heads * C_QK, tk), lambda bi, h: (bi, 0, h, 0)),
            pl.BlockSpec((1, s_len, heads * C_V), lambda bi, h: (bi, 0, h)),
        ],
        out_specs=pl.BlockSpec((1, t, heads * C_V), lambda bi, h: (bi, 0, h)),
        out_shape=jax.ShapeDtypeStruct((b, t, C_HEADS * C_V), BF16),
        compiler_params=_params(2),
        name="mla_attn",
    )(q, kt, v)


def _mla_attn_short_kernel(q_ref, kt_ref, v_ref, o_ref, *, pos0, heads):
    t, nkb, tk = q_ref.shape[1], kt_ref.shape[1], kt_ref.shape[3]
    s_len = nkb * tk
    row = (lax.broadcasted_iota(jnp.int32, (t, s_len), 0) + pos0) // CHUNK
    col = lax.broadcasted_iota(jnp.int32, (t, s_len), 1) // CHUNK
    allowed = col <= row
    outs = []
    for g in range(heads):
        q = q_ref[0, :, g * C_QK:(g + 1) * C_QK]
        s = jnp.concatenate([_dot(q, kt_ref[0, kb, g * C_QK:(g + 1) * C_QK, :]) for kb in range(nkb)], axis=1)
        s = jnp.where(allowed, s * MLA_SCALE, NEG_INF)
        e = jnp.exp(s - jnp.max(s, -1, keepdims=True))
        p = (e * (1.0 / jnp.sum(e, -1, keepdims=True))).astype(BF16)
        outs.append(_dot(p, v_ref[0, :, g * C_V:(g + 1) * C_V]).astype(o_ref.dtype))
    o_ref[0] = jnp.concatenate(outs, axis=1)


def _mla_attn_short(q, kt, v, *, pos0, heads=4):
    b, t, _ = q.shape
    _, nkb, _, tk = kt.shape
    s_len = nkb * tk
    assert v.shape[1] == s_len and pos0 + t <= s_len
    return pl.pallas_call(
        functools.partial(_mla_attn_short_kernel, pos0=pos0, heads=heads),
        grid=(b, C_HEADS // heads),
        in_specs=[
            pl.BlockSpec((1, t, heads * C_QK), lambda bi, h: (bi, 0, h)),
            pl.BlockSpec((1, nkb, heads * C_QK, tk), lambda bi, h: (bi, 0, h, 0)),
            pl.BlockSpec((1, s_len, heads * C_V), lambda bi, h: (bi, 0, h)),
        ],
        out_specs=pl.BlockSpec((1, t, heads * C_V), lambda bi, h: (bi, 0, h)),
        out_shape=jax.ShapeDtypeStruct((b, t, C_HEADS * C_V), BF16),
        compiler_params=_params(2),
        name="mla_attn_short",
    )(q, kt, v)


def _prepare_weights(w):
    o1 = A_Q_W
    o2 = o1 + A_KV_W
    o3 = o2 + A_KV_W
    o4 = o3 + B_W
    o5 = o4 + B_W
    w_ab = w['w_in_ab'][0]
    w_ab = jnp.concatenate([w_ab[:, :o1], w_ab[:, o3:o4], w_ab[:, o4:o5], w_ab[:, o5:], w_ab[:, o1:o2],
                            w_ab[:, o2:o3]], axis=1)
    w_ab_long = jnp.concatenate([w_ab[:, :AB_KV_COL0], w_ab[:, AB_KV_COL0 + B_W:]], axis=1)
    w_kb_t = w_ab[:, AB_KV_COL0:AB_KV_COL0 + B_W].T
    w_c = jnp.pad(w['w_in_c'][0], ((0, 0), (0, LANES - C_ROPE)))
    hq = C_NOPE + C_ROPE
    q_cols = []
    for p in range(C_HEADS // 2):
        h0, h1 = 2 * p, 2 * p + 1
        q_cols += [jnp.arange(h0 * hq, h0 * hq + C_NOPE), jnp.arange(h1 * hq, h1 * hq + C_NOPE),
                   jnp.arange(h0 * hq + C_NOPE, (h0 + 1) * hq), jnp.arange(h1 * hq + C_NOPE, (h1 + 1) * hq)]
    w_q = w['w_q_b_c'][0][:, jnp.concatenate(q_cols)]
    hkv = C_NOPE + C_V
    k_cols = jnp.concatenate([jnp.arange(h * hkv, h * hkv + C_NOPE) for h in range(C_HEADS)])
    v_cols = jnp.concatenate([jnp.arange(h * hkv + C_NOPE, (h + 1) * hkv) for h in range(C_HEADS)])
    w_k_t = w['w_kv_b_c'][0][:, k_cols].T
    w_v = w['w_kv_b_c'][0][:, v_cols]
    row = lambda a: a.reshape(1, -1)
    return {
        'w_in_ab': w_ab.astype(BF16), 'w_in_ab_long': w_ab_long.astype(BF16), 'w_kb_t': w_kb_t.astype(BF16),
        'w_out_ab': w['w_out_ab'][0].astype(BF16),
        'w_in_c': w_c.astype(BF16), 'w_q_b_c': w_q.astype(BF16), 'w_k_t_c': w_k_t.astype(BF16),
        'w_v_c': w_v.astype(BF16),
        'w_out_c': w['w_out_c'][0].astype(BF16),
        'g_q_c': row(w['g_q_c'][0]), 'g_kv_c': row(w['g_kv_c'][0]),
        'sinks_a': w['sinks_a'][0], 'rel_bias_b': w['rel_bias_b'][0],
        'ln1_g': [row(w['ln1_g'][i]) for i in range(DEPTH)], 'ln1_b': [row(w['ln1_b'][i]) for i in range(DEPTH)],
        'ln2_g': [row(w['ln2_g'][i]) for i in range(DEPTH)], 'ln2_b': [row(w['ln2_b'][i]) for i in range(DEPTH)],
        'w_mlp_up': [w['w_mlp_up'][i].astype(BF16) for i in range(DEPTH)],
        'w_mlp_down': [w['w_mlp_down'][i].astype(BF16) for i in range(DEPTH)],
        'w_ple_gate': [w['w_ple_gate'][i].astype(BF16) for i in range(DEPTH)],
        'b_ple_gate': [row(w['b_ple_gate'][i]) for i in range(DEPTH)],
        'w_ple': [w['w_ple'][i].astype(BF16) for i in range(DEPTH)],
    }


def _channel_mix(xf, p_all, pw, i):
    xf = _mlp_ln(xf, pw['w_mlp_up'][i], pw['w_mlp_down'][i], pw['ln2_g'][i], pw['ln2_b'][i])
    return _ple(xf, p_all, i, pw['w_ple_gate'][i], pw['b_ple_gate'][i], pw['w_ple'][i])


def _trunk(x, p, pos0, past, pw):
    b, t, d = x.shape
    n = b * t
    tm = _row_tile(n)
    xf = x.reshape(n, d)

    if past is None:
        h, kb_new, vb_new, ka_new, va_new, kb_t = _proj_ab(xf, pw['w_in_ab_long'], t, pos0, pw['w_kb_t'],
                                                           BAND_T_BLOCK)
    else:
        h, kb_new, vb_new, ka_new, va_new = _proj_ab(xf, pw['w_in_ab'], t, pos0)
    h3 = h.reshape(b, t, h.shape[1])
    kb_new, vb_new, ka_new, va_new = [a.reshape(b, a.shape[0] // b, a.shape[1])
                                      for a in (kb_new, vb_new, ka_new, va_new)]
    q_a = (h3, A_Q_W, 0)
    q_b = (h3, B_W, 1)
    if past is None:
        ka_blk = (AB_KV_COL0 + B_W) // A_KV_W
        attn_a = _band_attn(q_a, (h3, A_KV_W, ka_blk), (h3, A_KV_W, ka_blk + 1), heads=A_HEADS,
                            kv_heads=A_KV_HEADS, band=WINDOW, tq=2 * CHUNK, past=0, sinks=pw['sinks_a'])
        bias_b = _rel_bias_folded(pw['rel_bias_b'], B_BAND_PAST, BAND_T_BLOCK)
        attn_b = _band_attn_t(q_b, kb_t, (h3, B_W, 2), bias_b, heads=B_HEADS, band=B_BAND_PAST)
        ak, av, bk, bv = ka_new[:, -WINDOW:], va_new[:, -WINDOW:], kb_new, vb_new
    else:
        n_past_a, n_past_b = past[0].shape[1], past[2].shape[1]
        full = [jnp.concatenate([c.reshape(b, c.shape[1], -1), new], axis=1)
                for c, new in zip(past[:4], (ka_new, va_new, kb_new, vb_new))]
        ak, av = full[0][:, -WINDOW:], full[1][:, -WINDOW:]
        bk, bv = full[2][:, -B_BAND_PAST:], full[3][:, -B_BAND_PAST:]
        k_a, v_a, k_b, v_b = [(f.astype(BF16), f.shape[2], 0) for f in full]
        attn_a = _band_attn(q_a, k_a, v_a, heads=A_HEADS, kv_heads=A_KV_HEADS, band=WINDOW, tq=CHUNK,
                            past=n_past_a, sinks=pw['sinks_a'])
        bias_b = _rel_bias(pw['rel_bias_b'], B_BAND_PAST, CHUNK)
        attn_b = _band_attn(q_b, k_b, v_b, heads=B_HEADS, kv_heads=B_HEADS, band=B_BAND_PAST, tq=CHUNK,
                            past=n_past_b, bias=bias_b)
    xf = _outproj_ln([attn_a.reshape(n, A_Q_W), attn_b.reshape(n, B_W)], pw['w_out_ab'], xf,
                     pw['ln1_g'][0], pw['ln1_b'][0])
    p_all = p.reshape(p.shape[0], n, p.shape[3])
    xf = _channel_mix(xf, p_all, pw, 0)

    cos_c, sin_c = _rope_tables(t, pos0, C_ROPE, max(t, tm))
    cq, ckv, kr, krp = _proj_c(xf, pw['w_in_c'], pw['g_q_c'], pw['g_kv_c'], cos_c, sin_c)
    q = _q_c(cq, pw['w_q_b_c'], cos_c, sin_c).reshape(b, t, C_HEADS * C_QK)
    if past is None:
        s_len = t
        tk = min(MLA_LONG_BLOCK, t)
        ckv_all, krp_all = ckv, krp
    else:
        past_ckv, past_kr = past[4], past[5]
        tk = MLA_SHORT_BLOCK
        s_len = -(-(past_ckv.shape[1] + t) // tk) * tk
        pad = s_len - past_ckv.shape[1] - t
        ckv_all = jnp.pad(jnp.concatenate([past_ckv, ckv.reshape(b, t, C_KV_RANK)], axis=1),
                          ((0, 0), (0, pad), (0, 0))).reshape(b * s_len, C_KV_RANK)
        past_krp = jnp.pad(past_kr, ((0, 0), (0, 0), (0, LANES - C_ROPE))).astype(BF16)
        krp_all = jnp.pad(jnp.concatenate([past_krp, krp.reshape(b, t, LANES)], axis=1),
                          ((0, 0), (0, pad), (0, 0))).reshape(b * s_len, LANES)
    kt_c, v_c = _kv_c(ckv_all, krp_all, pw['w_k_t_c'], pw['w_v_c'], b, tk)
    v_c = v_c.reshape(b, s_len, C_HEADS * C_V)
    if past is None:
        attn_c = _mla_attn(q, kt_c, v_c, pos0=pos0)
    else:
        attn_c = _mla_attn_short(q, kt_c, v_c, pos0=pos0)
    xf = _outproj_ln([attn_c.reshape(n, C_HEADS * C_V)], pw['w_out_c'], xf, pw['ln1_g'][1], pw['ln1_b'][1])
    xf = _channel_mix(xf, p_all, pw, 1)

    heads4 = lambda a, hh: a.reshape(1, b, a.shape[1], hh, HEAD_DIM)
    return (xf.reshape(b, t, d), heads4(ak, A_KV_HEADS), heads4(av, A_KV_HEADS), heads4(bk, B_HEADS),
            heads4(bv, B_HEADS), ckv.reshape(1, b, t, C_KV_RANK), kr.reshape(1, b, t, C_ROPE))


def kernel(x_prompt, x_sample, cache_a_k, cache_a_v, cache_b_k, cache_b_v, cache_c_kv, cache_c_krope, p_prompt,
           p_sample, w_in_ab, sinks_a, rel_bias_b, w_out_ab, w_in_c, g_q_c, w_q_b_c, g_kv_c, w_kv_b_c, w_out_c,
           ln1_g, ln1_b, ln2_g, ln2_b, w_mlp_up, w_mlp_down, w_ple_gate, b_ple_gate, w_ple):
    pw = _prepare_weights({
        'w_in_ab': w_in_ab, 'sinks_a': sinks_a, 'rel_bias_b': rel_bias_b, 'w_out_ab': w_out_ab,
        'w_in_c': w_in_c, 'g_q_c': g_q_c, 'w_q_b_c': w_q_b_c, 'g_kv_c': g_kv_c, 'w_kv_b_c': w_kv_b_c,
        'w_out_c': w_out_c, 'ln1_g': ln1_g, 'ln1_b': ln1_b, 'ln2_g': ln2_g, 'ln2_b': ln2_b,
        'w_mlp_up': w_mlp_up, 'w_mlp_down': w_mlp_down, 'w_ple_gate': w_ple_gate, 'b_ple_gate': b_ple_gate,
        'w_ple': w_ple,
    })
    prompt = _trunk(x_prompt, p_prompt, 0, None, pw)
    past = (cache_a_k[0], cache_a_v[0], cache_b_k[0], cache_b_v[0], cache_c_kv[0], cache_c_krope[0])
    sample = _trunk(x_sample, p_sample, cache_c_kv.shape[2], past, pw)
    return (prompt[0], sample[0]) + prompt[1:] + sample[1:]
```

```python
import functools

import jax
import jax.numpy as jnp
from jax import lax
from jax.experimental import pallas as pl
from jax.experimental.pallas import tpu as pltpu

F32 = jnp.float32
BF16 = jnp.bfloat16

CHUNK = 64
HEAD_DIM = 128
A_HEADS = 8
A_KV_HEADS = 2
WINDOW = 128
B_HEADS = 8
B_BAND_PAST = 512
REL_CLIP = 128
C_HEADS = 16
C_Q_RANK = 768
C_KV_RANK = 512
C_NOPE = 128
C_ROPE = 64
C_V = 128
DEPTH = 2
ROPE_THETA = 10000.0
LN_EPS = 1e-5
RMS_EPS = 1e-6
NEG_INF = -1e30
DEEPNORM_ALPHA = (2 * DEPTH) ** 0.25
LOG2_E = 1.4426950408889634
MLA_SCALE = (C_NOPE + C_ROPE) ** -0.5

A_Q_W = A_HEADS * HEAD_DIM
A_KV_W = A_KV_HEADS * HEAD_DIM
B_W = B_HEADS * HEAD_DIM
AB_IN_W = A_Q_W + 2 * A_KV_W + 3 * B_W
AB_KV_COL0 = A_Q_W + B_W
AB_KV_W = AB_IN_W - AB_KV_COL0
C_IN_W = C_Q_RANK + C_KV_RANK + C_ROPE
C_QK = 256

LANES = 128
V7X_VMEM_BYTES = 64 * 1024 * 1024
VMEM_LIMIT = V7X_VMEM_BYTES - 8 * 1024 * 1024

ROW_TILE = 512
COL_CHUNK = 512
MLP_ROW_TILE = 512
MLP_FF_TILE = 1024
BAND_T_BLOCK = 256


def _params(n_axes):
    return pltpu.CompilerParams(dimension_semantics=("arbitrary",) * n_axes, vmem_limit_bytes=VMEM_LIMIT)


def _resident(shape):
    nd = len(shape)
    return pl.BlockSpec(shape, lambda *_: (0,) * nd, pipeline_mode=pl.Buffered(1))


def _row_tile(n):
    return ROW_TILE if n % ROW_TILE == 0 else n


def _dot(a, b):
    return jnp.dot(a, b, preferred_element_type=F32)


def _dot_t(a, b):
    return lax.dot_general(a, b, (((1,), (1,)), ((), ())), preferred_element_type=F32)


def _layer_norm(y, g, b):
    mu = jnp.mean(y, -1, keepdims=True)
    var = jnp.mean(jnp.square(y - mu), -1, keepdims=True)
    return (y - mu) * lax.rsqrt(var + LN_EPS) * g + b


def _rms_norm(y, g):
    return y * lax.rsqrt(jnp.mean(jnp.square(y), -1, keepdims=True) + RMS_EPS) * g


def _rope_tile(t, cos, sin, d):
    if d == LANES:
        swapped = pltpu.roll(t, LANES // 2, 1)
    else:
        lane = lax.broadcasted_iota(jnp.int32, t.shape, 1)
        swapped = jnp.where((lane % d) < d // 2, pltpu.roll(t, LANES - d // 2, 1), pltpu.roll(t, d // 2, 1))
    return t * cos + swapped * sin


def _rope_tables(t, pos0, d, rows):
    half = d // 2
    inv = ROPE_THETA ** (-jnp.arange(half, dtype=F32) * (2.0 / d))
    ang = (jnp.arange(t, dtype=F32) + pos0)[:, None] * inv[None, :]
    cos = jnp.cos(ang)
    sin = jnp.sin(ang)
    reps = (rows // t, LANES // d)
    return jnp.tile(jnp.concatenate([cos, cos], 1), reps), jnp.tile(jnp.concatenate([-sin, sin], 1), reps)


def _proj_ab_kernel(*refs, kv_period, kb_transposed):
    refs = list(refs)
    x_ref, w_ref = refs[:2]
    wkbt_ref = refs.pop(2) if kb_transposed else None
    cos_ref, sin_ref, h_ref, kb_ref, vb_ref, ka_ref, va_ref = refs[2:9]
    kbt_ref = refs[9] if kb_transposed else None
    xb = x_ref[...].astype(BF16)
    cos = cos_ref[...]
    sin = sin_ref[...]
    keep_state = (pl.program_id(0) % kv_period) == kv_period - 1
    kv0 = AB_KV_COL0
    state_refs = []
    if not kb_transposed:
        state_refs.append((kv0, kb_ref))
        kv0 += B_W
    state_refs += [(kv0, vb_ref), (kv0 + B_W, ka_ref), (kv0 + B_W + A_KV_W, va_ref)]
    rope_tiles = set(range(A_HEADS)) | {(kv0 + B_W) // LANES + u for u in range(A_KV_HEADS)}
    for j in range(w_ref.shape[1] // COL_CHUNK):
        c0 = j * COL_CHUNK
        acc = _dot(xb, w_ref[:, c0:c0 + COL_CHUNK])
        parts = []
        for u in range(COL_CHUNK // LANES):
            part = acc[:, u * LANES:(u + 1) * LANES]
            if c0 // LANES + u in rope_tiles:
                part = _rope_tile(part, cos, sin, HEAD_DIM)
            parts.append(part)
        acc = jnp.concatenate(parts, axis=1)
        h_ref[:, c0:c0 + COL_CHUNK] = acc.astype(h_ref.dtype)
        for s0, ref in state_refs:
            lo, hi = max(c0, s0), min(c0 + COL_CHUNK, s0 + ref.shape[1])
            if lo < hi:
                @pl.when(keep_state)
                def _(acc=acc, ref=ref, lo=lo, hi=hi, s0=s0, c0=c0):
                    ref[:, lo - s0:hi - s0] = acc[:, lo - c0:hi - c0]
    if kb_transposed:
        kb_t = _dot_t(wkbt_ref[...], xb)
        blk = kbt_ref.shape[3]
        for c in range(kbt_ref.shape[1]):
            kbt_ref[0, c] = kb_t[:, c * blk:(c + 1) * blk].astype(kbt_ref.dtype)

        @pl.when(keep_state)
        def _():
            kb_ref[...] = kb_t.T


def _proj_ab(xf, w, seq, pos0, w_kb_t=None, kt_block=None):
    n, d = xf.shape
    tm = _row_tile(n)
    assert seq % tm == 0 or tm % seq == 0
    kv_period = max(seq // tm, 1)
    assert min(seq, B_BAND_PAST) == min(seq, tm)
    cos, sin = _rope_tables(seq, pos0, HEAD_DIM, max(seq, tm))
    n_tab = cos.shape[0] // tm
    state_widths = (B_W, B_W, A_KV_W, A_KV_W)
    kb_transposed = w_kb_t is not None
    rows = lambda i: (i, 0)
    tab = lambda i: (i % n_tab, 0)
    in_specs = [pl.BlockSpec((tm, d), rows), _resident(w.shape)]
    args = [xf, w]
    out_specs = [pl.BlockSpec((tm, w.shape[1]), rows)] + [
        pl.BlockSpec((tm, sw), lambda i: (i // kv_period, 0)) for sw in state_widths]
    out_shape = [jax.ShapeDtypeStruct((n, w.shape[1]), BF16)] + [
        jax.ShapeDtypeStruct((n // kv_period, sw), F32) for sw in state_widths]
    if kb_transposed:
        assert seq % tm == 0 and tm % kt_block == 0
        tiles = seq // tm
        in_specs.append(_resident(w_kb_t.shape))
        args.append(w_kb_t)
        out_specs.append(pl.BlockSpec((1, tm // kt_block, B_W, kt_block), lambda i: (i // tiles, i % tiles, 0, 0)))
        out_shape.append(jax.ShapeDtypeStruct((n // seq, seq // kt_block, B_W, kt_block), BF16))
    in_specs += [pl.BlockSpec((tm, LANES), tab), pl.BlockSpec((tm, LANES), tab)]
    args += [cos, sin]
    return pl.pallas_call(
        functools.partial(_proj_ab_kernel, kv_period=kv_period, kb_transposed=kb_transposed),
        grid=(n // tm,),
        in_specs=in_specs,
        out_specs=out_specs,
        out_shape=out_shape,
        compiler_params=_params(1),
        name="proj_ab",
    )(*args)


def _rel_bias_kernel(tab_ref, o_ref, *, band):
    h = pl.program_id(0)
    shape = o_ref.shape[1:]
    r = lax.broadcasted_iota(jnp.int32, shape, 0)
    w = lax.broadcasted_iota(jnp.int32, shape, 1)
    idx = jnp.clip(band + r - w, -REL_CLIP, REL_CLIP) + REL_CLIP

    def body(d, acc):
        return jnp.where(idx == d, tab_ref[h, d], acc)

    o_ref[0] = lax.fori_loop(0, 2 * REL_CLIP + 1, body, jnp.zeros(shape, F32))


def _rel_bias(table, band, tq):
    heads = table.shape[0]
    return pl.pallas_call(
        functools.partial(_rel_bias_kernel, band=band),
        grid=(heads,),
        in_specs=[pl.BlockSpec(memory_space=pltpu.SMEM)],
        out_specs=pl.BlockSpec((1, tq, band + tq), lambda h: (h, 0, 0)),
        out_shape=jax.ShapeDtypeStruct((heads, tq, band + tq), F32),
        compiler_params=_params(1),
        name="rel_bias",
    )(table)


def _band_attn_kernel(*refs, heads, kv_heads, band, tq, past, has_bias, has_sinks):
    q_ref, k_ref, v_ref = refs[:3]
    rest = list(refs[3:])
    bias_ref = rest.pop(0) if has_bias else None
    sink_ref = rest.pop(0) if has_sinks else None
    o_ref = rest.pop(0)

    width = band + tq
    scale = HEAD_DIM ** -0.5
    ws = past + pl.program_id(1) * tq - band
    r = lax.broadcasted_iota(jnp.int32, (tq, width), 0) // CHUNK
    w = lax.broadcasted_iota(jnp.int32, (tq, width), 1)
    wc = w // CHUNK
    allowed = (wc >= r) & (wc <= r + band // CHUNK) & (w + ws >= 0)

    def window(ref, cols):
        if past >= band:
            return ref[0, pl.ds(pl.multiple_of(ws, CHUNK), width), cols]
        pieces = [ref[0, pl.ds(pl.multiple_of(jnp.maximum(ws + c * LANES, 0), LANES), LANES), cols]
                  for c in range(width // LANES)]
        return jnp.concatenate(pieces, axis=0)

    group = heads // kv_heads
    outs = []
    for kh in range(kv_heads):
        cols = slice(kh * HEAD_DIM, (kh + 1) * HEAD_DIM)
        k_w = window(k_ref, cols)
        v_w = window(v_ref, cols)
        for g in range(group):
            h = kh * group + g
            hcols = slice(h * HEAD_DIM, (h + 1) * HEAD_DIM)
            s = _dot_t(q_ref[0, :, hcols], k_w) * scale
            if has_bias:
                s = s + bias_ref[h]
            s = jnp.where(allowed, s, NEG_INF)
            m = jnp.max(s, -1, keepdims=True)
            if has_sinks:
                sink = sink_ref[h]
                m = jnp.maximum(m, sink)
            e = jnp.exp(s - m)
            den = jnp.sum(e, -1, keepdims=True)
            if has_sinks:
                den = den + jnp.exp(sink - m)
            p = (e * (1.0 / den)).astype(BF16)
            outs.append(_dot(p, v_w).astype(o_ref.dtype))
    o_ref[0] = jnp.concatenate(outs, axis=1)


def _band_attn(q, k, v, *, heads, kv_heads, band, tq, past, bias=None, sinks=None):
    (qa, qw, qi), (ka, kw, ki), (va, vw, vi) = q, k, v
    b, t, _ = qa.shape
    s_len = ka.shape[1]
    assert t % tq == 0 and s_len == past + t and qw == heads * HEAD_DIM and kw == kv_heads * HEAD_DIM
    assert past >= band or (past == 0 and tq % LANES == 0 and band % LANES == 0)
    in_specs = [
        pl.BlockSpec((1, tq, qw), lambda bi, i: (bi, i, qi)),
        pl.BlockSpec((1, s_len, kw), lambda bi, i: (bi, 0, ki)),
        pl.BlockSpec((1, s_len, vw), lambda bi, i: (bi, 0, vi)),
    ]
    args = [qa, ka, va]
    if bias is not None:
        in_specs.append(_resident(bias.shape))
        args.append(bias)
    if sinks is not None:
        in_specs.append(pl.BlockSpec(memory_space=pltpu.SMEM))
        args.append(sinks)
    return pl.pallas_call(
        functools.partial(_band_attn_kernel, heads=heads, kv_heads=kv_heads, band=band, tq=tq, past=past,
                          has_bias=bias is not None, has_sinks=sinks is not None),
        grid=(b, t // tq),
        in_specs=in_specs,
        out_specs=pl.BlockSpec((1, tq, qw), lambda bi, i: (bi, i, 0)),
        out_shape=jax.ShapeDtypeStruct((b, t, qw), BF16),
        compiler_params=_params(2),
        name="band_attn",
    )(*args)


def _rel_bias_folded_kernel(tab_ref, o_ref, *, band):
    h = pl.program_id(0)
    tq, width = o_ref.shape[1:]
    period = tq + width
    j = lax.broadcasted_iota(jnp.int32, (8, period), 1)
    dist = jnp.where(j < width, band - j, band - (j - period))
    idx = jnp.clip(dist, -REL_CLIP, REL_CLIP) + REL_CLIP

    def body(d, acc):
        return jnp.where(idx == d, tab_ref[h, d], acc)

    g = lax.fori_loop(0, 2 * REL_CLIP + 1, body, jnp.zeros((8, period), F32))
    full = jnp.concatenate([g] * (tq // 8), axis=0)
    bias = pltpu.roll(full, 0, 1, stride=1, stride_axis=0)[:, :width]
    rc = lax.broadcasted_iota(jnp.int32, (tq, width), 0) // CHUNK
    wc = lax.broadcasted_iota(jnp.int32, (tq, width), 1) // CHUNK
    o_ref[0] = jnp.where((wc >= rc) & (wc <= rc + band // CHUNK), bias * LOG2_E, NEG_INF)


def _rel_bias_folded(table, band, tq):
    heads = table.shape[0]
    assert (band + 2 * tq) % LANES == 0 and tq % 8 == 0
    return pl.pallas_call(
        functools.partial(_rel_bias_folded_kernel, band=band),
        grid=(heads,),
        in_specs=[pl.BlockSpec(memory_space=pltpu.SMEM)],
        out_specs=pl.BlockSpec((1, tq, band + tq), lambda h: (h, 0, 0)),
        out_shape=jax.ShapeDtypeStruct((heads, tq, band + tq), F32),
        compiler_params=_params(1),
        name="rel_bias_folded",
    )(table)


def _band_attn_t_kernel(q_ref, kt_ref, v_ref, bias_ref, o_ref, *, heads, band):
    tq = kt_ref.shape[3]
    n_past = band // tq
    width = band + tq
    i = pl.program_id(1)
    ws = i * tq - band
    col_valid = (lax.broadcasted_iota(jnp.int32, (1, width), 1) + ws) >= 0
    ones = jnp.ones((width, HEAD_DIM), BF16)
    outs = []
    for h in range(heads):
        hc = slice(h * HEAD_DIM, (h + 1) * HEAD_DIM)
        q_h = q_ref[0, :, hc]
        s = jnp.concatenate([_dot(q_h, kt_ref[0, jnp.maximum(i - n_past + c, 0), hc, :])
                             for c in range(n_past + 1)], axis=1)
        s = s * (HEAD_DIM ** -0.5 * LOG2_E) + bias_ref[h]
        s = jnp.where(col_valid, s, NEG_INF)
        p = jnp.exp2(s - jnp.max(s, -1, keepdims=True)).astype(BF16)
        v_w = jnp.concatenate([v_ref[0, pl.ds(pl.multiple_of(jnp.maximum(ws + c * tq, 0), tq), tq), hc]
                               for c in range(n_past + 1)], axis=0)
        o_ext = _dot(p, jnp.concatenate([v_w, ones], axis=1))
        outs.append((o_ext[:, :HEAD_DIM] * (1.0 / o_ext[:, HEAD_DIM:])).astype(o_ref.dtype))
    o_ref[0] = jnp.concatenate(outs, axis=1)


def _band_attn_t(q, kt, v, bias, *, heads, band):
    (qa, qw, qi), (va, vw, vi) = q, v
    b, t, _ = qa.shape
    _, nkb, _, tq = kt.shape
    assert nkb * tq == t and band % tq == 0 and qw == vw == heads * HEAD_DIM
    return pl.pallas_call(
        functools.partial(_band_attn_t_kernel, heads=heads, band=band),
        grid=(b, t // tq),
        in_specs=[
            pl.BlockSpec((1, tq, qw), lambda bi, i: (bi, i, qi)),
            pl.BlockSpec((1, nkb, heads * HEAD_DIM, tq), lambda bi, i: (bi, 0, 0, 0)),
            pl.BlockSpec((1, t, vw), lambda bi, i: (bi, 0, vi)),
            _resident(bias.shape),
        ],
        out_specs=pl.BlockSpec((1, tq, qw), lambda bi, i: (bi, i, 0)),
        out_shape=jax.ShapeDtypeStruct((b, t, qw), BF16),
        compiler_params=_params(2),
        name="band_attn_t",
    )(qa, kt, va, bias)


def _outproj_ln_kernel(*refs, n_in):
    a_refs = refs[:n_in]
    w_ref, x_ref, g_ref, b_ref, o_ref = refs[n_in:]
    d_out = o_ref.shape[1]
    for j in range(d_out // COL_CHUNK):
        cols = slice(j * COL_CHUNK, (j + 1) * COL_CHUNK)
        y = DEEPNORM_ALPHA * x_ref[:, cols]
        r0 = 0
        for a_ref in a_refs:
            kk = a_ref.shape[1]
            y = y + _dot(a_ref[...], w_ref[r0:r0 + kk, cols])
            r0 += kk
        o_ref[:, cols] = y
    o_ref[...] = _layer_norm(o_ref[...], g_ref[...], b_ref[...])


def _outproj_ln(a_list, w, xf, g, b):
    n, d = xf.shape
    tm = _row_tile(n)
    assert sum(a.shape[1] for a in a_list) == w.shape[0]
    return pl.pallas_call(
        functools.partial(_outproj_ln_kernel, n_in=len(a_list)),
        grid=(n // tm,),
        in_specs=[pl.BlockSpec((tm, a.shape[1]), lambda i: (i, 0)) for a in a_list] + [
            _resident(w.shape),
            pl.BlockSpec((tm, d), lambda i: (i, 0)),
            _resident(g.shape),
            _resident(b.shape),
        ],
        out_specs=pl.BlockSpec((tm, d), lambda i: (i, 0)),
        out_shape=jax.ShapeDtypeStruct((n, d), F32),
        compiler_params=_params(1),
        name="outproj_ln",
    )(*a_list, w, xf, g, b)


def _mlp_ln_kernel(x_ref, wu_ref, wd_ref, g_ref, b_ref, o_ref, xb_ref):
    f = pl.program_id(1)

    @pl.when(f == 0)
    def _():
        xb_ref[...] = x_ref[...].astype(BF16)
        o_ref[...] = jnp.zeros(o_ref.shape, o_ref.dtype)

    hid = _dot(xb_ref[...], wu_ref[...])
    hid = jnp.square(jnp.maximum(hid, 0.0)).astype(BF16)
    for j in range(o_ref.shape[1] // COL_CHUNK):
        cols = slice(j * COL_CHUNK, (j + 1) * COL_CHUNK)
        o_ref[:, cols] += _dot(hid, wd_ref[:, cols])

    @pl.when(f == pl.num_programs(1) - 1)
    def _():
        o_ref[...] = _layer_norm(DEEPNORM_ALPHA * x_ref[...] + o_ref[...], g_ref[...], b_ref[...])


def _mlp_ln(xf, w_up, w_down, g, b):
    n, d = xf.shape
    d_ff = w_up.shape[1]
    tm = MLP_ROW_TILE if n % MLP_ROW_TILE == 0 else n
    tf = MLP_FF_TILE
    return pl.pallas_call(
        _mlp_ln_kernel,
        grid=(n // tm, d_ff // tf),
        in_specs=[
            pl.BlockSpec((tm, d), lambda i, f: (i, 0)),
            pl.BlockSpec((d, tf), lambda i, f: (0, f)),
            pl.BlockSpec((tf, d), lambda i, f: (f, 0)),
            _resident(g.shape),
            _resident(b.shape),
        ],
        out_specs=pl.BlockSpec((tm, d), lambda i, f: (i, 0)),
        out_shape=jax.ShapeDtypeStruct((n, d), F32),
        scratch_shapes=[pltpu.VMEM((tm, d), BF16)],
        compiler_params=_params(2),
        name="mlp_ln",
    )(xf, w_up, w_down, g, b)


def _ple_kernel(x_ref, p_ref, wg_ref, bg_ref, wp_ref, o_ref):
    xb = x_ref[...].astype(BF16)
    pb = p_ref[...].astype(BF16)
    for j in range(o_ref.shape[1] // COL_CHUNK):
        cols = slice(j * COL_CHUNK, (j + 1) * COL_CHUNK)
        gate = jax.nn.sigmoid(_dot(xb, wg_ref[:, cols]) + bg_ref[:, cols])
        o_ref[:, cols] = x_ref[:, cols] + gate * _dot(pb, wp_ref[:, cols])


def _ple(xf, p_all, layer, wg, bg, wp):
    n, d = xf.shape
    tm = _row_tile(n)
    return pl.pallas_call(
        _ple_kernel,
        grid=(n // tm,),
        in_specs=[
            pl.BlockSpec((tm, d), lambda i: (i, 0)),
            pl.BlockSpec((None, tm, p_all.shape[2]), lambda i: (layer, i, 0)),
            _resident(wg.shape),
            _resident(bg.shape),
            _resident(wp.shape),
        ],
        out_specs=pl.BlockSpec((tm, d), lambda i: (i, 0)),
        out_shape=jax.ShapeDtypeStruct((n, d), F32),
        compiler_params=_params(1),
        name="ple",
    )(xf, p_all, wg, bg, wp)


def _proj_c_kernel(x_ref, w_ref, gq_ref, gkv_ref, cos_ref, sin_ref, cq_ref, ckv_ref, kr_ref, krp_ref):
    h = _dot(x_ref[...].astype(BF16), w_ref[...])
    cq_ref[...] = _rms_norm(h[:, :C_Q_RANK], gq_ref[...]).astype(cq_ref.dtype)
    ckv_ref[...] = _rms_norm(h[:, C_Q_RANK:C_Q_RANK + C_KV_RANK], gkv_ref[...])
    t = h[:, C_Q_RANK + C_KV_RANK:]
    rot = _rope_tile(t, cos_ref[...], sin_ref[...], C_ROPE)
    lane = lax.broadcasted_iota(jnp.int32, rot.shape, 1)
    rot = jnp.where(lane < C_ROPE, rot, 0.0)
    kr_ref[...] = rot[:, :C_ROPE]
    krp_ref[...] = rot.astype(krp_ref.dtype)


def _proj_c(xf, w, gq, gkv, cos, sin):
    n, d = xf.shape
    tm = _row_tile(n)
    n_tab = cos.shape[0] // tm
    rows = lambda i: (i, 0)
    return pl.pallas_call(
        _proj_c_kernel,
        grid=(n // tm,),
        in_specs=[
            pl.BlockSpec((tm, d), rows),
            _resident(w.shape),
            _resident(gq.shape),
            _resident(gkv.shape),
            pl.BlockSpec((tm, LANES), lambda i: (i % n_tab, 0)),
            pl.BlockSpec((tm, LANES), lambda i: (i % n_tab, 0)),
        ],
        out_specs=[
            pl.BlockSpec((tm, C_Q_RANK), rows),
            pl.BlockSpec((tm, C_KV_RANK), rows),
            pl.BlockSpec((tm, C_ROPE), rows),
            pl.BlockSpec((tm, LANES), rows),
        ],
        out_shape=[
            jax.ShapeDtypeStruct((n, C_Q_RANK), BF16),
            jax.ShapeDtypeStruct((n, C_KV_RANK), F32),
            jax.ShapeDtypeStruct((n, C_ROPE), F32),
            jax.ShapeDtypeStruct((n, LANES), BF16),
        ],
        compiler_params=_params(1),
        name="proj_c",
    )(xf, w, gq, gkv, cos, sin)


Q_GROUP = 4


def _q_c_kernel(cq_ref, w_ref, cos_ref, sin_ref, q_ref):
    cq = cq_ref[...]
    cos = cos_ref[...]
    sin = sin_ref[...]
    grp_w = Q_GROUP * (C_NOPE + C_ROPE)
    for gi in range(C_HEADS // Q_GROUP):
        acc = _dot(cq, w_ref[:, gi * grp_w:(gi + 1) * grp_w]) * (MLA_SCALE * LOG2_E)
        for u in range(Q_GROUP):
            o0 = (gi * Q_GROUP + u) * C_QK
            q_ref[:, o0:o0 + C_NOPE] = acc[:, u * C_NOPE:(u + 1) * C_NOPE].astype(q_ref.dtype)
            if u % 2 == 0:
                r0 = Q_GROUP * C_NOPE + (u // 2) * LANES
                rot = _rope_tile(acc[:, r0:r0 + LANES], cos, sin, C_ROPE)
                lane = lax.broadcasted_iota(jnp.int32, rot.shape, 1)
                piece = rot
            else:
                piece = pltpu.roll(rot, C_ROPE, 1)
            q_ref[:, o0 + C_NOPE:o0 + C_QK] = jnp.where(lane < C_ROPE, piece, 0.0).astype(q_ref.dtype)


def _q_c(cq, w, cos, sin):
    n = cq.shape[0]
    tm = _row_tile(n)
    n_tab = cos.shape[0] // tm
    return pl.pallas_call(
        _q_c_kernel,
        grid=(n // tm,),
        in_specs=[
            pl.BlockSpec((tm, C_Q_RANK), lambda i: (i, 0)),
            _resident(w.shape),
            pl.BlockSpec((tm, LANES), lambda i: (i % n_tab, 0)),
            pl.BlockSpec((tm, LANES), lambda i: (i % n_tab, 0)),
        ],
        out_specs=pl.BlockSpec((tm, C_HEADS * C_QK), lambda i: (i, 0)),
        out_shape=jax.ShapeDtypeStruct((n, C_HEADS * C_QK), BF16),
        compiler_params=_params(1),
        name="q_c",
    )(cq, w, cos, sin)


def _kv_c_kernel(ckv_ref, krp_ref, wkt_ref, wv_ref, kt_ref, krt_ref, v_ref):
    cb = ckv_ref[...].astype(BF16)
    eye = (lax.broadcasted_iota(jnp.int32, (LANES, LANES), 0)
           == lax.broadcasted_iota(jnp.int32, (LANES, LANES), 1)).astype(F32).astype(BF16)
    krt_ref[0, 0] = _dot_t(eye, krp_ref[...]).astype(krt_ref.dtype)
    kt_ref[0, 0] = _dot_t(wkt_ref[...], cb).astype(kt_ref.dtype)
    for j in range(C_HEADS * C_V // COL_CHUNK):
        cols = slice(j * COL_CHUNK, (j + 1) * COL_CHUNK)
        v_ref[:, cols] = _dot(cb, wv_ref[:, cols]).astype(v_ref.dtype)


def _kv_c(ckv, krp, wkt, wv, batch, tk):
    n = ckv.shape[0]
    nkb = n // batch // tk
    assert n == batch * nkb * tk
    rows = lambda i: (i, 0)
    blk = lambda i: (i // nkb, i % nkb, 0, 0)
    return pl.pallas_call(
        _kv_c_kernel,
        grid=(n // tk,),
        in_specs=[pl.BlockSpec((tk, C_KV_RANK), rows), pl.BlockSpec((tk, LANES), rows), _resident(wkt.shape),
                  _resident(wv.shape)],
        out_specs=[pl.BlockSpec((1, 1, C_HEADS * C_NOPE, tk), blk), pl.BlockSpec((1, 1, LANES, tk), blk),
                   pl.BlockSpec((tk, C_HEADS * C_V), rows)],
        out_shape=[
            jax.ShapeDtypeStruct((batch, nkb, C_HEADS * C_NOPE, tk), BF16),
            jax.ShapeDtypeStruct((batch, nkb, LANES, tk), BF16),
            jax.ShapeDtypeStruct((n, C_HEADS * C_V), BF16),
        ],
        compiler_params=_params(1),
        name="kv_c",
    )(ckv, krp, wkt, wv)


MLA_LONG_BLOCK = 512


def _mla_attn_kernel(q_ref, kt_ref, krt_ref, v_ref, o_ref, *, pos0, heads):
    t = q_ref.shape[1]
    tq = tk = kt_ref.shape[3]
    row = lax.broadcasted_iota(jnp.int32, (tq, tk), 0) // CHUNK
    col = lax.broadcasted_iota(jnp.int32, (tq, tk), 1) // CHUNK
    diag_ok = col <= row
    ones = jnp.ones((tk, C_V), BF16)

    def scores(qs, kb):
        return tuple(_dot(qs[g], jnp.concatenate([kt_ref[0, kb, g * C_NOPE:(g + 1) * C_NOPE, :], krt_ref[0, kb]],
                                                 axis=0)) for g in range(heads))

    def update(raw, kb, masked, carry):
        k0 = kb * tk
        out = []
        for g in range(heads):
            m, acc = carry[g]
            s = raw[g]
            if masked:
                s = jnp.where(diag_ok, s, NEG_INF)
            m_new = jnp.maximum(m, jnp.max(s, -1, keepdims=True))
            alpha = jnp.exp2(m - m_new)
            p = jnp.exp2(s - m_new).astype(BF16)
            v_ext = jnp.concatenate([v_ref[0, pl.ds(k0, tk), g * C_V:(g + 1) * C_V], ones], axis=1)
            out.append((m_new, alpha * acc + _dot(p, v_ext)))
        return tuple(out)

    for qi in range(t // tq):
        q0 = qi * tq
        qs = [q_ref[0, q0:q0 + tq, g * C_QK:(g + 1) * C_QK] for g in range(heads)]
        n_full = (pos0 + q0) // tk
        state = tuple((jnp.full((tq, 1), NEG_INF, F32), jnp.zeros((tq, 2 * C_V), F32)) for _ in range(heads))
        for kb in range(n_full):
            state = update(scores(qs, kb), kb, False, state)
        state = update(scores(qs, n_full), n_full, True, state)
        for g in range(heads):
            acc = state[g][1]
            o_ref[0, q0:q0 + tq, g * C_V:(g + 1) * C_V] = (acc[:, :C_V] * (1.0 / acc[:, C_V:])).astype(o_ref.dtype)


def _mla_attn(q, kt, krt, v, *, pos0, heads=2):
    b, t, _ = q.shape
    _, nkb, _, tk = kt.shape
    s_len = nkb * tk
    assert t % tk == 0 and pos0 % tk == 0 and pos0 + t <= s_len and v.shape[1] == s_len
    return pl.pallas_call(
        functools.partial(_mla_attn_kernel, pos0=pos0, heads=heads),
        grid=(b, C_HEADS // heads),
        in_specs=[
            pl.BlockSpec((1, t, heads * C_QK), lambda bi, h: (bi, 0, h)),
            pl.BlockSpec((1, nkb, heads * C_NOPE, tk), lambda bi, h: (bi, 0, h, 0)),
            pl.BlockSpec((1, nkb, LANES, tk), lambda bi, h: (bi, 0, 0, 0)),
            pl.BlockSpec((1, s_len, heads * C_V), lambda bi, h: (bi, 0, h)),
        ],
        out_specs=pl.BlockSpec((1, t, heads * C_V), lambda bi, h: (bi, 0, h)),
        out_shape=jax.ShapeDtypeStruct((b, t, C_HEADS * C_V), BF16),
        compiler_params=_params(2),
        name="mla_attn",
    )(q, kt, krt, v)


def _mla_absorbed_kernel(q_ref, ckv_new_ref, kr_new_ref, ckv_past_ref, kr_past_ref, wkt_ref, wv_ref, o_ref):
    t = q_ref.shape[1]
    q_lat, q_rope = [], []
    for h in range(C_HEADS):
        q_lat.append(_dot(q_ref[0, :, h * C_QK:h * C_QK + C_NOPE], wkt_ref[h * C_NOPE:(h + 1) * C_NOPE, :]))
        q_rope.append(q_ref[0, :, h * C_QK + C_NOPE:h * C_QK + C_NOPE + C_ROPE])
    q_lat = jnp.concatenate(q_lat, axis=0).astype(BF16)
    q_rope = jnp.concatenate(q_rope, axis=0)
    lat = [ckv_past_ref[...].astype(BF16), ckv_new_ref[0].astype(BF16)]
    rot = [kr_past_ref[...].astype(BF16), kr_new_ref[0].astype(BF16)]
    s = jnp.concatenate([_dot_t(q_lat, c) + _dot_t(q_rope, r) for c, r in zip(lat, rot)], axis=1)
    e = jnp.exp2(s - jnp.max(s, -1, keepdims=True))
    p = (e * (1.0 / jnp.sum(e, -1, keepdims=True))).astype(BF16)
    n_past = lat[0].shape[0]
    o_lat = (_dot(p[:, :n_past], lat[0]) + _dot(p[:, n_past:], lat[1])).astype(BF16)
    o_ref[0] = jnp.concatenate([_dot(o_lat[h * t:(h + 1) * t], wv_ref[:, h * C_V:(h + 1) * C_V])
                                for h in range(C_HEADS)], axis=1).astype(o_ref.dtype)


def _mla_absorbed(q, ckv_new, kr_new, ckv_past, kr_past, wkt, wv, *, pos0):
    b, t, _ = q.shape
    n_past = ckv_past.shape[2]
    assert t == CHUNK and pos0 % CHUNK == 0 and n_past <= pos0
    return pl.pallas_call(
        _mla_absorbed_kernel,
        grid=(b,),
        in_specs=[
            pl.BlockSpec((1, t, C_HEADS * C_QK), lambda bi: (bi, 0, 0)),
            pl.BlockSpec((1, t, C_KV_RANK), lambda bi: (bi, 0, 0)),
            pl.BlockSpec((1, t, C_ROPE), lambda bi: (bi, 0, 0)),
            pl.BlockSpec((None, None, n_past, C_KV_RANK), lambda bi: (0, bi, 0, 0)),
            pl.BlockSpec((None, None, n_past, C_ROPE), lambda bi: (0, bi, 0, 0)),
            _resident(wkt.shape),
            _resident(wv.shape),
        ],
        out_specs=pl.BlockSpec((1, t, C_HEADS * C_V), lambda bi: (bi, 0, 0)),
        out_shape=jax.ShapeDtypeStruct((b, t, C_HEADS * C_V), BF16),
        compiler_params=_params(1),
        name="mla_absorbed",
    )(q, ckv_new, kr_new, ckv_past, kr_past, wkt, wv)


def _prepare_weights(w):
    o1 = A_Q_W
    o2 = o1 + A_KV_W
    o3 = o2 + A_KV_W
    o4 = o3 + B_W
    o5 = o4 + B_W
    w_ab = w['w_in_ab'][0]
    w_ab = jnp.concatenate([w_ab[:, :o1], w_ab[:, o3:o4], w_ab[:, o4:o5], w_ab[:, o5:], w_ab[:, o1:o2],
                            w_ab[:, o2:o3]], axis=1)
    w_ab_long = jnp.concatenate([w_ab[:, :AB_KV_COL0], w_ab[:, AB_KV_COL0 + B_W:]], axis=1)
    w_kb_t = w_ab[:, AB_KV_COL0:AB_KV_COL0 + B_W].T
    w_c = jnp.pad(w['w_in_c'][0], ((0, 0), (0, LANES - C_ROPE)))
    hq = C_NOPE + C_ROPE
    q_cols = []
    for g0 in range(0, C_HEADS, Q_GROUP):
        q_cols += [jnp.arange(h * hq, h * hq + C_NOPE) for h in range(g0, g0 + Q_GROUP)]
        q_cols += [jnp.arange(h * hq + C_NOPE, (h + 1) * hq) for h in range(g0, g0 + Q_GROUP)]
    w_q = w['w_q_b_c'][0][:, jnp.concatenate(q_cols)]
    hkv = C_NOPE + C_V
    k_cols = jnp.concatenate([jnp.arange(h * hkv, h * hkv + C_NOPE) for h in range(C_HEADS)])
    v_cols = jnp.concatenate([jnp.arange(h * hkv + C_NOPE, (h + 1) * hkv) for h in range(C_HEADS)])
    w_k_t = w['w_kv_b_c'][0][:, k_cols].T
    w_v = w['w_kv_b_c'][0][:, v_cols]
    row = lambda a: a.reshape(1, -1)
    return {
        'w_in_ab': w_ab.astype(BF16), 'w_in_ab_long': w_ab_long.astype(BF16), 'w_kb_t': w_kb_t.astype(BF16),
        'w_out_ab': w['w_out_ab'][0].astype(BF16),
        'w_in_c': w_c.astype(BF16), 'w_q_b_c': w_q.astype(BF16), 'w_k_t_c': w_k_t.astype(BF16),
        'w_v_c': w_v.astype(BF16),
        'w_out_c': w['w_out_c'][0].astype(BF16),
        'g_q_c': row(w['g_q_c'][0]), 'g_kv_c': row(w['g_kv_c'][0]),
        'sinks_a': w['sinks_a'][0], 'rel_bias_b': w['rel_bias_b'][0],
        'ln1_g': [row(w['ln1_g'][i]) for i in range(DEPTH)], 'ln1_b': [row(w['ln1_b'][i]) for i in range(DEPTH)],
        'ln2_g': [row(w['ln2_g'][i]) for i in range(DEPTH)], 'ln2_b': [row(w['ln2_b'][i]) for i in range(DEPTH)],
        'w_mlp_up': [w['w_mlp_up'][i].astype(BF16) for i in range(DEPTH)],
        'w_mlp_down': [w['w_mlp_down'][i].astype(BF16) for i in range(DEPTH)],
        'w_ple_gate': [w['w_ple_gate'][i].astype(BF16) for i in range(DEPTH)],
        'b_ple_gate': [row(w['b_ple_gate'][i]) for i in range(DEPTH)],
        'w_ple': [w['w_ple'][i].astype(BF16) for i in range(DEPTH)],
    }


def _channel_mix(xf, p_all, pw, i):
    xf = _mlp_ln(xf, pw['w_mlp_up'][i], pw['w_mlp_down'][i], pw['ln2_g'][i], pw['ln2_b'][i])
    return _ple(xf, p_all, i, pw['w_ple_gate'][i], pw['b_ple_gate'][i], pw['w_ple'][i])


def _trunk(x, p, pos0, past, pw):
    b, t, d = x.shape
    n = b * t
    tm = _row_tile(n)
    xf = x.reshape(n, d)

    if past is None:
        h, kb_new, vb_new, ka_new, va_new, kb_t = _proj_ab(xf, pw['w_in_ab_long'], t, pos0, pw['w_kb_t'],
                                                           BAND_T_BLOCK)
    else:
        h, kb_new, vb_new, ka_new, va_new = _proj_ab(xf, pw['w_in_ab'], t, pos0)
    h3 = h.reshape(b, t, h.shape[1])
    kb_new, vb_new, ka_new, va_new = [a.reshape(b, a.shape[0] // b, a.shape[1])
                                      for a in (kb_new, vb_new, ka_new, va_new)]
    q_a = (h3, A_Q_W, 0)
    q_b = (h3, B_W, 1)
    if past is None:
        ka_blk = (AB_KV_COL0 + B_W) // A_KV_W
        attn_a = _band_attn(q_a, (h3, A_KV_W, ka_blk), (h3, A_KV_W, ka_blk + 1), heads=A_HEADS,
                            kv_heads=A_KV_HEADS, band=WINDOW, tq=2 * CHUNK, past=0, sinks=pw['sinks_a'])
        bias_b = _rel_bias_folded(pw['rel_bias_b'], B_BAND_PAST, BAND_T_BLOCK)
        attn_b = _band_attn_t(q_b, kb_t, (h3, B_W, 2), bias_b, heads=B_HEADS, band=B_BAND_PAST)
        ak, av, bk, bv = ka_new[:, -WINDOW:], va_new[:, -WINDOW:], kb_new, vb_new
    else:
        n_past_a, n_past_b = past[0].shape[1], past[2].shape[1]
        full = [jnp.concatenate([c.reshape(b, c.shape[1], -1), new], axis=1)
                for c, new in zip(past[:4], (ka_new, va_new, kb_new, vb_new))]
        ak, av = full[0][:, -WINDOW:], full[1][:, -WINDOW:]
        bk, bv = full[2][:, -B_BAND_PAST:], full[3][:, -B_BAND_PAST:]
        k_a, v_a, k_b, v_b = [(f.astype(BF16), f.shape[2], 0) for f in full]
        attn_a = _band_attn(q_a, k_a, v_a, heads=A_HEADS, kv_heads=A_KV_HEADS, band=WINDOW, tq=CHUNK,
                            past=n_past_a, sinks=pw['sinks_a'])
        bias_b = _rel_bias(pw['rel_bias_b'], B_BAND_PAST, CHUNK)
        attn_b = _band_attn(q_b, k_b, v_b, heads=B_HEADS, kv_heads=B_HEADS, band=B_BAND_PAST, tq=CHUNK,
                            past=n_past_b, bias=bias_b)
    xf = _outproj_ln([attn_a.reshape(n, A_Q_W), attn_b.reshape(n, B_W)], pw['w_out_ab'], xf,
                     pw['ln1_g'][0], pw['ln1_b'][0])
    p_all = p.reshape(p.shape[0], n, p.shape[3])
    xf = _channel_mix(xf, p_all, pw, 0)

    cos_c, sin_c = _rope_tables(t, pos0, C_ROPE, max(t, tm))
    cq, ckv, kr, krp = _proj_c(xf, pw['w_in_c'], pw['g_q_c'], pw['g_kv_c'], cos_c, sin_c)
    q = _q_c(cq, pw['w_q_b_c'], cos_c, sin_c).reshape(b, t, C_HEADS * C_QK)
    if past is None:
        kt_c, krt_c, v_c = _kv_c(ckv, krp, pw['w_k_t_c'], pw['w_v_c'], b, min(MLA_LONG_BLOCK, t))
        attn_c = _mla_attn(q, kt_c, krt_c, v_c.reshape(b, t, C_HEADS * C_V), pos0=pos0)
    else:
        attn_c = _mla_absorbed(q, ckv.reshape(b, t, C_KV_RANK), kr.reshape(b, t, C_ROPE), past[4], past[5],
                               pw['w_k_t_c'], pw['w_v_c'], pos0=pos0)
    xf = _outproj_ln([attn_c.reshape(n, C_HEADS * C_V)], pw['w_out_c'], xf, pw['ln1_g'][1], pw['ln1_b'][1])
    xf = _channel_mix(xf, p_all, pw, 1)

    heads4 = lambda a, hh: a.reshape(1, b, a.shape[1], hh, HEAD_DIM)
    return (xf.reshape(b, t, d), heads4(ak, A_KV_HEADS), heads4(av, A_KV_HEADS), heads4(bk, B_HEADS),
            heads4(bv, B_HEADS), ckv.reshape(1, b, t, C_KV_RANK), kr.reshape(1, b, t, C_ROPE))


def kernel(x_prompt, x_sample, cache_a_k, cache_a_v, cache_b_k, cache_b_v, cache_c_kv, cache_c_krope, p_prompt,
           p_sample, w_in_ab, sinks_a, rel_bias_b, w_out_ab, w_in_c, g_q_c, w_q_b_c, g_kv_c, w_kv_b_c, w_out_c,
           ln1_g, ln1_b, ln2_g, ln2_b, w_mlp_up, w_mlp_down, w_ple_gate, b_ple_gate, w_ple):
    pw = _prepare_weights({
        'w_in_ab': w_in_ab, 'sinks_a': sinks_a, 'rel_bias_b': rel_bias_b, 'w_out_ab': w_out_ab,
        'w_in_c': w_in_c, 'g_q_c': g_q_c, 'w_q_b_c': w_q_b_c, 'g_kv_c': g_kv_c, 'w_kv_b_c': w_kv_b_c,
        'w_out_c': w_out_c, 'ln1_g': ln1_g, 'ln1_b': ln1_b, 'ln2_g': ln2_g, 'ln2_b': ln2_b,
        'w_mlp_up': w_mlp_up, 'w_mlp_down': w_mlp_down, 'w_ple_gate': w_ple_gate, 'b_ple_gate': b_ple_gate,
        'w_ple': w_ple,
    })
    prompt = _trunk(x_prompt, p_prompt, 0, None, pw)
    past = (cache_a_k[0], cache_a_v[0], cache_b_k[0], cache_b_v[0], cache_c_kv, cache_c_krope)
    sample = _trunk(x_sample, p_sample, cache_c_kv.shape[2], past, pw)
    return (prompt[0], sample[0]) + prompt[1:] + sample[1:]
```

```python
import functools

import jax
import jax.numpy as jnp
from jax import lax
from jax.experimental import pallas as pl
from jax.experimental.pallas import tpu as pltpu

F32 = jnp.float32
BF16 = jnp.bfloat16

CHUNK = 64
HEAD_DIM = 128
A_HEADS = 8
A_KV_HEADS = 2
WINDOW = 128
B_HEADS = 8
B_BAND_PAST = 512
REL_CLIP = 128
C_HEADS = 16
C_Q_RANK = 768
C_KV_RANK = 512
C_NOPE = 128
C_ROPE = 64
C_V = 128
DEPTH = 2
ROPE_THETA = 10000.0
LN_EPS = 1e-5
RMS_EPS = 1e-6
NEG_INF = -1e30
DEEPNORM_ALPHA = (2 * DEPTH) ** 0.25
LOG2_E = 1.4426950408889634
MLA_SCALE = (C_NOPE + C_ROPE) ** -0.5

A_Q_W = A_HEADS * HEAD_DIM
A_KV_W = A_KV_HEADS * HEAD_DIM
B_W = B_HEADS * HEAD_DIM
AB_IN_W = A_Q_W + 2 * A_KV_W + 3 * B_W
AB_KV_COL0 = A_Q_W + B_W
AB_KV_W = AB_IN_W - AB_KV_COL0
C_IN_W = C_Q_RANK + C_KV_RANK + C_ROPE
C_QK = 256

LANES = 128
V7X_VMEM_BYTES = 64 * 1024 * 1024
VMEM_LIMIT = V7X_VMEM_BYTES - 8 * 1024 * 1024

ROW_TILE = 512
COL_CHUNK = 512
MLP_ROW_TILE = 512
MLP_FF_TILE = 1024
BAND_T_BLOCK = 256


def _params(n_axes):
    return pltpu.CompilerParams(dimension_semantics=("arbitrary",) * n_axes, vmem_limit_bytes=VMEM_LIMIT)


def _resident(shape):
    nd = len(shape)
    return pl.BlockSpec(shape, lambda *_: (0,) * nd, pipeline_mode=pl.Buffered(1))


def _resident_slab(shape, index):
    nd = len(shape) - 1
    return pl.BlockSpec((None,) + tuple(shape[1:]), lambda *_: (index,) + (0,) * nd, pipeline_mode=pl.Buffered(1))


def _row_tile(n):
    return ROW_TILE if n % ROW_TILE == 0 else n


def _dot(a, b):
    return jnp.dot(a, b, preferred_element_type=F32)


def _dot_t(a, b):
    return lax.dot_general(a, b, (((1,), (1,)), ((), ())), preferred_element_type=F32)


def _layer_norm(y, g, b):
    mu = jnp.mean(y, -1, keepdims=True)
    var = jnp.mean(jnp.square(y - mu), -1, keepdims=True)
    return (y - mu) * lax.rsqrt(var + LN_EPS) * g + b


def _rms_norm(y, g):
    return y * lax.rsqrt(jnp.mean(jnp.square(y), -1, keepdims=True) + RMS_EPS) * g


def _rope_tile(t, cos, sin, d):
    if d == LANES:
        swapped = pltpu.roll(t, LANES // 2, 1)
    else:
        lane = lax.broadcasted_iota(jnp.int32, t.shape, 1)
        swapped = jnp.where((lane % d) < d // 2, pltpu.roll(t, LANES - d // 2, 1), pltpu.roll(t, d // 2, 1))
    return t * cos + swapped * sin


def _rope_tables(t, pos0, d, rows):
    half = d // 2
    inv = ROPE_THETA ** (-jnp.arange(half, dtype=F32) * (2.0 / d))
    ang = (jnp.arange(t, dtype=F32) + pos0)[:, None] * inv[None, :]
    cos = jnp.cos(ang)
    sin = jnp.sin(ang)
    reps = (rows // t, LANES // d)
    return jnp.tile(jnp.concatenate([cos, cos], 1), reps), jnp.tile(jnp.concatenate([-sin, sin], 1), reps)


def _proj_ab_kernel(*refs, kv_period, kb_transposed):
    refs = list(refs)
    x_ref, w_ref = refs[:2]
    wkbt_ref = refs.pop(2) if kb_transposed else None
    cos_ref, sin_ref, h_ref, kb_ref, vb_ref, ka_ref, va_ref = refs[2:9]
    kbt_ref = refs[9] if kb_transposed else None
    xb = x_ref[...].astype(BF16)
    cos = cos_ref[...]
    sin = sin_ref[...]
    keep_state = (pl.program_id(0) % kv_period) == kv_period - 1
    kv0 = AB_KV_COL0
    state_refs = []
    if not kb_transposed:
        state_refs.append((kv0, kb_ref))
        kv0 += B_W
    state_refs += [(kv0, vb_ref), (kv0 + B_W, ka_ref), (kv0 + B_W + A_KV_W, va_ref)]
    rope_tiles = set(range(A_HEADS)) | {(kv0 + B_W) // LANES + u for u in range(A_KV_HEADS)}
    for j in range(w_ref.shape[1] // COL_CHUNK):
        c0 = j * COL_CHUNK
        acc = _dot(xb, w_ref[:, c0:c0 + COL_CHUNK])
        parts = []
        for u in range(COL_CHUNK // LANES):
            part = acc[:, u * LANES:(u + 1) * LANES]
            if c0 // LANES + u in rope_tiles:
                part = _rope_tile(part, cos, sin, HEAD_DIM)
            parts.append(part)
        acc = jnp.concatenate(parts, axis=1)
        h_ref[:, c0:c0 + COL_CHUNK] = acc.astype(h_ref.dtype)
        for s0, ref in state_refs:
            lo, hi = max(c0, s0), min(c0 + COL_CHUNK, s0 + ref.shape[1])
            if lo < hi:
                @pl.when(keep_state)
                def _(acc=acc, ref=ref, lo=lo, hi=hi, s0=s0, c0=c0):
                    ref[:, lo - s0:hi - s0] = acc[:, lo - c0:hi - c0]
    if kb_transposed:
        kb_t = _dot_t(wkbt_ref[...], xb)
        blk = kbt_ref.shape[3]
        for c in range(kbt_ref.shape[1]):
            kbt_ref[0, c] = kb_t[:, c * blk:(c + 1) * blk].astype(kbt_ref.dtype)

        @pl.when(keep_state)
        def _():
            kb_ref[...] = kb_t.T


def _proj_ab(xf, w, seq, pos0, w_kb_t=None, kt_block=None):
    n, d = xf.shape
    tm = _row_tile(n)
    assert seq % tm == 0 or tm % seq == 0
    kv_period = max(seq // tm, 1)
    assert min(seq, B_BAND_PAST) == min(seq, tm)
    cos, sin = _rope_tables(seq, pos0, HEAD_DIM, max(seq, tm))
    n_tab = cos.shape[0] // tm
    state_widths = (B_W, B_W, A_KV_W, A_KV_W)
    kb_transposed = w_kb_t is not None
    rows = lambda i: (i, 0)
    tab = lambda i: (i % n_tab, 0)
    in_specs = [pl.BlockSpec((tm, d), rows), _resident(w.shape)]
    args = [xf, w]
    out_specs = [pl.BlockSpec((tm, w.shape[1]), rows)] + [
        pl.BlockSpec((tm, sw), lambda i: (i // kv_period, 0)) for sw in state_widths]
    out_shape = [jax.ShapeDtypeStruct((n, w.shape[1]), BF16)] + [
        jax.ShapeDtypeStruct((n // kv_period, sw), F32) for sw in state_widths]
    if kb_transposed:
        assert seq % tm == 0 and tm % kt_block == 0
        tiles = seq // tm
        in_specs.append(_resident(w_kb_t.shape))
        args.append(w_kb_t)
        out_specs.append(pl.BlockSpec((1, tm // kt_block, B_W, kt_block), lambda i: (i // tiles, i % tiles, 0, 0)))
        out_shape.append(jax.ShapeDtypeStruct((n // seq, seq // kt_block, B_W, kt_block), BF16))
    in_specs += [pl.BlockSpec((tm, LANES), tab), pl.BlockSpec((tm, LANES), tab)]
    args += [cos, sin]
    return pl.pallas_call(
        functools.partial(_proj_ab_kernel, kv_period=kv_period, kb_transposed=kb_transposed),
        grid=(n // tm,),
        in_specs=in_specs,
        out_specs=out_specs,
        out_shape=out_shape,
        compiler_params=_params(1),
        name="proj_ab",
    )(*args)


def _rel_bias_kernel(tab_ref, o_ref, *, band):
    h = pl.program_id(0)
    shape = o_ref.shape[1:]
    r = lax.broadcasted_iota(jnp.int32, shape, 0)
    w = lax.broadcasted_iota(jnp.int32, shape, 1)
    idx = jnp.clip(band + r - w, -REL_CLIP, REL_CLIP) + REL_CLIP

    def body(d, acc):
        return jnp.where(idx == d, tab_ref[h, d], acc)

    o_ref[0] = lax.fori_loop(0, 2 * REL_CLIP + 1, body, jnp.zeros(shape, F32))


def _rel_bias(table, band, tq):
    heads = table.shape[0]
    return pl.pallas_call(
        functools.partial(_rel_bias_kernel, band=band),
        grid=(heads,),
        in_specs=[pl.BlockSpec(memory_space=pltpu.SMEM)],
        out_specs=pl.BlockSpec((1, tq, band + tq), lambda h: (h, 0, 0)),
        out_shape=jax.ShapeDtypeStruct((heads, tq, band + tq), F32),
        compiler_params=_params(1),
        name="rel_bias",
    )(table)


def _band_attn_kernel(*refs, heads, kv_heads, band, tq, past, has_bias, has_sinks):
    q_ref, k_ref, v_ref = refs[:3]
    rest = list(refs[3:])
    bias_ref = rest.pop(0) if has_bias else None
    sink_ref = rest.pop(0) if has_sinks else None
    o_ref = rest.pop(0)

    width = band + tq
    scale = HEAD_DIM ** -0.5
    ws = past + pl.program_id(1) * tq - band
    r = lax.broadcasted_iota(jnp.int32, (tq, width), 0) // CHUNK
    w = lax.broadcasted_iota(jnp.int32, (tq, width), 1)
    wc = w // CHUNK
    allowed = (wc >= r) & (wc <= r + band // CHUNK) & (w + ws >= 0)

    def window(ref, cols):
        if past >= band:
            return ref[0, pl.ds(pl.multiple_of(ws, CHUNK), width), cols]
        pieces = [ref[0, pl.ds(pl.multiple_of(jnp.maximum(ws + c * LANES, 0), LANES), LANES), cols]
                  for c in range(width // LANES)]
        return jnp.concatenate(pieces, axis=0)

    group = heads // kv_heads
    outs = []
    for kh in range(kv_heads):
        cols = slice(kh * HEAD_DIM, (kh + 1) * HEAD_DIM)
        k_w = window(k_ref, cols)
        v_w = window(v_ref, cols)
        for g in range(group):
            h = kh * group + g
            hcols = slice(h * HEAD_DIM, (h + 1) * HEAD_DIM)
            s = _dot_t(q_ref[0, :, hcols], k_w) * scale
            if has_bias:
                s = s + bias_ref[h]
            s = jnp.where(allowed, s, NEG_INF)
            m = jnp.max(s, -1, keepdims=True)
            if has_sinks:
                sink = sink_ref[h]
                m = jnp.maximum(m, sink)
            e = jnp.exp(s - m)
            den = jnp.sum(e, -1, keepdims=True)
            if has_sinks:
                den = den + jnp.exp(sink - m)
            p = (e * (1.0 / den)).astype(BF16)
            outs.append(_dot(p, v_w).astype(o_ref.dtype))
    o_ref[0] = jnp.concatenate(outs, axis=1)


def _band_attn(q, k, v, *, heads, kv_heads, band, tq, past, bias=None, sinks=None):
    (qa, qw, qi), (ka, kw, ki), (va, vw, vi) = q, k, v
    b, t, _ = qa.shape
    s_len = ka.shape[1]
    assert t % tq == 0 and s_len == past + t and qw == heads * HEAD_DIM and kw == kv_heads * HEAD_DIM
    assert past >= band or (past == 0 and tq % LANES == 0 and band % LANES == 0)
    in_specs = [
        pl.BlockSpec((1, tq, qw), lambda bi, i: (bi, i, qi)),
        pl.BlockSpec((1, s_len, kw), lambda bi, i: (bi, 0, ki)),
        pl.BlockSpec((1, s_len, vw), lambda bi, i: (bi, 0, vi)),
    ]
    args = [qa, ka, va]
    if bias is not None:
        in_specs.append(_resident(bias.shape))
        args.append(bias)
    if sinks is not None:
        in_specs.append(pl.BlockSpec(memory_space=pltpu.SMEM))
        args.append(sinks)
    return pl.pallas_call(
        functools.partial(_band_attn_kernel, heads=heads, kv_heads=kv_heads, band=band, tq=tq, past=past,
                          has_bias=bias is not None, has_sinks=sinks is not None),
        grid=(b, t // tq),
        in_specs=in_specs,
        out_specs=pl.BlockSpec((1, tq, qw), lambda bi, i: (bi, i, 0)),
        out_shape=jax.ShapeDtypeStruct((b, t, qw), BF16),
        compiler_params=_params(2),
        name="band_attn",
    )(*args)


def _rel_bias_folded_kernel(tab_ref, o_ref, *, band):
    h = pl.program_id(0)
    tq, width = o_ref.shape[1:]
    period = tq + width
    j = lax.broadcasted_iota(jnp.int32, (8, period), 1)
    dist = jnp.where(j < width, band - j, band - (j - period))
    idx = jnp.clip(dist, -REL_CLIP, REL_CLIP) + REL_CLIP

    def body(d, acc):
        return jnp.where(idx == d, tab_ref[h, d], acc)

    g = lax.fori_loop(0, 2 * REL_CLIP + 1, body, jnp.zeros((8, period), F32))
    full = jnp.concatenate([g] * (tq // 8), axis=0)
    bias = pltpu.roll(full, 0, 1, stride=1, stride_axis=0)[:, :width]
    rc = lax.broadcasted_iota(jnp.int32, (tq, width), 0) // CHUNK
    wc = lax.broadcasted_iota(jnp.int32, (tq, width), 1) // CHUNK
    o_ref[0] = jnp.where((wc >= rc) & (wc <= rc + band // CHUNK), bias * LOG2_E, NEG_INF)


def _rel_bias_folded(table, band, tq):
    heads = table.shape[0]
    assert (band + 2 * tq) % LANES == 0 and tq % 8 == 0
    return pl.pallas_call(
        functools.partial(_rel_bias_folded_kernel, band=band),
        grid=(heads,),
        in_specs=[pl.BlockSpec(memory_space=pltpu.SMEM)],
        out_specs=pl.BlockSpec((1, tq, band + tq), lambda h: (h, 0, 0)),
        out_shape=jax.ShapeDtypeStruct((heads, tq, band + tq), F32),
        compiler_params=_params(1),
        name="rel_bias_folded",
    )(table)


def _band_attn_t_kernel(q_ref, kt_ref, v_ref, bias_ref, o_ref, *, heads, band):
    tq = kt_ref.shape[3]
    n_past = band // tq
    width = band + tq
    i = pl.program_id(1)
    ws = i * tq - band
    col_valid = (lax.broadcasted_iota(jnp.int32, (1, width), 1) + ws) >= 0
    ones = jnp.ones((width, HEAD_DIM), BF16)
    outs = []
    for h in range(heads):
        hc = slice(h * HEAD_DIM, (h + 1) * HEAD_DIM)
        q_h = q_ref[0, :, hc]
        s = jnp.concatenate([_dot(q_h, kt_ref[0, jnp.maximum(i - n_past + c, 0), hc, :])
                             for c in range(n_past + 1)], axis=1)
        s = s * (HEAD_DIM ** -0.5 * LOG2_E) + bias_ref[h]
        s = jnp.where(col_valid, s, NEG_INF)
        p = jnp.exp2(s - jnp.max(s, -1, keepdims=True)).astype(BF16)
        v_w = jnp.concatenate([v_ref[0, pl.ds(pl.multiple_of(jnp.maximum(ws + c * tq, 0), tq), tq), hc]
                               for c in range(n_past + 1)], axis=0)
        o_ext = _dot(p, jnp.concatenate([v_w, ones], axis=1))
        outs.append((o_ext[:, :HEAD_DIM] * (1.0 / o_ext[:, HEAD_DIM:])).astype(o_ref.dtype))
    o_ref[0] = jnp.concatenate(outs, axis=1)


def _band_attn_t(q, kt, v, bias, *, heads, band):
    (qa, qw, qi), (va, vw, vi) = q, v
    b, t, _ = qa.shape
    _, nkb, _, tq = kt.shape
    assert nkb * tq == t and band % tq == 0 and qw == vw == heads * HEAD_DIM
    return pl.pallas_call(
        functools.partial(_band_attn_t_kernel, heads=heads, band=band),
        grid=(b, t // tq),
        in_specs=[
            pl.BlockSpec((1, tq, qw), lambda bi, i: (bi, i, qi)),
            pl.BlockSpec((1, nkb, heads * HEAD_DIM, tq), lambda bi, i: (bi, 0, 0, 0)),
            pl.BlockSpec((1, t, vw), lambda bi, i: (bi, 0, vi)),
            _resident(bias.shape),
        ],
        out_specs=pl.BlockSpec((1, tq, qw), lambda bi, i: (bi, i, 0)),
        out_shape=jax.ShapeDtypeStruct((b, t, qw), BF16),
        compiler_params=_params(2),
        name="band_attn_t",
    )(qa, kt, va, bias)


def _outproj_ln_kernel(*refs, n_in):
    a_refs = refs[:n_in]
    w_ref, x_ref, g_ref, b_ref, o_ref = refs[n_in:]
    d_out = o_ref.shape[1]
    for j in range(d_out // COL_CHUNK):
        cols = slice(j * COL_CHUNK, (j + 1) * COL_CHUNK)
        y = DEEPNORM_ALPHA * x_ref[:, cols]
        r0 = 0
        for a_ref in a_refs:
            kk = a_ref.shape[1]
            y = y + _dot(a_ref[...], w_ref[r0:r0 + kk, cols])
            r0 += kk
        o_ref[:, cols] = y
    o_ref[...] = _layer_norm(o_ref[...], g_ref[...], b_ref[...])


def _outproj_ln(a_list, w, xf, g, b):
    n, d = xf.shape
    tm = _row_tile(n)
    assert sum(a.shape[1] for a in a_list) == w.shape[0]
    return pl.pallas_call(
        functools.partial(_outproj_ln_kernel, n_in=len(a_list)),
        grid=(n // tm,),
        in_specs=[pl.BlockSpec((tm, a.shape[1]), lambda i: (i, 0)) for a in a_list] + [
            _resident(w.shape),
            pl.BlockSpec((tm, d), lambda i: (i, 0)),
            _resident(g.shape),
            _resident(b.shape),
        ],
        out_specs=pl.BlockSpec((tm, d), lambda i: (i, 0)),
        out_shape=jax.ShapeDtypeStruct((n, d), F32),
        compiler_params=_params(1),
        name="outproj_ln",
    )(*a_list, w, xf, g, b)


def _mlp_ln_kernel(x_ref, wu_ref, wd_ref, g_ref, b_ref, o_ref, xb_ref):
    f = pl.program_id(1)

    @pl.when(f == 0)
    def _():
        xb_ref[...] = x_ref[...].astype(BF16)
        o_ref[...] = jnp.zeros(o_ref.shape, o_ref.dtype)

    hid = _dot(xb_ref[...], wu_ref[...])
    hid = jnp.square(jnp.maximum(hid, 0.0)).astype(BF16)
    for j in range(o_ref.shape[1] // COL_CHUNK):
        cols = slice(j * COL_CHUNK, (j + 1) * COL_CHUNK)
        o_ref[:, cols] += _dot(hid, wd_ref[:, cols])

    @pl.when(f == pl.num_programs(1) - 1)
    def _():
        o_ref[...] = _layer_norm(DEEPNORM_ALPHA * x_ref[...] + o_ref[...], g_ref[...], b_ref[...])


def _mlp_ln(xf, layer, w_up, w_down, g, b):
    n, d = xf.shape
    d_ff = w_up.shape[2]
    tm = MLP_ROW_TILE if n % MLP_ROW_TILE == 0 else n
    tf = MLP_FF_TILE
    return pl.pallas_call(
        _mlp_ln_kernel,
        grid=(n // tm, d_ff // tf),
        in_specs=[
            pl.BlockSpec((tm, d), lambda i, f: (i, 0)),
            pl.BlockSpec((None, d, tf), lambda i, f: (layer, 0, f)),
            pl.BlockSpec((None, tf, d), lambda i, f: (layer, f, 0)),
            _resident(g.shape),
            _resident(b.shape),
        ],
        out_specs=pl.BlockSpec((tm, d), lambda i, f: (i, 0)),
        out_shape=jax.ShapeDtypeStruct((n, d), F32),
        scratch_shapes=[pltpu.VMEM((tm, d), BF16)],
        compiler_params=_params(2),
        name="mlp_ln",
    )(xf, w_up, w_down, g, b)


def _ple_kernel(x_ref, p_ref, wg_ref, bg_ref, wp_ref, o_ref):
    xb = x_ref[...].astype(BF16)
    pb = p_ref[...].astype(BF16)
    for j in range(o_ref.shape[1] // COL_CHUNK):
        cols = slice(j * COL_CHUNK, (j + 1) * COL_CHUNK)
        gate = jax.nn.sigmoid(_dot(xb, wg_ref[:, cols]) + bg_ref[:, cols])
        o_ref[:, cols] = x_ref[:, cols] + gate * _dot(pb, wp_ref[:, cols])


def _ple(xf, p_all, layer, wg, bg, wp):
    n, d = xf.shape
    tm = _row_tile(n)
    return pl.pallas_call(
        _ple_kernel,
        grid=(n // tm,),
        in_specs=[
            pl.BlockSpec((tm, d), lambda i: (i, 0)),
            pl.BlockSpec((None, tm, p_all.shape[2]), lambda i: (layer, i, 0)),
            _resident_slab(wg.shape, layer),
            _resident(bg.shape),
            _resident_slab(wp.shape, layer),
        ],
        out_specs=pl.BlockSpec((tm, d), lambda i: (i, 0)),
        out_shape=jax.ShapeDtypeStruct((n, d), F32),
        compiler_params=_params(1),
        name="ple",
    )(xf, p_all, wg, bg, wp)


Q_GROUP = 4


def _proj_cq_kernel(x_ref, w_ref, gq_ref, gkv_ref, cos_ref, sin_ref, wq_ref, q_ref, ckv_ref, kr_ref, krp_ref):
    cos = cos_ref[...]
    sin = sin_ref[...]
    h = _dot(x_ref[...].astype(BF16), w_ref[...])
    cq = _rms_norm(h[:, :C_Q_RANK], gq_ref[...]).astype(BF16)
    ckv_ref[...] = _rms_norm(h[:, C_Q_RANK:C_Q_RANK + C_KV_RANK], gkv_ref[...])
    rot = _rope_tile(h[:, C_Q_RANK + C_KV_RANK:], cos, sin, C_ROPE)
    lane = lax.broadcasted_iota(jnp.int32, rot.shape, 1)
    rot = jnp.where(lane < C_ROPE, rot, 0.0)
    kr_ref[...] = rot[:, :C_ROPE]
    krp_ref[...] = rot.astype(krp_ref.dtype)

    grp_w = Q_GROUP * (C_NOPE + C_ROPE)
    for gi in range(C_HEADS // Q_GROUP):
        acc = _dot(cq, wq_ref[:, gi * grp_w:(gi + 1) * grp_w]) * (MLA_SCALE * LOG2_E)
        for u in range(Q_GROUP):
            o0 = (gi * Q_GROUP + u) * C_QK
            q_ref[:, o0:o0 + C_NOPE] = acc[:, u * C_NOPE:(u + 1) * C_NOPE].astype(q_ref.dtype)
            if u % 2 == 0:
                r0 = Q_GROUP * C_NOPE + (u // 2) * LANES
                qrot = _rope_tile(acc[:, r0:r0 + LANES], cos, sin, C_ROPE)
                piece = qrot
            else:
                piece = pltpu.roll(qrot, C_ROPE, 1)
            q_ref[:, o0 + C_NOPE:o0 + C_QK] = jnp.where(lane < C_ROPE, piece, 0.0).astype(q_ref.dtype)


def _proj_cq(xf, w, gq, gkv, cos, sin, wq):
    n, d = xf.shape
    tm = _row_tile(n)
    n_tab = cos.shape[0] // tm
    rows = lambda i: (i, 0)
    tab = lambda i: (i % n_tab, 0)
    return pl.pallas_call(
        _proj_cq_kernel,
        grid=(n // tm,),
        in_specs=[
            pl.BlockSpec((tm, d), rows),
            _resident(w.shape),
            _resident(gq.shape),
            _resident(gkv.shape),
            pl.BlockSpec((tm, LANES), tab),
            pl.BlockSpec((tm, LANES), tab),
            _resident(wq.shape),
        ],
        out_specs=[
            pl.BlockSpec((tm, C_HEADS * C_QK), rows),
            pl.BlockSpec((tm, C_KV_RANK), rows),
            pl.BlockSpec((tm, C_ROPE), rows),
            pl.BlockSpec((tm, LANES), rows),
        ],
        out_shape=[
            jax.ShapeDtypeStruct((n, C_HEADS * C_QK), BF16),
            jax.ShapeDtypeStruct((n, C_KV_RANK), F32),
            jax.ShapeDtypeStruct((n, C_ROPE), F32),
            jax.ShapeDtypeStruct((n, LANES), BF16),
        ],
        compiler_params=_params(1),
        name="proj_cq",
    )(xf, w, gq, gkv, cos, sin, wq)


def _kv_c_kernel(ckv_ref, krp_ref, wkt_ref, wv_ref, kt_ref, krt_ref, v_ref):
    cb = ckv_ref[...].astype(BF16)
    eye = (lax.broadcasted_iota(jnp.int32, (LANES, LANES), 0)
           == lax.broadcasted_iota(jnp.int32, (LANES, LANES), 1)).astype(F32).astype(BF16)
    krt_ref[0, 0] = _dot_t(eye, krp_ref[...]).astype(krt_ref.dtype)
    kt_ref[0, 0] = _dot_t(wkt_ref[...], cb).astype(kt_ref.dtype)
    for j in range(C_HEADS * C_V // COL_CHUNK):
        cols = slice(j * COL_CHUNK, (j + 1) * COL_CHUNK)
        v_ref[:, cols] = _dot(cb, wv_ref[:, cols]).astype(v_ref.dtype)


def _kv_c(ckv, krp, wkt, wv, batch, tk):
    n = ckv.shape[0]
    nkb = n // batch // tk
    assert n == batch * nkb * tk
    rows = lambda i: (i, 0)
    blk = lambda i: (i // nkb, i % nkb, 0, 0)
    return pl.pallas_call(
        _kv_c_kernel,
        grid=(n // tk,),
        in_specs=[pl.BlockSpec((tk, C_KV_RANK), rows), pl.BlockSpec((tk, LANES), rows), _resident(wkt.shape),
                  _resident(wv.shape)],
        out_specs=[pl.BlockSpec((1, 1, C_HEADS * C_NOPE, tk), blk), pl.BlockSpec((1, 1, LANES, tk), blk),
                   pl.BlockSpec((tk, C_HEADS * C_V), rows)],
        out_shape=[
            jax.ShapeDtypeStruct((batch, nkb, C_HEADS * C_NOPE, tk), BF16),
            jax.ShapeDtypeStruct((batch, nkb, LANES, tk), BF16),
            jax.ShapeDtypeStruct((n, C_HEADS * C_V), BF16),
        ],
        compiler_params=_params(1),
        name="kv_c",
    )(ckv, krp, wkt, wv)


MLA_LONG_BLOCK = 512


def _mla_attn_kernel(q_ref, kt_ref, krt_ref, v_ref, o_ref, *, pos0, heads):
    t = q_ref.shape[1]
    tq = tk = kt_ref.shape[3]
    half = tq // 2
    row = lax.broadcasted_iota(jnp.int32, (tq, tk), 0) // CHUNK
    col = lax.broadcasted_iota(jnp.int32, (tq, tk), 1) // CHUNK
    diag_ok = col <= row
    ones = jnp.ones((tk, C_V), BF16)

    def k_slab(g, kb):
        return jnp.concatenate([kt_ref[0, kb, g * C_NOPE:(g + 1) * C_NOPE, :], krt_ref[0, kb]], axis=0)

    def v_slab(g, kb):
        return jnp.concatenate([v_ref[0, kb * tk:(kb + 1) * tk, g * C_V:(g + 1) * C_V], ones], axis=1)

    def step(s, m, acc, v_ext):
        m_new = jnp.maximum(m, jnp.max(s, -1, keepdims=True))
        p = jnp.exp2(s - m_new).astype(BF16)
        return m_new, jnp.exp2(m - m_new) * acc + _dot(p, v_ext)

    def finish(acc):
        return (acc[:, :C_V] * (1.0 / acc[:, C_V:])).astype(o_ref.dtype)

    for qi in range(t // tq):
        q0 = qi * tq
        qs = [q_ref[0, q0:q0 + tq, g * C_QK:(g + 1) * C_QK] for g in range(heads)]
        n_full = (pos0 + q0) // tk
        state = [(jnp.full((tq, 1), NEG_INF, F32), jnp.zeros((tq, 2 * C_V), F32)) for _ in range(heads)]
        for kb in range(n_full):
            raw = [_dot(qs[g], k_slab(g, kb)) for g in range(heads)]
            state = [step(raw[g], *state[g], v_slab(g, kb)) for g in range(heads)]
        for g in range(heads):
            m, acc = state[g]
            k_d, v_d = k_slab(g, n_full), v_slab(g, n_full)
            s_top = jnp.where(diag_ok[:half, :half], _dot(qs[g][:half], k_d[:, :half]), NEG_INF)
            _, acc_top = step(s_top, m[:half], acc[:half], v_d[:half])
            s_bot = jnp.where(diag_ok[half:], _dot(qs[g][half:], k_d), NEG_INF)
            _, acc_bot = step(s_bot, m[half:], acc[half:], v_d)
            o_ref[0, q0:q0 + half, g * C_V:(g + 1) * C_V] = finish(acc_top)
            o_ref[0, q0 + half:q0 + tq, g * C_V:(g + 1) * C_V] = finish(acc_bot)


def _mla_attn(q, kt, krt, v, *, pos0, heads=2):
    b, t, _ = q.shape
    _, nkb, _, tk = kt.shape
    s_len = nkb * tk
    assert t % tk == 0 and pos0 % tk == 0 and pos0 + t <= s_len and v.shape[1] == s_len
    return pl.pallas_call(
        functools.partial(_mla_attn_kernel, pos0=pos0, heads=heads),
        grid=(b, C_HEADS // heads),
        in_specs=[
            pl.BlockSpec((1, t, heads * C_QK), lambda bi, h: (bi, 0, h)),
            pl.BlockSpec((1, nkb, heads * C_NOPE, tk), lambda bi, h: (bi, 0, h, 0)),
            pl.BlockSpec((1, nkb, LANES, tk), lambda bi, h: (bi, 0, 0, 0)),
            pl.BlockSpec((1, s_len, heads * C_V), lambda bi, h: (bi, 0, h)),
        ],
        out_specs=pl.BlockSpec((1, t, heads * C_V), lambda bi, h: (bi, 0, h)),
        out_shape=jax.ShapeDtypeStruct((b, t, C_HEADS * C_V), BF16),
        compiler_params=_params(2),
        name="mla_attn",
    )(q, kt, krt, v)


def _mla_absorbed_kernel(q_ref, ckv_new_ref, kr_new_ref, ckv_past_ref, kr_past_ref, wkt_ref, wv_ref, o_ref):
    t = q_ref.shape[1]
    q_lat, q_rope = [], []
    for h in range(C_HEADS):
        q_lat.append(_dot(q_ref[0, :, h * C_QK:h * C_QK + C_NOPE], wkt_ref[h * C_NOPE:(h + 1) * C_NOPE, :]))
        q_rope.append(q_ref[0, :, h * C_QK + C_NOPE:h * C_QK + C_NOPE + C_ROPE])
    q_lat = jnp.concatenate(q_lat, axis=0).astype(BF16)
    q_rope = jnp.concatenate(q_rope, axis=0)
    lat = [ckv_past_ref[...].astype(BF16), ckv_new_ref[0].astype(BF16)]
    rot = [kr_past_ref[...].astype(BF16), kr_new_ref[0].astype(BF16)]
    s = jnp.concatenate([_dot_t(q_lat, c) + _dot_t(q_rope, r) for c, r in zip(lat, rot)], axis=1)
    e = jnp.exp2(s - jnp.max(s, -1, keepdims=True))
    p = (e * (1.0 / jnp.sum(e, -1, keepdims=True))).astype(BF16)
    n_past = lat[0].shape[0]
    o_lat = (_dot(p[:, :n_past], lat[0]) + _dot(p[:, n_past:], lat[1])).astype(BF16)
    o_ref[0] = jnp.concatenate([_dot(o_lat[h * t:(h + 1) * t], wv_ref[:, h * C_V:(h + 1) * C_V])
                                for h in range(C_HEADS)], axis=1).astype(o_ref.dtype)


def _mla_absorbed(q, ckv_new, kr_new, ckv_past, kr_past, wkt, wv, *, pos0):
    b, t, _ = q.shape
    n_past = ckv_past.shape[2]
    assert t == CHUNK and pos0 % CHUNK == 0 and n_past <= pos0
    return pl.pallas_call(
        _mla_absorbed_kernel,
        grid=(b,),
        in_specs=[
            pl.BlockSpec((1, t, C_HEADS * C_QK), lambda bi: (bi, 0, 0)),
            pl.BlockSpec((1, t, C_KV_RANK), lambda bi: (bi, 0, 0)),
            pl.BlockSpec((1, t, C_ROPE), lambda bi: (bi, 0, 0)),
            pl.BlockSpec((None, None, n_past, C_KV_RANK), lambda bi: (0, bi, 0, 0)),
            pl.BlockSpec((None, None, n_past, C_ROPE), lambda bi: (0, bi, 0, 0)),
            _resident(wkt.shape),
            _resident(wv.shape),
        ],
        out_specs=pl.BlockSpec((1, t, C_HEADS * C_V), lambda bi: (bi, 0, 0)),
        out_shape=jax.ShapeDtypeStruct((b, t, C_HEADS * C_V), BF16),
        compiler_params=_params(1),
        name="mla_absorbed",
    )(q, ckv_new, kr_new, ckv_past, kr_past, wkt, wv)


def _prepare_weights(w):
    o1 = A_Q_W
    o2 = o1 + A_KV_W
    o3 = o2 + A_KV_W
    o4 = o3 + B_W
    o5 = o4 + B_W
    w_ab = w['w_in_ab'][0]
    w_ab = jnp.concatenate([w_ab[:, :o1], w_ab[:, o3:o4], w_ab[:, o4:o5], w_ab[:, o5:], w_ab[:, o1:o2],
                            w_ab[:, o2:o3]], axis=1)
    w_ab_long = jnp.concatenate([w_ab[:, :AB_KV_COL0], w_ab[:, AB_KV_COL0 + B_W:]], axis=1)
    w_kb_t = w_ab[:, AB_KV_COL0:AB_KV_COL0 + B_W].T
    w_c = jnp.pad(w['w_in_c'][0], ((0, 0), (0, LANES - C_ROPE)))
    hq = C_NOPE + C_ROPE
    q_cols = []
    for g0 in range(0, C_HEADS, Q_GROUP):
        q_cols += [jnp.arange(h * hq, h * hq + C_NOPE) for h in range(g0, g0 + Q_GROUP)]
        q_cols += [jnp.arange(h * hq + C_NOPE, (h + 1) * hq) for h in range(g0, g0 + Q_GROUP)]
    w_q = w['w_q_b_c'][0][:, jnp.concatenate(q_cols)]
    hkv = C_NOPE + C_V
    k_cols = jnp.concatenate([jnp.arange(h * hkv, h * hkv + C_NOPE) for h in range(C_HEADS)])
    v_cols = jnp.concatenate([jnp.arange(h * hkv + C_NOPE, (h + 1) * hkv) for h in range(C_HEADS)])
    w_k_t = w['w_kv_b_c'][0][:, k_cols].T
    w_v = w['w_kv_b_c'][0][:, v_cols]
    row = lambda a: a.reshape(1, -1)
    return {
        'w_in_ab': w_ab.astype(BF16), 'w_in_ab_long': w_ab_long.astype(BF16), 'w_kb_t': w_kb_t.astype(BF16),
        'w_out_ab': w['w_out_ab'][0].astype(BF16),
        'w_in_c': w_c.astype(BF16), 'w_q_b_c': w_q.astype(BF16), 'w_k_t_c': w_k_t.astype(BF16),
        'w_v_c': w_v.astype(BF16),
        'w_out_c': w['w_out_c'][0].astype(BF16),
        'g_q_c': row(w['g_q_c'][0]), 'g_kv_c': row(w['g_kv_c'][0]),
        'sinks_a': w['sinks_a'][0], 'rel_bias_b': w['rel_bias_b'][0],
        'ln1_g': [row(w['ln1_g'][i]) for i in range(DEPTH)], 'ln1_b': [row(w['ln1_b'][i]) for i in range(DEPTH)],
        'ln2_g': [row(w['ln2_g'][i]) for i in range(DEPTH)], 'ln2_b': [row(w['ln2_b'][i]) for i in range(DEPTH)],
        'w_mlp_up': w['w_mlp_up'].astype(BF16), 'w_mlp_down': w['w_mlp_down'].astype(BF16),
        'w_ple_gate': w['w_ple_gate'].astype(BF16), 'w_ple': w['w_ple'].astype(BF16),
        'b_ple_gate': [row(w['b_ple_gate'][i]) for i in range(DEPTH)],
    }


def _channel_mix(xf, p_all, pw, i):
    xf = _mlp_ln(xf, i, pw['w_mlp_up'], pw['w_mlp_down'], pw['ln2_g'][i], pw['ln2_b'][i])
    return _ple(xf, p_all, i, pw['w_ple_gate'], pw['b_ple_gate'][i], pw['w_ple'])


def _trunk(x, p, pos0, past, pw):
    b, t, d = x.shape
    n = b * t
    tm = _row_tile(n)
    xf = x.reshape(n, d)

    if past is None:
        h, kb_new, vb_new, ka_new, va_new, kb_t = _proj_ab(xf, pw['w_in_ab_long'], t, pos0, pw['w_kb_t'],
                                                           BAND_T_BLOCK)
    else:
        h, kb_new, vb_new, ka_new, va_new = _proj_ab(xf, pw['w_in_ab'], t, pos0)
    h3 = h.reshape(b, t, h.shape[1])
    kb_new, vb_new, ka_new, va_new = [a.reshape(b, a.shape[0] // b, a.shape[1])
                                      for a in (kb_new, vb_new, ka_new, va_new)]
    q_a = (h3, A_Q_W, 0)
    q_b = (h3, B_W, 1)
    if past is None:
        ka_blk = (AB_KV_COL0 + B_W) // A_KV_W
        attn_a = _band_attn(q_a, (h3, A_KV_W, ka_blk), (h3, A_KV_W, ka_blk + 1), heads=A_HEADS,
                            kv_heads=A_KV_HEADS, band=WINDOW, tq=2 * CHUNK, past=0, sinks=pw['sinks_a'])
        bias_b = _rel_bias_folded(pw['rel_bias_b'], B_BAND_PAST, BAND_T_BLOCK)
        attn_b = _band_attn_t(q_b, kb_t, (h3, B_W, 2), bias_b, heads=B_HEADS, band=B_BAND_PAST)
        ak, av, bk, bv = ka_new[:, -WINDOW:], va_new[:, -WINDOW:], kb_new, vb_new
    else:
        n_past_a, n_past_b = past[0].shape[1], past[2].shape[1]
        full = [jnp.concatenate([c.reshape(b, c.shape[1], -1), new], axis=1)
                for c, new in zip(past[:4], (ka_new, va_new, kb_new, vb_new))]
        ak, av = full[0][:, -WINDOW:], full[1][:, -WINDOW:]
        bk, bv = full[2][:, -B_BAND_PAST:], full[3][:, -B_BAND_PAST:]
        k_a, v_a, k_b, v_b = [(f.astype(BF16), f.shape[2], 0) for f in full]
        attn_a = _band_attn(q_a, k_a, v_a, heads=A_HEADS, kv_heads=A_KV_HEADS, band=WINDOW, tq=CHUNK,
                            past=n_past_a, sinks=pw['sinks_a'])
        bias_b = _rel_bias(pw['rel_bias_b'], B_BAND_PAST, CHUNK)
        attn_b = _band_attn(q_b, k_b, v_b, heads=B_HEADS, kv_heads=B_HEADS, band=B_BAND_PAST, tq=CHUNK,
                            past=n_past_b, bias=bias_b)
    xf = _outproj_ln([attn_a.reshape(n, A_Q_W), attn_b.reshape(n, B_W)], pw['w_out_ab'], xf,
                     pw['ln1_g'][0], pw['ln1_b'][0])
    p_all = p.reshape(p.shape[0], n, p.shape[3])
    xf = _channel_mix(xf, p_all, pw, 0)

    cos_c, sin_c = _rope_tables(t, pos0, C_ROPE, max(t, tm))
    q, ckv, kr, krp = _proj_cq(xf, pw['w_in_c'], pw['g_q_c'], pw['g_kv_c'], cos_c, sin_c, pw['w_q_b_c'])
    q = q.reshape(b, t, C_HEADS * C_QK)
    if past is None:
        kt_c, krt_c, v_c = _kv_c(ckv, krp, pw['w_k_t_c'], pw['w_v_c'], b, min(MLA_LONG_BLOCK, t))
        attn_c = _mla_attn(q, kt_c, krt_c, v_c.reshape(b, t, C_HEADS * C_V), pos0=pos0)
    else:
        attn_c = _mla_absorbed(q, ckv.reshape(b, t, C_KV_RANK), kr.reshape(b, t, C_ROPE), past[4], past[5],
                               pw['w_k_t_c'], pw['w_v_c'], pos0=pos0)
    xf = _outproj_ln([attn_c.reshape(n, C_HEADS * C_V)], pw['w_out_c'], xf, pw['ln1_g'][1], pw['ln1_b'][1])
    xf = _channel_mix(xf, p_all, pw, 1)

    heads4 = lambda a, hh: a.reshape(1, b, a.shape[1], hh, HEAD_DIM)
    return (xf.reshape(b, t, d), heads4(ak, A_KV_HEADS), heads4(av, A_KV_HEADS), heads4(bk, B_HEADS),
            heads4(bv, B_HEADS), ckv.reshape(1, b, t, C_KV_RANK), kr.reshape(1, b, t, C_ROPE))


def kernel(x_prompt, x_sample, cache_a_k, cache_a_v, cache_b_k, cache_b_v, cache_c_kv, cache_c_krope, p_prompt,
           p_sample, w_in_ab, sinks_a, rel_bias_b, w_out_ab, w_in_c, g_q_c, w_q_b_c, g_kv_c, w_kv_b_c, w_out_c,
           ln1_g, ln1_b, ln2_g, ln2_b, w_mlp_up, w_mlp_down, w_ple_gate, b_ple_gate, w_ple):
    pw = _prepare_weights({
        'w_in_ab': w_in_ab, 'sinks_a': sinks_a, 'rel_bias_b': rel_bias_b, 'w_out_ab': w_out_ab,
        'w_in_c': w_in_c, 'g_q_c': g_q_c, 'w_q_b_c': w_q_b_c, 'g_kv_c': g_kv_c, 'w_kv_b_c': w_kv_b_c,
        'w_out_c': w_out_c, 'ln1_g': ln1_g, 'ln1_b': ln1_b, 'ln2_g': ln2_g, 'ln2_b': ln2_b,
        'w_mlp_up': w_mlp_up, 'w_mlp_down': w_mlp_down, 'w_ple_gate': w_ple_gate, 'b_ple_gate': b_ple_gate,
        'w_ple': w_ple,
    })
    prompt = _trunk(x_prompt, p_prompt, 0, None, pw)
    past = (cache_a_k[0], cache_a_v[0], cache_b_k[0], cache_b_v[0], cache_c_kv, cache_c_krope)
    sample = _trunk(x_sample, p_sample, cache_c_kv.shape[2], past, pw)
    return (prompt[0], sample[0]) + prompt[1:] + sample[1:]
```

```python
import functools

import jax
import jax.numpy as jnp
from jax import lax
from jax.experimental import pallas as pl
from jax.experimental.pallas import tpu as pltpu

F32 = jnp.float32
BF16 = jnp.bfloat16

CHUNK = 64
HEAD_DIM = 128
A_HEADS = 8
A_KV_HEADS = 2
WINDOW = 128
B_HEADS = 8
B_BAND_PAST = 512
REL_CLIP = 128
C_HEADS = 16
C_Q_RANK = 768
C_KV_RANK = 512
C_NOPE = 128
C_ROPE = 64
C_V = 128
DEPTH = 2
ROPE_THETA = 10000.0
LN_EPS = 1e-5
RMS_EPS = 1e-6
NEG_INF = -1e30
DEEPNORM_ALPHA = (2 * DEPTH) ** 0.25
LOG2_E = 1.4426950408889634
MLA_SCALE = (C_NOPE + C_ROPE) ** -0.5

A_Q_W = A_HEADS * HEAD_DIM
A_KV_W = A_KV_HEADS * HEAD_DIM
B_W = B_HEADS * HEAD_DIM
AB_IN_W = A_Q_W + 2 * A_KV_W + 3 * B_W
AB_KV_COL0 = A_Q_W + B_W
AB_KV_W = AB_IN_W - AB_KV_COL0
C_IN_W = C_Q_RANK + C_KV_RANK + C_ROPE
C_QK = 256

LANES = 128
V7X_VMEM_BYTES = 64 * 1024 * 1024
VMEM_LIMIT = V7X_VMEM_BYTES - 8 * 1024 * 1024

ROW_TILE = 512
COL_CHUNK = 512
MLP_ROW_TILE = 512
MLP_FF_TILE = 1024
BAND_B_BLOCK = 256
BAND_A_BLOCK = 128


def _params(n_axes):
    return pltpu.CompilerParams(dimension_semantics=("arbitrary",) * n_axes, vmem_limit_bytes=VMEM_LIMIT)


def _resident(shape):
    nd = len(shape)
    return pl.BlockSpec(shape, lambda *_: (0,) * nd, pipeline_mode=pl.Buffered(1))


def _resident_slab(shape, index):
    nd = len(shape) - 1
    return pl.BlockSpec((None,) + tuple(shape[1:]), lambda *_: (index,) + (0,) * nd, pipeline_mode=pl.Buffered(1))


def _row_tile(n):
    return ROW_TILE if n % ROW_TILE == 0 else n


def _dot(a, b):
    return jnp.dot(a, b, preferred_element_type=F32)


def _dot_t(a, b):
    return lax.dot_general(a, b, (((1,), (1,)), ((), ())), preferred_element_type=F32)


def _layer_norm(y, g, b):
    mu = jnp.mean(y, -1, keepdims=True)
    var = jnp.mean(jnp.square(y - mu), -1, keepdims=True)
    return (y - mu) * lax.rsqrt(var + LN_EPS) * g + b


def _rms_norm(y, g):
    return y * lax.rsqrt(jnp.mean(jnp.square(y), -1, keepdims=True) + RMS_EPS) * g


def _rope_tile(t, cos, sin, d):
    if d == LANES:
        swapped = pltpu.roll(t, LANES // 2, 1)
    else:
        lane = lax.broadcasted_iota(jnp.int32, t.shape, 1)
        swapped = jnp.where((lane % d) < d // 2, pltpu.roll(t, LANES - d // 2, 1), pltpu.roll(t, d // 2, 1))
    return t * cos + swapped * sin


def _rope_tables(t, pos0, d, rows):
    half = d // 2
    inv = ROPE_THETA ** (-jnp.arange(half, dtype=F32) * (2.0 / d))
    ang = (jnp.arange(t, dtype=F32) + pos0)[:, None] * inv[None, :]
    cos = jnp.cos(ang)
    sin = jnp.sin(ang)
    reps = (rows // t, LANES // d)
    return jnp.tile(jnp.concatenate([cos, cos], 1), reps), jnp.tile(jnp.concatenate([-sin, sin], 1), reps)


def _proj_ab_kernel(*refs, kv_period, keys_transposed):
    if keys_transposed:
        (x_ref, w_ref, wkbt_ref, wkat_ref, cos_ref, sin_ref, cost_ref, sint_ref,
         h_ref, kb_ref, vb_ref, ka_ref, va_ref, kbt_ref, kat_ref) = refs
    else:
        x_ref, w_ref, cos_ref, sin_ref, h_ref, kb_ref, vb_ref, ka_ref, va_ref = refs
    xb = x_ref[...].astype(BF16)
    cos = cos_ref[...]
    sin = sin_ref[...]
    keep_state = (pl.program_id(0) % kv_period) == kv_period - 1
    kv0 = AB_KV_COL0
    rope_tiles = set(range(A_HEADS))
    if keys_transposed:
        state_refs = [(kv0, vb_ref), (kv0 + B_W, va_ref)]
    else:
        state_refs = [(kv0, kb_ref), (kv0 + B_W, vb_ref), (kv0 + 2 * B_W, ka_ref), (kv0 + 2 * B_W + A_KV_W, va_ref)]
        rope_tiles |= {(kv0 + 2 * B_W) // LANES + u for u in range(A_KV_HEADS)}
    width = w_ref.shape[1]
    for c0 in range(0, width, COL_CHUNK):
        cw = min(COL_CHUNK, width - c0)
        acc = _dot(xb, w_ref[:, c0:c0 + cw])
        parts = []
        for u in range(cw // LANES):
            part = acc[:, u * LANES:(u + 1) * LANES]
            if c0 // LANES + u in rope_tiles:
                part = _rope_tile(part, cos, sin, HEAD_DIM)
            parts.append(part)
        acc = jnp.concatenate(parts, axis=1)
        h_ref[:, c0:c0 + cw] = acc.astype(h_ref.dtype)
        for s0, ref in state_refs:
            lo, hi = max(c0, s0), min(c0 + cw, s0 + ref.shape[1])
            if lo < hi:
                @pl.when(keep_state)
                def _(acc=acc, ref=ref, lo=lo, hi=hi, s0=s0, c0=c0):
                    ref[:, lo - s0:hi - s0] = acc[:, lo - c0:hi - c0]
    if keys_transposed:
        kb_t = _dot_t(wkbt_ref[...], xb)
        ka_t = _dot_t(wkat_ref[...], xb)
        cos_t = cost_ref[...]
        sin_t = sint_ref[...]
        half = HEAD_DIM // 2
        rotated = []
        for u in range(A_KV_HEADS):
            t = ka_t[u * HEAD_DIM:(u + 1) * HEAD_DIM]
            rotated.append(t * cos_t + jnp.concatenate([t[half:], t[:half]], axis=0) * sin_t)
        ka_t = jnp.concatenate(rotated, axis=0)
        for t_val, t_ref in ((kb_t, kbt_ref), (ka_t, kat_ref)):
            blk = t_ref.shape[3]
            for c in range(t_ref.shape[1]):
                t_ref[0, c] = t_val[:, c * blk:(c + 1) * blk].astype(t_ref.dtype)

        @pl.when(keep_state)
        def _():
            kb_ref[...] = kb_t.T
            ka_ref[...] = ka_t.T


def _proj_ab(xf, w, seq, pos0, w_kb_t=None, w_ka_t=None, kb_block=None, ka_block=None):
    n, d = xf.shape
    tm = _row_tile(n)
    assert seq % tm == 0 or tm % seq == 0
    kv_period = max(seq // tm, 1)
    assert min(seq, B_BAND_PAST) == min(seq, tm)
    cos, sin = _rope_tables(seq, pos0, HEAD_DIM, max(seq, tm))
    n_tab = cos.shape[0] // tm
    state_widths = (B_W, B_W, A_KV_W, A_KV_W)
    keys_transposed = w_kb_t is not None
    rows = lambda i: (i, 0)
    tab = lambda i: (i % n_tab, 0)
    in_specs = [pl.BlockSpec((tm, d), rows), _resident(w.shape)]
    args = [xf, w]
    out_specs = [pl.BlockSpec((tm, w.shape[1]), rows)] + [
        pl.BlockSpec((tm, sw), lambda i: (i // kv_period, 0)) for sw in state_widths]
    out_shape = [jax.ShapeDtypeStruct((n, w.shape[1]), BF16)] + [
        jax.ShapeDtypeStruct((n // kv_period, sw), F32) for sw in state_widths]
    if keys_transposed:
        in_specs += [_resident(w_kb_t.shape), _resident(w_ka_t.shape)]
        args += [w_kb_t, w_ka_t]
    in_specs += [pl.BlockSpec((tm, LANES), tab), pl.BlockSpec((tm, LANES), tab)]
    args += [cos, sin]
    if keys_transposed:
        assert seq % tm == 0 and tm % kb_block == 0 and tm % ka_block == 0
        tiles = seq // tm
        tab_t = lambda i: (0, i % n_tab)
        in_specs += [pl.BlockSpec((LANES, tm), tab_t), pl.BlockSpec((LANES, tm), tab_t)]
        args += [cos.T, sin.T]
        for width, blk in ((B_W, kb_block), (A_KV_W, ka_block)):
            out_specs.append(pl.BlockSpec((1, tm // blk, width, blk), lambda i: (i // tiles, i % tiles, 0, 0)))
            out_shape.append(jax.ShapeDtypeStruct((n // seq, seq // blk, width, blk), BF16))
    return pl.pallas_call(
        functools.partial(_proj_ab_kernel, kv_period=kv_period, keys_transposed=keys_transposed),
        grid=(n // tm,),
        in_specs=in_specs,
        out_specs=out_specs,
        out_shape=out_shape,
        compiler_params=_params(1),
        name="proj_ab",
    )(*args)


def _rel_bias_kernel(tab_ref, o_ref, *, band):
    h = pl.program_id(0)
    shape = o_ref.shape[1:]
    r = lax.broadcasted_iota(jnp.int32, shape, 0)
    w = lax.broadcasted_iota(jnp.int32, shape, 1)
    idx = jnp.clip(band + r - w, -REL_CLIP, REL_CLIP) + REL_CLIP

    def body(d, acc):
        return jnp.where(idx == d, tab_ref[h, d], acc)

    o_ref[0] = lax.fori_loop(0, 2 * REL_CLIP + 1, body, jnp.zeros(shape, F32))


def _rel_bias(table, band, tq):
    heads = table.shape[0]
    return pl.pallas_call(
        functools.partial(_rel_bias_kernel, band=band),
        grid=(heads,),
        in_specs=[pl.BlockSpec(memory_space=pltpu.SMEM)],
        out_specs=pl.BlockSpec((1, tq, band + tq), lambda h: (h, 0, 0)),
        out_shape=jax.ShapeDtypeStruct((heads, tq, band + tq), F32),
        compiler_params=_params(1),
        name="rel_bias",
    )(table)


def _band_attn_kernel(*refs, heads, kv_heads, band, tq, past, has_bias, has_sinks):
    q_ref, k_ref, v_ref = refs[:3]
    rest = list(refs[3:])
    bias_ref = rest.pop(0) if has_bias else None
    sink_ref = rest.pop(0) if has_sinks else None
    o_ref = rest.pop(0)

    width = band + tq
    scale = HEAD_DIM ** -0.5
    ws = past + pl.program_id(1) * tq - band
    r = lax.broadcasted_iota(jnp.int32, (tq, width), 0) // CHUNK
    w = lax.broadcasted_iota(jnp.int32, (tq, width), 1)
    wc = w // CHUNK
    allowed = (wc >= r) & (wc <= r + band // CHUNK) & (w + ws >= 0)

    def window(ref, cols):
        if past >= band:
            return ref[0, pl.ds(pl.multiple_of(ws, CHUNK), width), cols]
        pieces = [ref[0, pl.ds(pl.multiple_of(jnp.maximum(ws + c * LANES, 0), LANES), LANES), cols]
                  for c in range(width // LANES)]
        return jnp.concatenate(pieces, axis=0)

    group = heads // kv_heads
    outs = []
    for kh in range(kv_heads):
        cols = slice(kh * HEAD_DIM, (kh + 1) * HEAD_DIM)
        k_w = window(k_ref, cols)
        v_w = window(v_ref, cols)
        for g in range(group):
            h = kh * group + g
            hcols = slice(h * HEAD_DIM, (h + 1) * HEAD_DIM)
            s = _dot_t(q_ref[0, :, hcols], k_w) * scale
            if has_bias:
                s = s + bias_ref[h]
            s = jnp.where(allowed, s, NEG_INF)
            m = jnp.max(s, -1, keepdims=True)
            if has_sinks:
                sink = sink_ref[h]
                m = jnp.maximum(m, sink)
            e = jnp.exp(s - m)
            den = jnp.sum(e, -1, keepdims=True)
            if has_sinks:
                den = den + jnp.exp(sink - m)
            p = (e * (1.0 / den)).astype(BF16)
            outs.append(_dot(p, v_w).astype(o_ref.dtype))
    o_ref[0] = jnp.concatenate(outs, axis=1)


def _band_attn(q, k, v, *, heads, kv_heads, band, tq, past, bias=None, sinks=None):
    (qa, qw, qi), (ka, kw, ki), (va, vw, vi) = q, k, v
    b, t, _ = qa.shape
    s_len = ka.shape[1]
    assert t % tq == 0 and s_len == past + t and qw == heads * HEAD_DIM and kw == kv_heads * HEAD_DIM
    assert past >= band or (past == 0 and tq % LANES == 0 and band % LANES == 0)
    in_specs = [
        pl.BlockSpec((1, tq, qw), lambda bi, i: (bi, i, qi)),
        pl.BlockSpec((1, s_len, kw), lambda bi, i: (bi, 0, ki)),
        pl.BlockSpec((1, s_len, vw), lambda bi, i: (bi, 0, vi)),
    ]
    args = [qa, ka, va]
    if bias is not None:
        in_specs.append(_resident(bias.shape))
        args.append(bias)
    if sinks is not None:
        in_specs.append(pl.BlockSpec(memory_space=pltpu.SMEM))
        args.append(sinks)
    return pl.pallas_call(
        functools.partial(_band_attn_kernel, heads=heads, kv_heads=kv_heads, band=band, tq=tq, past=past,
                          has_bias=bias is not None, has_sinks=sinks is not None),
        grid=(b, t // tq),
        in_specs=in_specs,
        out_specs=pl.BlockSpec((1, tq, qw), lambda bi, i: (bi, i, 0)),
        out_shape=jax.ShapeDtypeStruct((b, t, qw), BF16),
        compiler_params=_params(2),
        name="band_attn",
    )(*args)


def _rel_bias_folded_kernel(tab_ref, o_ref, *, band):
    h = pl.program_id(0)
    tq, width = o_ref.shape[1:]
    period = tq + width
    j = lax.broadcasted_iota(jnp.int32, (8, period), 1)
    dist = jnp.where(j < width, band - j, band - (j - period))
    idx = jnp.clip(dist, -REL_CLIP, REL_CLIP) + REL_CLIP

    def body(d, acc):
        return jnp.where(idx == d, tab_ref[h, d], acc)

    g = lax.fori_loop(0, 2 * REL_CLIP + 1, body, jnp.zeros((8, period), F32))
    full = jnp.concatenate([g] * (tq // 8), axis=0)
    bias = pltpu.roll(full, 0, 1, stride=1, stride_axis=0)[:, :width]
    rc = lax.broadcasted_iota(jnp.int32, (tq, width), 0) // CHUNK
    wc = lax.broadcasted_iota(jnp.int32, (tq, width), 1) // CHUNK
    o_ref[0] = jnp.where((wc >= rc) & (wc <= rc + band // CHUNK), bias * LOG2_E, NEG_INF)


def _rel_bias_folded(table, band, tq):
    heads = table.shape[0]
    assert (band + 2 * tq) % LANES == 0 and tq % 8 == 0
    return pl.pallas_call(
        functools.partial(_rel_bias_folded_kernel, band=band),
        grid=(heads,),
        in_specs=[pl.BlockSpec(memory_space=pltpu.SMEM)],
        out_specs=pl.BlockSpec((1, tq, band + tq), lambda h: (h, 0, 0)),
        out_shape=jax.ShapeDtypeStruct((heads, tq, band + tq), F32),
        compiler_params=_params(1),
        name="rel_bias_folded",
    )(table)


def _band_attn_t_kernel(*refs, heads, kv_heads, band, has_bias, has_sinks):
    q_ref, kt_ref, v_ref = refs[:3]
    rest = list(refs[3:])
    bias_ref = rest.pop(0) if has_bias else None
    sink_ref = rest.pop(0) if has_sinks else None
    o_ref = rest.pop(0)
    tq = kt_ref.shape[3]
    n_past = band // tq
    width = band + tq
    group = heads // kv_heads
    i = pl.program_id(1)
    ws = i * tq - band
    allowed = (lax.broadcasted_iota(jnp.int32, (1, width), 1) + ws) >= 0
    if not has_bias:
        rc = (lax.broadcasted_iota(jnp.int32, (group * tq, width), 0) % tq) // CHUNK
        wc = lax.broadcasted_iota(jnp.int32, (group * tq, width), 1) // CHUNK
        allowed = allowed & (wc >= rc) & (wc <= rc + band // CHUNK)
    ones = jnp.ones((width, HEAD_DIM), BF16)
    outs = []
    for kh in range(kv_heads):
        kc = slice(kh * HEAD_DIM, (kh + 1) * HEAD_DIM)
        hs = range(kh * group, (kh + 1) * group)
        q_st = jnp.concatenate([q_ref[0, :, h * HEAD_DIM:(h + 1) * HEAD_DIM] for h in hs], axis=0)
        k_t = jnp.concatenate([kt_ref[0, jnp.maximum(i - n_past + c, 0), kc, :] for c in range(n_past + 1)], axis=1)
        s = _dot(q_st, k_t) * (HEAD_DIM ** -0.5 * LOG2_E)
        if has_bias:
            s = s + jnp.concatenate([bias_ref[h] for h in hs], axis=0)
        s = jnp.where(allowed, s, NEG_INF)
        ms, ps = [], []
        for g, h in enumerate(hs):
            s_g = s[g * tq:(g + 1) * tq]
            m_g = jnp.max(s_g, -1, keepdims=True)
            if has_sinks:
                m_g = jnp.maximum(m_g, sink_ref[h] * LOG2_E)
            ms.append(m_g)
            ps.append(jnp.exp2(s_g - m_g).astype(BF16))
        v_w = jnp.concatenate([v_ref[0, pl.ds(pl.multiple_of(jnp.maximum(ws + c * tq, 0), tq), tq), kc]
                               for c in range(n_past + 1)], axis=0)
        o_ext = _dot(jnp.concatenate(ps, axis=0), jnp.concatenate([v_w, ones], axis=1))
        for g, h in enumerate(hs):
            o_g = o_ext[g * tq:(g + 1) * tq]
            den = o_g[:, HEAD_DIM:]
            if has_sinks:
                den = den + jnp.exp2(sink_ref[h] * LOG2_E - ms[g])
            outs.append((o_g[:, :HEAD_DIM] * (1.0 / den)).astype(o_ref.dtype))
    o_ref[0] = jnp.concatenate(outs, axis=1)


def _band_attn_t(q, kt, v, *, heads, kv_heads, band, bias=None, sinks=None):
    (qa, qw, qi), (va, vw, vi) = q, v
    b, t, _ = qa.shape
    _, nkb, _, tq = kt.shape
    assert nkb * tq == t and band % tq == 0 and qw == heads * HEAD_DIM and vw == kv_heads * HEAD_DIM
    assert bias is None or heads == kv_heads
    in_specs = [
        pl.BlockSpec((1, tq, qw), lambda bi, i: (bi, i, qi)),
        pl.BlockSpec((1, nkb, kv_heads * HEAD_DIM, tq), lambda bi, i: (bi, 0, 0, 0)),
        pl.BlockSpec((1, t, vw), lambda bi, i: (bi, 0, vi)),
    ]
    args = [qa, kt, va]
    if bias is not None:
        in_specs.append(_resident(bias.shape))
        args.append(bias)
    if sinks is not None:
        in_specs.append(pl.BlockSpec(memory_space=pltpu.SMEM))
        args.append(sinks)
    return pl.pallas_call(
        functools.partial(_band_attn_t_kernel, heads=heads, kv_heads=kv_heads, band=band,
                          has_bias=bias is not None, has_sinks=sinks is not None),
        grid=(b, t // tq),
        in_specs=in_specs,
        out_specs=pl.BlockSpec((1, tq, qw), lambda bi, i: (bi, i, 0)),
        out_shape=jax.ShapeDtypeStruct((b, t, qw), BF16),
        compiler_params=_params(2),
        name="band_attn_t",
    )(*args)


def _outproj_ln_kernel(*refs, n_in):
    a_refs = refs[:n_in]
    w_ref, x_ref, g_ref, b_ref, o_ref = refs[n_in:]
    d_out = o_ref.shape[1]
    for j in range(d_out // COL_CHUNK):
        cols = slice(j * COL_CHUNK, (j + 1) * COL_CHUNK)
        y = DEEPNORM_ALPHA * x_ref[:, cols]
        r0 = 0
        for a_ref in a_refs:
            kk = a_ref.shape[1]
            y = y + _dot(a_ref[...], w_ref[r0:r0 + kk, cols])
            r0 += kk
        o_ref[:, cols] = y
    o_ref[...] = _layer_norm(o_ref[...], g_ref[...], b_ref[...])


def _outproj_ln(a_list, w, xf, g, b):
    n, d = xf.shape
    tm = _row_tile(n)
    assert sum(a.shape[1] for a in a_list) == w.shape[0]
    return pl.pallas_call(
        functools.partial(_outproj_ln_kernel, n_in=len(a_list)),
        grid=(n // tm,),
        in_specs=[pl.BlockSpec((tm, a.shape[1]), lambda i: (i, 0)) for a in a_list] + [
            _resident(w.shape),
            pl.BlockSpec((tm, d), lambda i: (i, 0)),
            _resident(g.shape),
            _resident(b.shape),
        ],
        out_specs=pl.BlockSpec((tm, d), lambda i: (i, 0)),
        out_shape=jax.ShapeDtypeStruct((n, d), F32),
        compiler_params=_params(1),
        name="outproj_ln",
    )(*a_list, w, xf, g, b)


def _mlp_ln_kernel(x_ref, wu_ref, wd_ref, g_ref, b_ref, o_ref, xb_ref):
    f = pl.program_id(1)

    @pl.when(f == 0)
    def _():
        xb_ref[...] = x_ref[...].astype(BF16)
        o_ref[...] = jnp.zeros(o_ref.shape, o_ref.dtype)

    hid = _dot(xb_ref[...], wu_ref[...])
    hid = jnp.square(jnp.maximum(hid, 0.0)).astype(BF16)
    for j in range(o_ref.shape[1] // COL_CHUNK):
        cols = slice(j * COL_CHUNK, (j + 1) * COL_CHUNK)
        o_ref[:, cols] += _dot(hid, wd_ref[:, cols])

    @pl.when(f == pl.num_programs(1) - 1)
    def _():
        o_ref[...] = _layer_norm(DEEPNORM_ALPHA * x_ref[...] + o_ref[...], g_ref[...], b_ref[...])


def _mlp_ln(xf, layer, w_up, w_down, g, b):
    n, d = xf.shape
    d_ff = w_up.shape[2]
    tm = MLP_ROW_TILE if n % MLP_ROW_TILE == 0 else n
    tf = MLP_FF_TILE
    return pl.pallas_call(
        _mlp_ln_kernel,
        grid=(n // tm, d_ff // tf),
        in_specs=[
            pl.BlockSpec((tm, d), lambda i, f: (i, 0)),
            pl.BlockSpec((None, d, tf), lambda i, f: (layer, 0, f)),
            pl.BlockSpec((None, tf, d), lambda i, f: (layer, f, 0)),
            _resident(g.shape),
            _resident(b.shape),
        ],
        out_specs=pl.BlockSpec((tm, d), lambda i, f: (i, 0)),
        out_shape=jax.ShapeDtypeStruct((n, d), F32),
        scratch_shapes=[pltpu.VMEM((tm, d), BF16)],
        compiler_params=_params(2),
        name="mlp_ln",
    )(xf, w_up, w_down, g, b)


def _ple_kernel(x_ref, p_ref, wg_ref, bg_ref, wp_ref, o_ref):
    xb = x_ref[...].astype(BF16)
    pb = p_ref[...].astype(BF16)
    for j in range(o_ref.shape[1] // COL_CHUNK):
        cols = slice(j * COL_CHUNK, (j + 1) * COL_CHUNK)
        gate = jax.nn.sigmoid(_dot(xb, wg_ref[:, cols]) + bg_ref[:, cols])
        o_ref[:, cols] = x_ref[:, cols] + gate * _dot(pb, wp_ref[:, cols])


def _ple(xf, p_all, layer, wg, bg, wp):
    n, d = xf.shape
    tm = _row_tile(n)
    return pl.pallas_call(
        _ple_kernel,
        grid=(n // tm,),
        in_specs=[
            pl.BlockSpec((tm, d), lambda i: (i, 0)),
            pl.BlockSpec((None, tm, p_all.shape[2]), lambda i: (layer, i, 0)),
            _resident_slab(wg.shape, layer),
            _resident(bg.shape),
            _resident_slab(wp.shape, layer),
        ],
        out_specs=pl.BlockSpec((tm, d), lambda i: (i, 0)),
        out_shape=jax.ShapeDtypeStruct((n, d), F32),
        compiler_params=_params(1),
        name="ple",
    )(xf, p_all, wg, bg, wp)


Q_GROUP = 4


def _proj_cq_kernel(x_ref, w_ref, gq_ref, gkv_ref, cos_ref, sin_ref, wq_ref, q_ref, ckv_ref, kr_ref, krp_ref):
    cos = cos_ref[...]
    sin = sin_ref[...]
    h = _dot(x_ref[...].astype(BF16), w_ref[...])
    cq = _rms_norm(h[:, :C_Q_RANK], gq_ref[...]).astype(BF16)
    ckv_ref[...] = _rms_norm(h[:, C_Q_RANK:C_Q_RANK + C_KV_RANK], gkv_ref[...])
    rot = _rope_tile(h[:, C_Q_RANK + C_KV_RANK:], cos, sin, C_ROPE)
    lane = lax.broadcasted_iota(jnp.int32, rot.shape, 1)
    rot = jnp.where(lane < C_ROPE, rot, 0.0)
    kr_ref[...] = rot[:, :C_ROPE]
    krp_ref[...] = rot.astype(krp_ref.dtype)

    grp_w = Q_GROUP * (C_NOPE + C_ROPE)
    for gi in range(C_HEADS // Q_GROUP):
        acc = _dot(cq, wq_ref[:, gi * grp_w:(gi + 1) * grp_w]) * (MLA_SCALE * LOG2_E)
        for u in range(Q_GROUP):
            o0 = (gi * Q_GROUP + u) * C_QK
            q_ref[:, o0:o0 + C_NOPE] = acc[:, u * C_NOPE:(u + 1) * C_NOPE].astype(q_ref.dtype)
            if u % 2 == 0:
                r0 = Q_GROUP * C_NOPE + (u // 2) * LANES
                qrot = _rope_tile(acc[:, r0:r0 + LANES], cos, sin, C_ROPE)
                piece = qrot
            else:
                piece = pltpu.roll(qrot, C_ROPE, 1)
            q_ref[:, o0 + C_NOPE:o0 + C_QK] = jnp.where(lane < C_ROPE, piece, 0.0).astype(q_ref.dtype)


def _proj_cq(xf, w, gq, gkv, cos, sin, wq):
    n, d = xf.shape
    tm = _row_tile(n)
    n_tab = cos.shape[0] // tm
    rows = lambda i: (i, 0)
    tab = lambda i: (i % n_tab, 0)
    return pl.pallas_call(
        _proj_cq_kernel,
        grid=(n // tm,),
        in_specs=[
            pl.BlockSpec((tm, d), rows),
            _resident(w.shape),
            _resident(gq.shape),
            _resident(gkv.shape),
            pl.BlockSpec((tm, LANES), tab),
            pl.BlockSpec((tm, LANES), tab),
            _resident(wq.shape),
        ],
        out_specs=[
            pl.BlockSpec((tm, C_HEADS * C_QK), rows),
            pl.BlockSpec((tm, C_KV_RANK), rows),
            pl.BlockSpec((tm, C_ROPE), rows),
            pl.BlockSpec((tm, LANES), rows),
        ],
        out_shape=[
            jax.ShapeDtypeStruct((n, C_HEADS * C_QK), BF16),
            jax.ShapeDtypeStruct((n, C_KV_RANK), F32),
            jax.ShapeDtypeStruct((n, C_ROPE), F32),
            jax.ShapeDtypeStruct((n, LANES), BF16),
        ],
        compiler_params=_params(1),
        name="proj_cq",
    )(xf, w, gq, gkv, cos, sin, wq)


def _kv_c_kernel(ckv_ref, krp_ref, wkt_ref, wv_ref, kt_ref, krt_ref, v_ref):
    cb = ckv_ref[...].astype(BF16)
    eye = (lax.broadcasted_iota(jnp.int32, (LANES, LANES), 0)
           == lax.broadcasted_iota(jnp.int32, (LANES, LANES), 1)).astype(F32).astype(BF16)
    krt_ref[0, 0] = _dot_t(eye, krp_ref[...]).astype(krt_ref.dtype)
    kt_ref[0, 0] = _dot_t(wkt_ref[...], cb).astype(kt_ref.dtype)
    for j in range(C_HEADS * C_V // COL_CHUNK):
        cols = slice(j * COL_CHUNK, (j + 1) * COL_CHUNK)
        v_ref[:, cols] = _dot(cb, wv_ref[:, cols]).astype(v_ref.dtype)


def _kv_c(ckv, krp, wkt, wv, batch, tk):
    n = ckv.shape[0]
    nkb = n // batch // tk
    assert n == batch * nkb * tk
    rows = lambda i: (i, 0)
    blk = lambda i: (i // nkb, i % nkb, 0, 0)
    return pl.pallas_call(
        _kv_c_kernel,
        grid=(n // tk,),
        in_specs=[pl.BlockSpec((tk, C_KV_RANK), rows), pl.BlockSpec((tk, LANES), rows), _resident(wkt.shape),
                  _resident(wv.shape)],
        out_specs=[pl.BlockSpec((1, 1, C_HEADS * C_NOPE, tk), blk), pl.BlockSpec((1, 1, LANES, tk), blk),
                   pl.BlockSpec((tk, C_HEADS * C_V), rows)],
        out_shape=[
            jax.ShapeDtypeStruct((batch, nkb, C_HEADS * C_NOPE, tk), BF16),
            jax.ShapeDtypeStruct((batch, nkb, LANES, tk), BF16),
            jax.ShapeDtypeStruct((n, C_HEADS * C_V), BF16),
        ],
        compiler_params=_params(1),
        name="kv_c",
    )(ckv, krp, wkt, wv)


MLA_LONG_BLOCK = 512


def _mla_attn_kernel(q_ref, kt_ref, krt_ref, v_ref, o_ref, *, pos0, heads):
    t = q_ref.shape[1]
    tq = tk = kt_ref.shape[3]
    row = lax.broadcasted_iota(jnp.int32, (tq, tk), 0) // CHUNK
    col = lax.broadcasted_iota(jnp.int32, (tq, tk), 1) // CHUNK
    diag_ok = col <= row
    ones = jnp.ones((tk, C_V), BF16)

    def k_slab(g, kb):
        return jnp.concatenate([kt_ref[0, kb, g * C_NOPE:(g + 1) * C_NOPE, :], krt_ref[0, kb]], axis=0)

    def v_slab(g, kb):
        return jnp.concatenate([v_ref[0, kb * tk:(kb + 1) * tk, g * C_V:(g + 1) * C_V], ones], axis=1)

    def step(s, m, acc, v_ext):
        m_new = jnp.maximum(m, jnp.max(s, -1, keepdims=True))
        p = jnp.exp2(s - m_new).astype(BF16)
        return m_new, jnp.exp2(m - m_new) * acc + _dot(p, v_ext)

    def finish(acc):
        return (acc[:, :C_V] * (1.0 / acc[:, C_V:])).astype(o_ref.dtype)

    for qi in range(t // tq):
        q0 = qi * tq
        qs = [q_ref[0, q0:q0 + tq, g * C_QK:(g + 1) * C_QK] for g in range(heads)]
        n_full = (pos0 + q0) // tk
        state = [(jnp.full((tq, 1), NEG_INF, F32), jnp.zeros((tq, 2 * C_V), F32)) for _ in range(heads)]
        for kb in range(n_full):
            raw = [_dot(qs[g], k_slab(g, kb)) for g in range(heads)]
            state = [step(raw[g], *state[g], v_slab(g, kb)) for g in range(heads)]
        raw = [jnp.where(diag_ok, _dot(qs[g], k_slab(g, n_full)), NEG_INF) for g in range(heads)]
        state = [step(raw[g], *state[g], v_slab(g, n_full)) for g in range(heads)]
        for g in range(heads):
            o_ref[0, q0:q0 + tq, g * C_V:(g + 1) * C_V] = finish(state[g][1])


def _mla_attn(q, kt, krt, v, *, pos0, heads=2):
    b, t, _ = q.shape
    _, nkb, _, tk = kt.shape
    s_len = nkb * tk
    assert t % tk == 0 and pos0 % tk == 0 and pos0 + t <= s_len and v.shape[1] == s_len
    return pl.pallas_call(
        functools.partial(_mla_attn_kernel, pos0=pos0, heads=heads),
        grid=(b, C_HEADS // heads),
        in_specs=[
            pl.BlockSpec((1, t, heads * C_QK), lambda bi, h: (bi, 0, h)),
            pl.BlockSpec((1, nkb, heads * C_NOPE, tk), lambda bi, h: (bi, 0, h, 0)),
            pl.BlockSpec((1, nkb, LANES, tk), lambda bi, h: (bi, 0, 0, 0)),
            pl.BlockSpec((1, s_len, heads * C_V), lambda bi, h: (bi, 0, h)),
        ],
        out_specs=pl.BlockSpec((1, t, heads * C_V), lambda bi, h: (bi, 0, h)),
        out_shape=jax.ShapeDtypeStruct((b, t, C_HEADS * C_V), BF16),
        compiler_params=_params(2),
        name="mla_attn",
    )(q, kt, krt, v)


def _mla_absorbed_kernel(q_ref, ckv_new_ref, kr_new_ref, ckv_past_ref, kr_past_ref, wkt_ref, wv_ref, o_ref):
    t = q_ref.shape[1]
    q_lat, q_rope = [], []
    for h in range(C_HEADS):
        q_lat.append(_dot(q_ref[0, :, h * C_QK:h * C_QK + C_NOPE], wkt_ref[h * C_NOPE:(h + 1) * C_NOPE, :]))
        q_rope.append(q_ref[0, :, h * C_QK + C_NOPE:h * C_QK + C_NOPE + C_ROPE])
    q_lat = jnp.concatenate(q_lat, axis=0).astype(BF16)
    q_rope = jnp.concatenate(q_rope, axis=0)
    lat = [ckv_past_ref[...].astype(BF16), ckv_new_ref[0].astype(BF16)]
    rot = [kr_past_ref[...].astype(BF16), kr_new_ref[0].astype(BF16)]
    s = jnp.concatenate([_dot_t(q_lat, c) + _dot_t(q_rope, r) for c, r in zip(lat, rot)], axis=1)
    e = jnp.exp2(s - jnp.max(s, -1, keepdims=True))
    p = (e * (1.0 / jnp.sum(e, -1, keepdims=True))).astype(BF16)
    n_past = lat[0].shape[0]
    o_lat = (_dot(p[:, :n_past], lat[0]) + _dot(p[:, n_past:], lat[1])).astype(BF16)
    o_ref[0] = jnp.concatenate([_dot(o_lat[h * t:(h + 1) * t], wv_ref[:, h * C_V:(h + 1) * C_V])
                                for h in range(C_HEADS)], axis=1).astype(o_ref.dtype)


def _mla_absorbed(q, ckv_new, kr_new, ckv_past, kr_past, wkt, wv, *, pos0):
    b, t, _ = q.shape
    n_past = ckv_past.shape[2]
    assert t == CHUNK and pos0 % CHUNK == 0 and n_past <= pos0
    return pl.pallas_call(
        _mla_absorbed_kernel,
        grid=(b,),
        in_specs=[
            pl.BlockSpec((1, t, C_HEADS * C_QK), lambda bi: (bi, 0, 0)),
            pl.BlockSpec((1, t, C_KV_RANK), lambda bi: (bi, 0, 0)),
            pl.BlockSpec((1, t, C_ROPE), lambda bi: (bi, 0, 0)),
            pl.BlockSpec((None, None, n_past, C_KV_RANK), lambda bi: (0, bi, 0, 0)),
            pl.BlockSpec((None, None, n_past, C_ROPE), lambda bi: (0, bi, 0, 0)),
            _resident(wkt.shape),
            _resident(wv.shape),
        ],
        out_specs=pl.BlockSpec((1, t, C_HEADS * C_V), lambda bi: (bi, 0, 0)),
        out_shape=jax.ShapeDtypeStruct((b, t, C_HEADS * C_V), BF16),
        compiler_params=_params(1),
        name="mla_absorbed",
    )(q, ckv_new, kr_new, ckv_past, kr_past, wkt, wv)


def _prepare_weights(w):
    o1 = A_Q_W
    o2 = o1 + A_KV_W
    o3 = o2 + A_KV_W
    o4 = o3 + B_W
    o5 = o4 + B_W
    w_ab = w['w_in_ab'][0]
    w_ab = jnp.concatenate([w_ab[:, :o1], w_ab[:, o3:o4], w_ab[:, o4:o5], w_ab[:, o5:], w_ab[:, o1:o2],
                            w_ab[:, o2:o3]], axis=1)
    ka0 = AB_KV_COL0 + 2 * B_W
    w_ab_long = jnp.concatenate([w_ab[:, :AB_KV_COL0], w_ab[:, AB_KV_COL0 + B_W:ka0], w_ab[:, ka0 + A_KV_W:]], axis=1)
    w_kb_t = w_ab[:, AB_KV_COL0:AB_KV_COL0 + B_W].T
    w_ka_t = w_ab[:, ka0:ka0 + A_KV_W].T
    w_c = jnp.pad(w['w_in_c'][0], ((0, 0), (0, LANES - C_ROPE)))
    hq = C_NOPE + C_ROPE
    q_cols = []
    for g0 in range(0, C_HEADS, Q_GROUP):
        q_cols += [jnp.arange(h * hq, h * hq + C_NOPE) for h in range(g0, g0 + Q_GROUP)]
        q_cols += [jnp.arange(h * hq + C_NOPE, (h + 1) * hq) for h in range(g0, g0 + Q_GROUP)]
    w_q = w['w_q_b_c'][0][:, jnp.concatenate(q_cols)]
    hkv = C_NOPE + C_V
    k_cols = jnp.concatenate([jnp.arange(h * hkv, h * hkv + C_NOPE) for h in range(C_HEADS)])
    v_cols = jnp.concatenate([jnp.arange(h * hkv + C_NOPE, (h + 1) * hkv) for h in range(C_HEADS)])
    w_k_t = w['w_kv_b_c'][0][:, k_cols].T
    w_v = w['w_kv_b_c'][0][:, v_cols]
    row = lambda a: a.reshape(1, -1)
    return {
        'w_in_ab': w_ab.astype(BF16), 'w_in_ab_long': w_ab_long.astype(BF16), 'w_kb_t': w_kb_t.astype(BF16),
        'w_ka_t': w_ka_t.astype(BF16),
        'w_out_ab': w['w_out_ab'][0].astype(BF16),
        'w_in_c': w_c.astype(BF16), 'w_q_b_c': w_q.astype(BF16), 'w_k_t_c': w_k_t.astype(BF16),
        'w_v_c': w_v.astype(BF16),
        'w_out_c': w['w_out_c'][0].astype(BF16),
        'g_q_c': row(w['g_q_c'][0]), 'g_kv_c': row(w['g_kv_c'][0]),
        'sinks_a': w['sinks_a'][0], 'rel_bias_b': w['rel_bias_b'][0],
        'ln1_g': [row(w['ln1_g'][i]) for i in range(DEPTH)], 'ln1_b': [row(w['ln1_b'][i]) for i in range(DEPTH)],
        'ln2_g': [row(w['ln2_g'][i]) for i in range(DEPTH)], 'ln2_b': [row(w['ln2_b'][i]) for i in range(DEPTH)],
        'w_mlp_up': w['w_mlp_up'].astype(BF16), 'w_mlp_down': w['w_mlp_down'].astype(BF16),
        'w_ple_gate': w['w_ple_gate'].astype(BF16), 'w_ple': w['w_ple'].astype(BF16),
        'b_ple_gate': [row(w['b_ple_gate'][i]) for i in range(DEPTH)],
    }


def _channel_mix(xf, p_all, pw, i):
    xf = _mlp_ln(xf, i, pw['w_mlp_up'], pw['w_mlp_down'], pw['ln2_g'][i], pw['ln2_b'][i])
    return _ple(xf, p_all, i, pw['w_ple_gate'], pw['b_ple_gate'][i], pw['w_ple'])


def _trunk(x, p, pos0, past, pw):
    b, t, d = x.shape
    n = b * t
    tm = _row_tile(n)
    xf = x.reshape(n, d)

    if past is None:
        h, kb_new, vb_new, ka_new, va_new, kb_t, ka_t = _proj_ab(
            xf, pw['w_in_ab_long'], t, pos0, pw['w_kb_t'], pw['w_ka_t'], BAND_B_BLOCK, BAND_A_BLOCK)
    else:
        h, kb_new, vb_new, ka_new, va_new = _proj_ab(xf, pw['w_in_ab'], t, pos0)
    h3 = h.reshape(b, t, h.shape[1])
    kb_new, vb_new, ka_new, va_new = [a.reshape(b, a.shape[0] // b, a.shape[1])
                                      for a in (kb_new, vb_new, ka_new, va_new)]
    q_a = (h3, A_Q_W, 0)
    q_b = (h3, B_W, 1)
    if past is None:
        attn_a = _band_attn_t(q_a, ka_t, (h3, A_KV_W, (AB_KV_COL0 + B_W) // A_KV_W), heads=A_HEADS,
                              kv_heads=A_KV_HEADS, band=WINDOW, sinks=pw['sinks_a'])
        bias_b = _rel_bias_folded(pw['rel_bias_b'], B_BAND_PAST, BAND_B_BLOCK)
        attn_b = _band_attn_t(q_b, kb_t, (h3, B_W, 2), heads=B_HEADS, kv_heads=B_HEADS, band=B_BAND_PAST,
                              bias=bias_b)
        ak, av, bk, bv = ka_new[:, -WINDOW:], va_new[:, -WINDOW:], kb_new, vb_new
    else:
        n_past_a, n_past_b = past[0].shape[1], past[2].shape[1]
        full = [jnp.concatenate([c.reshape(b, c.shape[1], -1), new], axis=1)
                for c, new in zip(past[:4], (ka_new, va_new, kb_new, vb_new))]
        ak, av = full[0][:, -WINDOW:], full[1][:, -WINDOW:]
        bk, bv = full[2][:, -B_BAND_PAST:], full[3][:, -B_BAND_PAST:]
        k_a, v_a, k_b, v_b = [(f.astype(BF16), f.shape[2], 0) for f in full]
        attn_a = _band_attn(q_a, k_a, v_a, heads=A_HEADS, kv_heads=A_KV_HEADS, band=WINDOW, tq=CHUNK,
                            past=n_past_a, sinks=pw['sinks_a'])
        bias_b = _rel_bias(pw['rel_bias_b'], B_BAND_PAST, CHUNK)
        attn_b = _band_attn(q_b, k_b, v_b, heads=B_HEADS, kv_heads=B_HEADS, band=B_BAND_PAST, tq=CHUNK,
                            past=n_past_b, bias=bias_b)
    xf = _outproj_ln([attn_a.reshape(n, A_Q_W), attn_b.reshape(n, B_W)], pw['w_out_ab'], xf,
                     pw['ln1_g'][0], pw['ln1_b'][0])
    p_all = p.reshape(p.shape[0], n, p.shape[3])
    xf = _channel_mix(xf, p_all, pw, 0)

    cos_c, sin_c = _rope_tables(t, pos0, C_ROPE, max(t, tm))
    q, ckv, kr, krp = _proj_cq(xf, pw['w_in_c'], pw['g_q_c'], pw['g_kv_c'], cos_c, sin_c, pw['w_q_b_c'])
    q = q.reshape(b, t, C_HEADS * C_QK)
    if past is None:
        kt_c, krt_c, v_c = _kv_c(ckv, krp, pw['w_k_t_c'], pw['w_v_c'], b, min(MLA_LONG_BLOCK, t))
        attn_c = _mla_attn(q, kt_c, krt_c, v_c.reshape(b, t, C_HEADS * C_V), pos0=pos0)
    else:
        attn_c = _mla_absorbed(q, ckv.reshape(b, t, C_KV_RANK), kr.reshape(b, t, C_ROPE), past[4], past[5],
                               pw['w_k_t_c'], pw['w_v_c'], pos0=pos0)
    xf = _outproj_ln([attn_c.reshape(n, C_HEADS * C_V)], pw['w_out_c'], xf, pw['ln1_g'][1], pw['ln1_b'][1])
    xf = _channel_mix(xf, p_all, pw, 1)

    heads4 = lambda a, hh: a.reshape(1, b, a.shape[1], hh, HEAD_DIM)
    return (xf.reshape(b, t, d), heads4(ak, A_KV_HEADS), heads4(av, A_KV_HEADS), heads4(bk, B_HEADS),
            heads4(bv, B_HEADS), ckv.reshape(1, b, t, C_KV_RANK), kr.reshape(1, b, t, C_ROPE))


def kernel(x_prompt, x_sample, cache_a_k, cache_a_v, cache_b_k, cache_b_v, cache_c_kv, cache_c_krope, p_prompt,
           p_sample, w_in_ab, sinks_a, rel_bias_b, w_out_ab, w_in_c, g_q_c, w_q_b_c, g_kv_c, w_kv_b_c, w_out_c,
           ln1_g, ln1_b, ln2_g, ln2_b, w_mlp_up, w_mlp_down, w_ple_gate, b_ple_gate, w_ple):
    pw = _prepare_weights({
        'w_in_ab': w_in_ab, 'sinks_a': sinks_a, 'rel_bias_b': rel_bias_b, 'w_out_ab': w_out_ab,
        'w_in_c': w_in_c, 'g_q_c': g_q_c, 'w_q_b_c': w_q_b_c, 'g_kv_c': g_kv_c, 'w_kv_b_c': w_kv_b_c,
        'w_out_c': w_out_c, 'ln1_g': ln1_g, 'ln1_b': ln1_b, 'ln2_g': ln2_g, 'ln2_b': ln2_b,
        'w_mlp_up': w_mlp_up, 'w_mlp_down': w_mlp_down, 'w_ple_gate': w_ple_gate, 'b_ple_gate': b_ple_gate,
        'w_ple': w_ple,
    })
    prompt = _trunk(x_prompt, p_prompt, 0, None, pw)
    past = (cache_a_k[0], cache_a_v[0], cache_b_k[0], cache_b_v[0], cache_c_kv, cache_c_krope)
    sample = _trunk(x_sample, p_sample, cache_c_kv.shape[2], past, pw)
    return (prompt[0], sample[0]) + prompt[1:] + sample[1:]
```

```python
import functools

import jax
import jax.numpy as jnp
from jax import lax
from jax.experimental import pallas as pl
from jax.experimental.pallas import tpu as pltpu

F32 = jnp.float32
BF16 = jnp.bfloat16

CHUNK = 64
HEAD_DIM = 128
A_HEADS = 8
A_KV_HEADS = 2
WINDOW = 128
B_HEADS = 8
B_BAND_PAST = 512
REL_CLIP = 128
C_HEADS = 16
C_Q_RANK = 768
C_KV_RANK = 512
C_NOPE = 128
C_ROPE = 64
C_V = 128
DEPTH = 2
ROPE_THETA = 10000.0
LN_EPS = 1e-5
RMS_EPS = 1e-6
NEG_INF = -1e30
DEEPNORM_ALPHA = (2 * DEPTH) ** 0.25
LOG2_E = 1.4426950408889634
MLA_SCALE = (C_NOPE + C_ROPE) ** -0.5

A_Q_W = A_HEADS * HEAD_DIM
A_KV_W = A_KV_HEADS * HEAD_DIM
B_W = B_HEADS * HEAD_DIM
AB_IN_W = A_Q_W + 2 * A_KV_W + 3 * B_W
AB_KV_COL0 = A_Q_W + B_W
AB_KV_W = AB_IN_W - AB_KV_COL0
C_IN_W = C_Q_RANK + C_KV_RANK + C_ROPE
C_QK = 256

LANES = 128
V7X_VMEM_BYTES = 64 * 1024 * 1024
VMEM_LIMIT = V7X_VMEM_BYTES - 8 * 1024 * 1024

ROW_TILE = 512
COL_CHUNK = 512
MLP_ROW_TILE = 512
MLP_FF_TILE = 1024
BAND_B_BLOCK = 256
BAND_A_BLOCK = 128


def _params(n_axes):
    return pltpu.CompilerParams(dimension_semantics=("arbitrary",) * n_axes, vmem_limit_bytes=VMEM_LIMIT)


def _resident(shape):
    nd = len(shape)
    return pl.BlockSpec(shape, lambda *_: (0,) * nd, pipeline_mode=pl.Buffered(1))


def _resident_slab(shape, index):
    nd = len(shape) - 1
    return pl.BlockSpec((None,) + tuple(shape[1:]), lambda *_: (index,) + (0,) * nd, pipeline_mode=pl.Buffered(1))


def _row_tile(n):
    return ROW_TILE if n % ROW_TILE == 0 else n


def _dot(a, b):
    return jnp.dot(a, b, preferred_element_type=F32)


def _dot_t(a, b):
    return lax.dot_general(a, b, (((1,), (1,)), ((), ())), preferred_element_type=F32)


def _layer_norm(y, g, b):
    mu = jnp.mean(y, -1, keepdims=True)
    var = jnp.mean(jnp.square(y - mu), -1, keepdims=True)
    return (y - mu) * lax.rsqrt(var + LN_EPS) * g + b


def _rms_norm(y, g):
    return y * lax.rsqrt(jnp.mean(jnp.square(y), -1, keepdims=True) + RMS_EPS) * g


def _rope_tile(t, cos, sin, d):
    if d == LANES:
        swapped = pltpu.roll(t, LANES // 2, 1)
    else:
        lane = lax.broadcasted_iota(jnp.int32, t.shape, 1)
        swapped = jnp.where((lane % d) < d // 2, pltpu.roll(t, LANES - d // 2, 1), pltpu.roll(t, d // 2, 1))
    return t * cos + swapped * sin


def _rope_tables(t, pos0, d, rows):
    half = d // 2
    inv = ROPE_THETA ** (-jnp.arange(half, dtype=F32) * (2.0 / d))
    ang = (jnp.arange(t, dtype=F32) + pos0)[:, None] * inv[None, :]
    cos = jnp.cos(ang)
    sin = jnp.sin(ang)
    reps = (rows // t, LANES // d)
    return jnp.tile(jnp.concatenate([cos, cos], 1), reps), jnp.tile(jnp.concatenate([-sin, sin], 1), reps)


def _proj_ab_kernel(*refs, kv_period, keys_transposed):
    if keys_transposed:
        (x_ref, w_ref, wkt_ref, cos_ref, sin_ref, cost_ref, sint_ref,
         h_ref, kb_ref, vb_ref, ka_ref, va_ref, kbt_ref, kat_ref) = refs
    else:
        x_ref, w_ref, cos_ref, sin_ref, h_ref, kb_ref, vb_ref, ka_ref, va_ref = refs
    xb = x_ref[...].astype(BF16)
    cos = cos_ref[...]
    sin = sin_ref[...]
    keep_state = (pl.program_id(0) % kv_period) == kv_period - 1
    kv0 = AB_KV_COL0
    rope_tiles = set(range(A_HEADS))
    if keys_transposed:
        state_refs = [(kv0, vb_ref), (kv0 + B_W, va_ref)]
    else:
        state_refs = [(kv0, kb_ref), (kv0 + B_W, vb_ref), (kv0 + 2 * B_W, ka_ref), (kv0 + 2 * B_W + A_KV_W, va_ref)]
        rope_tiles |= {(kv0 + 2 * B_W) // LANES + u for u in range(A_KV_HEADS)}
    width = w_ref.shape[1]
    for c0 in range(0, width, COL_CHUNK):
        cw = min(COL_CHUNK, width - c0)
        acc = _dot(xb, w_ref[:, c0:c0 + cw])
        parts = []
        for u in range(cw // LANES):
            part = acc[:, u * LANES:(u + 1) * LANES]
            if c0 // LANES + u in rope_tiles:
                part = _rope_tile(part, cos, sin, HEAD_DIM)
            parts.append(part)
        acc = jnp.concatenate(parts, axis=1)
        h_ref[:, c0:c0 + cw] = acc.astype(h_ref.dtype)
        for s0, ref in state_refs:
            lo, hi = max(c0, s0), min(c0 + cw, s0 + ref.shape[1])
            if lo < hi:
                @pl.when(keep_state)
                def _(acc=acc, ref=ref, lo=lo, hi=hi, s0=s0, c0=c0):
                    ref[:, lo - s0:hi - s0] = acc[:, lo - c0:hi - c0]
    if keys_transposed:
        k_t = _dot_t(wkt_ref[...], xb)
        kb_t, ka_t = k_t[:B_W], k_t[B_W:]
        cos_t = cost_ref[...]
        sin_t = sint_ref[...]
        half = HEAD_DIM // 2
        rotated = []
        for u in range(A_KV_HEADS):
            t = ka_t[u * HEAD_DIM:(u + 1) * HEAD_DIM]
            rotated.append(t * cos_t + jnp.concatenate([t[half:], t[:half]], axis=0) * sin_t)
        ka_t = jnp.concatenate(rotated, axis=0)
        for t_val, t_ref in ((kb_t, kbt_ref), (ka_t, kat_ref)):
            blk = t_ref.shape[3]
            for c in range(t_ref.shape[1]):
                t_ref[0, c] = t_val[:, c * blk:(c + 1) * blk].astype(t_ref.dtype)

        @pl.when(keep_state)
        def _():
            kb_ref[...] = kb_t.T
            ka_ref[...] = ka_t.T


def _proj_ab(xf, w, seq, pos0, w_k_t=None, kb_block=None, ka_block=None):
    n, d = xf.shape
    tm = _row_tile(n)
    assert seq % tm == 0 or tm % seq == 0
    kv_period = max(seq // tm, 1)
    assert min(seq, B_BAND_PAST) == min(seq, tm)
    cos, sin = _rope_tables(seq, pos0, HEAD_DIM, max(seq, tm))
    n_tab = cos.shape[0] // tm
    state_widths = (B_W, B_W, A_KV_W, A_KV_W)
    keys_transposed = w_k_t is not None
    rows = lambda i: (i, 0)
    tab = lambda i: (i % n_tab, 0)
    in_specs = [pl.BlockSpec((tm, d), rows), _resident(w.shape)]
    args = [xf, w]
    out_specs = [pl.BlockSpec((tm, w.shape[1]), rows)] + [
        pl.BlockSpec((tm, sw), lambda i: (i // kv_period, 0)) for sw in state_widths]
    out_shape = [jax.ShapeDtypeStruct((n, w.shape[1]), BF16)] + [
        jax.ShapeDtypeStruct((n // kv_period, sw), F32) for sw in state_widths]
    if keys_transposed:
        in_specs.append(_resident(w_k_t.shape))
        args.append(w_k_t)
    in_specs += [pl.BlockSpec((tm, LANES), tab), pl.BlockSpec((tm, LANES), tab)]
    args += [cos, sin]
    if keys_transposed:
        assert seq % tm == 0 and tm % kb_block == 0 and tm % ka_block == 0
        tiles = seq // tm
        tab_t = lambda i: (0, i % n_tab)
        in_specs += [pl.BlockSpec((LANES, tm), tab_t), pl.BlockSpec((LANES, tm), tab_t)]
        args += [cos.T, sin.T]
        for width, blk in ((B_W, kb_block), (A_KV_W, ka_block)):
            out_specs.append(pl.BlockSpec((1, tm // blk, width, blk), lambda i: (i // tiles, i % tiles, 0, 0)))
            out_shape.append(jax.ShapeDtypeStruct((n // seq, seq // blk, width, blk), BF16))
    return pl.pallas_call(
        functools.partial(_proj_ab_kernel, kv_period=kv_period, keys_transposed=keys_transposed),
        grid=(n // tm,),
        in_specs=in_specs,
        out_specs=out_specs,
        out_shape=out_shape,
        compiler_params=_params(1),
        name="proj_ab",
    )(*args)


def _rel_bias_kernel(tab_ref, o_ref, *, band):
    h = pl.program_id(0)
    shape = o_ref.shape[1:]
    r = lax.broadcasted_iota(jnp.int32, shape, 0)
    w = lax.broadcasted_iota(jnp.int32, shape, 1)
    idx = jnp.clip(band + r - w, -REL_CLIP, REL_CLIP) + REL_CLIP

    def body(d, acc):
        return jnp.where(idx == d, tab_ref[h, d], acc)

    o_ref[0] = lax.fori_loop(0, 2 * REL_CLIP + 1, body, jnp.zeros(shape, F32))


def _rel_bias(table, band, tq):
    heads = table.shape[0]
    return pl.pallas_call(
        functools.partial(_rel_bias_kernel, band=band),
        grid=(heads,),
        in_specs=[pl.BlockSpec(memory_space=pltpu.SMEM)],
        out_specs=pl.BlockSpec((1, tq, band + tq), lambda h: (h, 0, 0)),
        out_shape=jax.ShapeDtypeStruct((heads, tq, band + tq), F32),
        compiler_params=_params(1),
        name="rel_bias",
    )(table)


def _band_attn_kernel(*refs, heads, kv_heads, band, tq, past, has_bias, has_sinks):
    q_ref, k_ref, v_ref = refs[:3]
    rest = list(refs[3:])
    bias_ref = rest.pop(0) if has_bias else None
    sink_ref = rest.pop(0) if has_sinks else None
    o_ref = rest.pop(0)

    width = band + tq
    scale = HEAD_DIM ** -0.5
    ws = past + pl.program_id(1) * tq - band
    r = lax.broadcasted_iota(jnp.int32, (tq, width), 0) // CHUNK
    w = lax.broadcasted_iota(jnp.int32, (tq, width), 1)
    wc = w // CHUNK
    allowed = (wc >= r) & (wc <= r + band // CHUNK) & (w + ws >= 0)

    def window(ref, cols):
        if past >= band:
            return ref[0, pl.ds(pl.multiple_of(ws, CHUNK), width), cols]
        pieces = [ref[0, pl.ds(pl.multiple_of(jnp.maximum(ws + c * LANES, 0), LANES), LANES), cols]
                  for c in range(width // LANES)]
        return jnp.concatenate(pieces, axis=0)

    group = heads // kv_heads
    outs = []
    for kh in range(kv_heads):
        cols = slice(kh * HEAD_DIM, (kh + 1) * HEAD_DIM)
        k_w = window(k_ref, cols)
        v_w = window(v_ref, cols)
        for g in range(group):
            h = kh * group + g
            hcols = slice(h * HEAD_DIM, (h + 1) * HEAD_DIM)
            s = _dot_t(q_ref[0, :, hcols], k_w) * scale
            if has_bias:
                s = s + bias_ref[h]
            s = jnp.where(allowed, s, NEG_INF)
            m = jnp.max(s, -1, keepdims=True)
            if has_sinks:
                sink = sink_ref[h]
                m = jnp.maximum(m, sink)
            e = jnp.exp(s - m)
            den = jnp.sum(e, -1, keepdims=True)
            if has_sinks:
                den = den + jnp.exp(sink - m)
            p = (e * (1.0 / den)).astype(BF16)
            outs.append(_dot(p, v_w).astype(o_ref.dtype))
    o_ref[0] = jnp.concatenate(outs, axis=1)


def _band_attn(q, k, v, *, heads, kv_heads, band, tq, past, bias=None, sinks=None):
    (qa, qw, qi), (ka, kw, ki), (va, vw, vi) = q, k, v
    b, t, _ = qa.shape
    s_len = ka.shape[1]
    assert t % tq == 0 and s_len == past + t and qw == heads * HEAD_DIM and kw == kv_heads * HEAD_DIM
    assert past >= band or (past == 0 and tq % LANES == 0 and band % LANES == 0)
    in_specs = [
        pl.BlockSpec((1, tq, qw), lambda bi, i: (bi, i, qi)),
        pl.BlockSpec((1, s_len, kw), lambda bi, i: (bi, 0, ki)),
        pl.BlockSpec((1, s_len, vw), lambda bi, i: (bi, 0, vi)),
    ]
    args = [qa, ka, va]
    if bias is not None:
        in_specs.append(_resident(bias.shape))
        args.append(bias)
    if sinks is not None:
        in_specs.append(pl.BlockSpec(memory_space=pltpu.SMEM))
        args.append(sinks)
    return pl.pallas_call(
        functools.partial(_band_attn_kernel, heads=heads, kv_heads=kv_heads, band=band, tq=tq, past=past,
                          has_bias=bias is not None, has_sinks=sinks is not None),
        grid=(b, t // tq),
        in_specs=in_specs,
        out_specs=pl.BlockSpec((1, tq, qw), lambda bi, i: (bi, i, 0)),
        out_shape=jax.ShapeDtypeStruct((b, t, qw), BF16),
        compiler_params=_params(2),
        name="band_attn",
    )(*args)


def _rel_bias_folded_kernel(tab_ref, o_ref, *, band):
    h = pl.program_id(0)
    tq, width = o_ref.shape[1:]
    period = tq + width
    j = lax.broadcasted_iota(jnp.int32, (8, period), 1)
    dist = jnp.where(j < width, band - j, band - (j - period))
    idx = jnp.clip(dist, -REL_CLIP, REL_CLIP) + REL_CLIP

    def body(d, acc):
        return jnp.where(idx == d, tab_ref[h, d], acc)

    g = lax.fori_loop(0, 2 * REL_CLIP + 1, body, jnp.zeros((8, period), F32))
    full = jnp.concatenate([g] * (tq // 8), axis=0)
    bias = pltpu.roll(full, 0, 1, stride=1, stride_axis=0)[:, :width]
    rc = lax.broadcasted_iota(jnp.int32, (tq, width), 0) // CHUNK
    wc = lax.broadcasted_iota(jnp.int32, (tq, width), 1) // CHUNK
    o_ref[0] = jnp.where((wc >= rc) & (wc <= rc + band // CHUNK), bias * LOG2_E, NEG_INF)


def _rel_bias_folded(table, band, tq):
    heads = table.shape[0]
    assert (band + 2 * tq) % LANES == 0 and tq % 8 == 0
    return pl.pallas_call(
        functools.partial(_rel_bias_folded_kernel, band=band),
        grid=(heads,),
        in_specs=[pl.BlockSpec(memory_space=pltpu.SMEM)],
        out_specs=pl.BlockSpec((1, tq, band + tq), lambda h: (h, 0, 0)),
        out_shape=jax.ShapeDtypeStruct((heads, tq, band + tq), F32),
        compiler_params=_params(1),
        name="rel_bias_folded",
    )(table)


def _band_attn_t_kernel(*refs, heads, kv_heads, band, has_bias, has_sinks):
    q_ref, kt_ref, v_ref = refs[:3]
    rest = list(refs[3:])
    bias_ref = rest.pop(0) if has_bias else None
    sink_ref = rest.pop(0) if has_sinks else None
    o_ref = rest.pop(0)
    tq = kt_ref.shape[3]
    n_past = band // tq
    width = band + tq
    group = heads // kv_heads
    i = pl.program_id(1)
    ws = i * tq - band
    allowed = (lax.broadcasted_iota(jnp.int32, (1, width), 1) + ws) >= 0
    if not has_bias:
        rc = (lax.broadcasted_iota(jnp.int32, (group * tq, width), 0) % tq) // CHUNK
        wc = lax.broadcasted_iota(jnp.int32, (group * tq, width), 1) // CHUNK
        allowed = allowed & (wc >= rc) & (wc <= rc + band // CHUNK)
    ones = jnp.ones((width, HEAD_DIM), BF16)
    outs = []
    for kh in range(kv_heads):
        kc = slice(kh * HEAD_DIM, (kh + 1) * HEAD_DIM)
        hs = range(kh * group, (kh + 1) * group)
        q_st = jnp.concatenate([q_ref[0, :, h * HEAD_DIM:(h + 1) * HEAD_DIM] for h in hs], axis=0)
        k_t = jnp.concatenate([kt_ref[0, jnp.maximum(i - n_past + c, 0), kc, :] for c in range(n_past + 1)], axis=1)
        s = _dot(q_st, k_t) * (HEAD_DIM ** -0.5 * LOG2_E)
        if has_bias:
            s = s + jnp.concatenate([bias_ref[h] for h in hs], axis=0)
        s = jnp.where(allowed, s, NEG_INF)
        ms, ps = [], []
        for g, h in enumerate(hs):
            s_g = s[g * tq:(g + 1) * tq]
            m_g = jnp.max(s_g, -1, keepdims=True)
            if has_sinks:
                m_g = jnp.maximum(m_g, sink_ref[h] * LOG2_E)
            ms.append(m_g)
            ps.append(jnp.exp2(s_g - m_g).astype(BF16))
        v_w = jnp.concatenate([v_ref[0, pl.ds(pl.multiple_of(jnp.maximum(ws + c * tq, 0), tq), tq), kc]
                               for c in range(n_past + 1)], axis=0)
        o_ext = _dot(jnp.concatenate(ps, axis=0), jnp.concatenate([v_w, ones], axis=1))
        for g, h in enumerate(hs):
            o_g = o_ext[g * tq:(g + 1) * tq]
            den = o_g[:, HEAD_DIM:]
            if has_sinks:
                den = den + jnp.exp2(sink_ref[h] * LOG2_E - ms[g])
            outs.append((o_g[:, :HEAD_DIM] * (1.0 / den)).astype(o_ref.dtype))
    o_ref[0] = jnp.concatenate(outs, axis=1)


def _band_attn_t(q, kt, v, *, heads, kv_heads, band, bias=None, sinks=None):
    (qa, qw, qi), (va, vw, vi) = q, v
    b, t, _ = qa.shape
    _, nkb, _, tq = kt.shape
    assert nkb * tq == t and band % tq == 0 and qw == heads * HEAD_DIM and vw == kv_heads * HEAD_DIM
    assert bias is None or heads == kv_heads
    in_specs = [
        pl.BlockSpec((1, tq, qw), lambda bi, i: (bi, i, qi)),
        pl.BlockSpec((1, nkb, kv_heads * HEAD_DIM, tq), lambda bi, i: (bi, 0, 0, 0)),
        pl.BlockSpec((1, t, vw), lambda bi, i: (bi, 0, vi)),
    ]
    args = [qa, kt, va]
    if bias is not None:
        in_specs.append(_resident(bias.shape))
        args.append(bias)
    if sinks is not None:
        in_specs.append(pl.BlockSpec(memory_space=pltpu.SMEM))
        args.append(sinks)
    return pl.pallas_call(
        functools.partial(_band_attn_t_kernel, heads=heads, kv_heads=kv_heads, band=band,
                          has_bias=bias is not None, has_sinks=sinks is not None),
        grid=(b, t // tq),
        in_specs=in_specs,
        out_specs=pl.BlockSpec((1, tq, qw), lambda bi, i: (bi, i, 0)),
        out_shape=jax.ShapeDtypeStruct((b, t, qw), BF16),
        compiler_params=_params(2),
        name="band_attn_t",
    )(*args)


def _outproj_ln_kernel(*refs, n_in):
    a_refs = refs[:n_in]
    w_ref, x_ref, g_ref, b_ref, o_ref = refs[n_in:]
    d_out = o_ref.shape[1]
    for j in range(d_out // COL_CHUNK):
        cols = slice(j * COL_CHUNK, (j + 1) * COL_CHUNK)
        y = DEEPNORM_ALPHA * x_ref[:, cols]
        r0 = 0
        for a_ref in a_refs:
            kk = a_ref.shape[1]
            y = y + _dot(a_ref[...], w_ref[r0:r0 + kk, cols])
            r0 += kk
        o_ref[:, cols] = y
    o_ref[...] = _layer_norm(o_ref[...], g_ref[...], b_ref[...])


def _outproj_ln(a_list, w, xf, g, b):
    n, d = xf.shape
    tm = _row_tile(n)
    assert sum(a.shape[1] for a in a_list) == w.shape[0]
    return pl.pallas_call(
        functools.partial(_outproj_ln_kernel, n_in=len(a_list)),
        grid=(n // tm,),
        in_specs=[pl.BlockSpec((tm, a.shape[1]), lambda i: (i, 0)) for a in a_list] + [
            _resident(w.shape),
            pl.BlockSpec((tm, d), lambda i: (i, 0)),
            _resident(g.shape),
            _resident(b.shape),
        ],
        out_specs=pl.BlockSpec((tm, d), lambda i: (i, 0)),
        out_shape=jax.ShapeDtypeStruct((n, d), F32),
        compiler_params=_params(1),
        name="outproj_ln",
    )(*a_list, w, xf, g, b)


def _mlp_ln_kernel(x_ref, wu_ref, wd_ref, g_ref, b_ref, o_ref, xb_ref):
    f = pl.program_id(1)

    @pl.when(f == 0)
    def _():
        xb_ref[...] = x_ref[...].astype(BF16)
        o_ref[...] = jnp.zeros(o_ref.shape, o_ref.dtype)

    hid = _dot(xb_ref[...], wu_ref[...])
    hid = jnp.square(jnp.maximum(hid, 0.0)).astype(BF16)
    for j in range(o_ref.shape[1] // COL_CHUNK):
        cols = slice(j * COL_CHUNK, (j + 1) * COL_CHUNK)
        o_ref[:, cols] += _dot(hid, wd_ref[:, cols])

    @pl.when(f == pl.num_programs(1) - 1)
    def _():
        o_ref[...] = _layer_norm(DEEPNORM_ALPHA * x_ref[...] + o_ref[...], g_ref[...], b_ref[...])


def _mlp_ln(xf, layer, w_up, w_down, g, b):
    n, d = xf.shape
    d_ff = w_up.shape[2]
    tm = MLP_ROW_TILE if n % MLP_ROW_TILE == 0 else n
    tf = MLP_FF_TILE
    return pl.pallas_call(
        _mlp_ln_kernel,
        grid=(n // tm, d_ff // tf),
        in_specs=[
            pl.BlockSpec((tm, d), lambda i, f: (i, 0)),
            pl.BlockSpec((None, d, tf), lambda i, f: (layer, 0, f)),
            pl.BlockSpec((None, tf, d), lambda i, f: (layer, f, 0)),
            _resident(g.shape),
            _resident(b.shape),
        ],
        out_specs=pl.BlockSpec((tm, d), lambda i, f: (i, 0)),
        out_shape=jax.ShapeDtypeStruct((n, d), F32),
        scratch_shapes=[pltpu.VMEM((tm, d), BF16)],
        compiler_params=_params(2),
        name="mlp_ln",
    )(xf, w_up, w_down, g, b)


def _ple_kernel(x_ref, p_ref, wg_ref, bg_ref, wp_ref, o_ref):
    xb = x_ref[...].astype(BF16)
    pb = p_ref[...].astype(BF16)
    for j in range(o_ref.shape[1] // COL_CHUNK):
        cols = slice(j * COL_CHUNK, (j + 1) * COL_CHUNK)
        gate = jax.nn.sigmoid(_dot(xb, wg_ref[:, cols]) + bg_ref[:, cols])
        o_ref[:, cols] = x_ref[:, cols] + gate * _dot(pb, wp_ref[:, cols])


def _ple(xf, p_all, layer, wg, bg, wp):
    n, d = xf.shape
    tm = _row_tile(n)
    return pl.pallas_call(
        _ple_kernel,
        grid=(n // tm,),
        in_specs=[
            pl.BlockSpec((tm, d), lambda i: (i, 0)),
            pl.BlockSpec((None, tm, p_all.shape[2]), lambda i: (layer, i, 0)),
            _resident_slab(wg.shape, layer),
            _resident(bg.shape),
            _resident_slab(wp.shape, layer),
        ],
        out_specs=pl.BlockSpec((tm, d), lambda i: (i, 0)),
        out_shape=jax.ShapeDtypeStruct((n, d), F32),
        compiler_params=_params(1),
        name="ple",
    )(xf, p_all, wg, bg, wp)


Q_GROUP = 4


def _proj_cq_kernel(x_ref, w_ref, gq_ref, gkv_ref, cos_ref, sin_ref, wq_ref, q_ref, ckv_ref, kr_ref, krp_ref):
    cos = cos_ref[...]
    sin = sin_ref[...]
    h = _dot(x_ref[...].astype(BF16), w_ref[...])
    cq = _rms_norm(h[:, :C_Q_RANK], gq_ref[...]).astype(BF16)
    ckv_ref[...] = _rms_norm(h[:, C_Q_RANK:C_Q_RANK + C_KV_RANK], gkv_ref[...])
    rot = _rope_tile(h[:, C_Q_RANK + C_KV_RANK:], cos, sin, C_ROPE)
    lane = lax.broadcasted_iota(jnp.int32, rot.shape, 1)
    rot = jnp.where(lane < C_ROPE, rot, 0.0)
    kr_ref[...] = rot[:, :C_ROPE]
    krp_ref[...] = rot.astype(krp_ref.dtype)

    grp_w = Q_GROUP * (C_NOPE + C_ROPE)
    for gi in range(C_HEADS // Q_GROUP):
        acc = _dot(cq, wq_ref[:, gi * grp_w:(gi + 1) * grp_w]) * (MLA_SCALE * LOG2_E)
        for u in range(Q_GROUP):
            o0 = (gi * Q_GROUP + u) * C_QK
            q_ref[:, o0:o0 + C_NOPE] = acc[:, u * C_NOPE:(u + 1) * C_NOPE].astype(q_ref.dtype)
            if u % 2 == 0:
                r0 = Q_GROUP * C_NOPE + (u // 2) * LANES
                qrot = _rope_tile(acc[:, r0:r0 + LANES], cos, sin, C_ROPE)
                piece = qrot
            else:
                piece = pltpu.roll(qrot, C_ROPE, 1)
            q_ref[:, o0 + C_NOPE:o0 + C_QK] = jnp.where(lane < C_ROPE, piece, 0.0).astype(q_ref.dtype)


def _proj_cq(xf, w, gq, gkv, cos, sin, wq):
    n, d = xf.shape
    tm = _row_tile(n)
    n_tab = cos.shape[0] // tm
    rows = lambda i: (i, 0)
    tab = lambda i: (i % n_tab, 0)
    return pl.pallas_call(
        _proj_cq_kernel,
        grid=(n // tm,),
        in_specs=[
            pl.BlockSpec((tm, d), rows),
            _resident(w.shape),
            _resident(gq.shape),
            _resident(gkv.shape),
            pl.BlockSpec((tm, LANES), tab),
            pl.BlockSpec((tm, LANES), tab),
            _resident(wq.shape),
        ],
        out_specs=[
            pl.BlockSpec((tm, C_HEADS * C_QK), rows),
            pl.BlockSpec((tm, C_KV_RANK), rows),
            pl.BlockSpec((tm, C_ROPE), rows),
            pl.BlockSpec((tm, LANES), rows),
        ],
        out_shape=[
            jax.ShapeDtypeStruct((n, C_HEADS * C_QK), BF16),
            jax.ShapeDtypeStruct((n, C_KV_RANK), F32),
            jax.ShapeDtypeStruct((n, C_ROPE), F32),
            jax.ShapeDtypeStruct((n, LANES), BF16),
        ],
        compiler_params=_params(1),
        name="proj_cq",
    )(xf, w, gq, gkv, cos, sin, wq)


def _kv_c_kernel(ckv_ref, krp_ref, wkt_ref, wv_ref, kt_ref, krt_ref, v_ref):
    cb = ckv_ref[...].astype(BF16)
    eye = (lax.broadcasted_iota(jnp.int32, (LANES, LANES), 0)
           == lax.broadcasted_iota(jnp.int32, (LANES, LANES), 1)).astype(F32).astype(BF16)
    krt_ref[0, 0] = _dot_t(eye, krp_ref[...]).astype(krt_ref.dtype)
    kt_ref[0, 0] = _dot_t(wkt_ref[...], cb).astype(kt_ref.dtype)
    for j in range(C_HEADS * C_V // COL_CHUNK):
        cols = slice(j * COL_CHUNK, (j + 1) * COL_CHUNK)
        v_ref[:, cols] = _dot(cb, wv_ref[:, cols]).astype(v_ref.dtype)


def _kv_c(ckv, krp, wkt, wv, batch, tk):
    n = ckv.shape[0]
    nkb = n // batch // tk
    assert n == batch * nkb * tk
    rows = lambda i: (i, 0)
    blk = lambda i: (i // nkb, i % nkb, 0, 0)
    return pl.pallas_call(
        _kv_c_kernel,
        grid=(n // tk,),
        in_specs=[pl.BlockSpec((tk, C_KV_RANK), rows), pl.BlockSpec((tk, LANES), rows), _resident(wkt.shape),
                  _resident(wv.shape)],
        out_specs=[pl.BlockSpec((1, 1, C_HEADS * C_NOPE, tk), blk), pl.BlockSpec((1, 1, LANES, tk), blk),
                   pl.BlockSpec((tk, C_HEADS * C_V), rows)],
        out_shape=[
            jax.ShapeDtypeStruct((batch, nkb, C_HEADS * C_NOPE, tk), BF16),
            jax.ShapeDtypeStruct((batch, nkb, LANES, tk), BF16),
            jax.ShapeDtypeStruct((n, C_HEADS * C_V), BF16),
        ],
        compiler_params=_params(1),
        name="kv_c",
    )(ckv, krp, wkt, wv)


MLA_LONG_BLOCK = 512


def _mla_attn_kernel(q_ref, kt_ref, krt_ref, v_ref, o_ref, *, pos0, heads):
    t = q_ref.shape[1]
    tq = tk = kt_ref.shape[3]
    row = lax.broadcasted_iota(jnp.int32, (tq, tk), 0) // CHUNK
    col = lax.broadcasted_iota(jnp.int32, (tq, tk), 1) // CHUNK
    diag_ok = col <= row
    ones = jnp.ones((tk, C_V), BF16)

    def k_slab(g, kb):
        return jnp.concatenate([kt_ref[0, kb, g * C_NOPE:(g + 1) * C_NOPE, :], krt_ref[0, kb]], axis=0)

    def v_slab(g, kb):
        return jnp.concatenate([v_ref[0, kb * tk:(kb + 1) * tk, g * C_V:(g + 1) * C_V], ones], axis=1)

    def step(s, m, acc, v_ext):
        m_new = jnp.maximum(m, jnp.max(s, -1, keepdims=True))
        p = jnp.exp2(s - m_new).astype(BF16)
        return m_new, jnp.exp2(m - m_new) * acc + _dot(p, v_ext)

    def finish(acc):
        return (acc[:, :C_V] * (1.0 / acc[:, C_V:])).astype(o_ref.dtype)

    for qi in range(t // tq):
        q0 = qi * tq
        qs = [q_ref[0, q0:q0 + tq, g * C_QK:(g + 1) * C_QK] for g in range(heads)]
        n_full = (pos0 + q0) // tk
        state = [(jnp.full((tq, 1), NEG_INF, F32), jnp.zeros((tq, 2 * C_V), F32)) for _ in range(heads)]
        for kb in range(n_full):
            raw = [_dot(qs[g], k_slab(g, kb)) for g in range(heads)]
            state = [step(raw[g], *state[g], v_slab(g, kb)) for g in range(heads)]
        raw = [jnp.where(diag_ok, _dot(qs[g], k_slab(g, n_full)), NEG_INF) for g in range(heads)]
        state = [step(raw[g], *state[g], v_slab(g, n_full)) for g in range(heads)]
        for g in range(heads):
            o_ref[0, q0:q0 + tq, g * C_V:(g + 1) * C_V] = finish(state[g][1])


def _mla_attn(q, kt, krt, v, *, pos0, heads=4):
    b, t, _ = q.shape
    _, nkb, _, tk = kt.shape
    s_len = nkb * tk
    assert t % tk == 0 and pos0 % tk == 0 and pos0 + t <= s_len and v.shape[1] == s_len
    return pl.pallas_call(
        functools.partial(_mla_attn_kernel, pos0=pos0, heads=heads),
        grid=(b, C_HEADS // heads),
        in_specs=[
            pl.BlockSpec((1, t, heads * C_QK), lambda bi, h: (bi, 0, h)),
            pl.BlockSpec((1, nkb, heads * C_NOPE, tk), lambda bi, h: (bi, 0, h, 0)),
            pl.BlockSpec((1, nkb, LANES, tk), lambda bi, h: (bi, 0, 0, 0)),
            pl.BlockSpec((1, s_len, heads * C_V), lambda bi, h: (bi, 0, h)),
        ],
        out_specs=pl.BlockSpec((1, t, heads * C_V), lambda bi, h: (bi, 0, h)),
        out_shape=jax.ShapeDtypeStruct((b, t, C_HEADS * C_V), BF16),
        compiler_params=_params(2),
        name="mla_attn",
    )(q, kt, krt, v)


def _mla_absorbed_kernel(q_ref, ckv_new_ref, kr_new_ref, ckv_past_ref, kr_past_ref, wkt_ref, wv_ref, o_ref):
    t = q_ref.shape[1]
    q_lat, q_rope = [], []
    for h in range(C_HEADS):
        q_lat.append(_dot(q_ref[0, :, h * C_QK:h * C_QK + C_NOPE], wkt_ref[h * C_NOPE:(h + 1) * C_NOPE, :]))
        q_rope.append(q_ref[0, :, h * C_QK + C_NOPE:h * C_QK + C_NOPE + C_ROPE])
    q_lat = jnp.concatenate(q_lat, axis=0).astype(BF16)
    q_rope = jnp.concatenate(q_rope, axis=0)
    lat = [ckv_past_ref[...].astype(BF16), ckv_new_ref[0].astype(BF16)]
    rot = [kr_past_ref[...].astype(BF16), kr_new_ref[0].astype(BF16)]
    s = jnp.concatenate([_dot_t(q_lat, c) + _dot_t(q_rope, r) for c, r in zip(lat, rot)], axis=1)
    e = jnp.exp2(s - jnp.max(s, -1, keepdims=True))
    p = (e * (1.0 / jnp.sum(e, -1, keepdims=True))).astype(BF16)
    n_past = lat[0].shape[0]
    o_lat = (_dot(p[:, :n_past], lat[0]) + _dot(p[:, n_past:], lat[1])).astype(BF16)
    o_ref[0] = jnp.concatenate([_dot(o_lat[h * t:(h + 1) * t], wv_ref[:, h * C_V:(h + 1) * C_V])
                                for h in range(C_HEADS)], axis=1).astype(o_ref.dtype)


def _mla_absorbed(q, ckv_new, kr_new, ckv_past, kr_past, wkt, wv, *, pos0):
    b, t, _ = q.shape
    n_past = ckv_past.shape[2]
    assert t == CHUNK and pos0 % CHUNK == 0 and n_past <= pos0
    return pl.pallas_call(
        _mla_absorbed_kernel,
        grid=(b,),
        in_specs=[
            pl.BlockSpec((1, t, C_HEADS * C_QK), lambda bi: (bi, 0, 0)),
            pl.BlockSpec((1, t, C_KV_RANK), lambda bi: (bi, 0, 0)),
            pl.BlockSpec((1, t, C_ROPE), lambda bi: (bi, 0, 0)),
            pl.BlockSpec((None, None, n_past, C_KV_RANK), lambda bi: (0, bi, 0, 0)),
            pl.BlockSpec((None, None, n_past, C_ROPE), lambda bi: (0, bi, 0, 0)),
            _resident(wkt.shape),
            _resident(wv.shape),
        ],
        out_specs=pl.BlockSpec((1, t, C_HEADS * C_V), lambda bi: (bi, 0, 0)),
        out_shape=jax.ShapeDtypeStruct((b, t, C_HEADS * C_V), BF16),
        compiler_params=_params(1),
        name="mla_absorbed",
    )(q, ckv_new, kr_new, ckv_past, kr_past, wkt, wv)


def _prepare_weights(w):
    o1 = A_Q_W
    o2 = o1 + A_KV_W
    o3 = o2 + A_KV_W
    o4 = o3 + B_W
    o5 = o4 + B_W
    w_ab = w['w_in_ab'][0]
    w_ab = jnp.concatenate([w_ab[:, :o1], w_ab[:, o3:o4], w_ab[:, o4:o5], w_ab[:, o5:], w_ab[:, o1:o2],
                            w_ab[:, o2:o3]], axis=1)
    ka0 = AB_KV_COL0 + 2 * B_W
    w_ab_long = jnp.concatenate([w_ab[:, :AB_KV_COL0], w_ab[:, AB_KV_COL0 + B_W:ka0], w_ab[:, ka0 + A_KV_W:]], axis=1)
    w_k_t_ab = jnp.concatenate([w_ab[:, AB_KV_COL0:AB_KV_COL0 + B_W], w_ab[:, ka0:ka0 + A_KV_W]], axis=1).T
    w_c = jnp.pad(w['w_in_c'][0], ((0, 0), (0, LANES - C_ROPE)))
    hq = C_NOPE + C_ROPE
    q_cols = []
    for g0 in range(0, C_HEADS, Q_GROUP):
        q_cols += [jnp.arange(h * hq, h * hq + C_NOPE) for h in range(g0, g0 + Q_GROUP)]
        q_cols += [jnp.arange(h * hq + C_NOPE, (h + 1) * hq) for h in range(g0, g0 + Q_GROUP)]
    w_q = w['w_q_b_c'][0][:, jnp.concatenate(q_cols)]
    hkv = C_NOPE + C_V
    k_cols = jnp.concatenate([jnp.arange(h * hkv, h * hkv + C_NOPE) for h in range(C_HEADS)])
    v_cols = jnp.concatenate([jnp.arange(h * hkv + C_NOPE, (h + 1) * hkv) for h in range(C_HEADS)])
    w_k_t = w['w_kv_b_c'][0][:, k_cols].T
    w_v = w['w_kv_b_c'][0][:, v_cols]
    row = lambda a: a.reshape(1, -1)
    return {
        'w_in_ab': w_ab.astype(BF16), 'w_in_ab_long': w_ab_long.astype(BF16), 'w_k_t_ab': w_k_t_ab.astype(BF16),
        'w_out_ab': w['w_out_ab'][0].astype(BF16),
        'w_in_c': w_c.astype(BF16), 'w_q_b_c': w_q.astype(BF16), 'w_k_t_c': w_k_t.astype(BF16),
        'w_v_c': w_v.astype(BF16),
        'w_out_c': w['w_out_c'][0].astype(BF16),
        'g_q_c': row(w['g_q_c'][0]), 'g_kv_c': row(w['g_kv_c'][0]),
        'sinks_a': w['sinks_a'][0], 'rel_bias_b': w['rel_bias_b'][0],
        'ln1_g': [row(w['ln1_g'][i]) for i in range(DEPTH)], 'ln1_b': [row(w['ln1_b'][i]) for i in range(DEPTH)],
        'ln2_g': [row(w['ln2_g'][i]) for i in range(DEPTH)], 'ln2_b': [row(w['ln2_b'][i]) for i in range(DEPTH)],
        'w_mlp_up': w['w_mlp_up'].astype(BF16), 'w_mlp_down': w['w_mlp_down'].astype(BF16),
        'w_ple_gate': w['w_ple_gate'].astype(BF16), 'w_ple': w['w_ple'].astype(BF16),
        'b_ple_gate': [row(w['b_ple_gate'][i]) for i in range(DEPTH)],
    }


def _channel_mix(xf, p_all, pw, i):
    xf = _mlp_ln(xf, i, pw['w_mlp_up'], pw['w_mlp_down'], pw['ln2_g'][i], pw['ln2_b'][i])
    return _ple(xf, p_all, i, pw['w_ple_gate'], pw['b_ple_gate'][i], pw['w_ple'])


def _trunk(x, p, pos0, past, pw):
    b, t, d = x.shape
    n = b * t
    tm = _row_tile(n)
    xf = x.reshape(n, d)

    if past is None:
        h, kb_new, vb_new, ka_new, va_new, kb_t, ka_t = _proj_ab(
            xf, pw['w_in_ab_long'], t, pos0, pw['w_k_t_ab'], BAND_B_BLOCK, BAND_A_BLOCK)
    else:
        h, kb_new, vb_new, ka_new, va_new = _proj_ab(xf, pw['w_in_ab'], t, pos0)
    h3 = h.reshape(b, t, h.shape[1])
    kb_new, vb_new, ka_new, va_new = [a.reshape(b, a.shape[0] // b, a.shape[1])
                                      for a in (kb_new, vb_new, ka_new, va_new)]
    q_a = (h3, A_Q_W, 0)
    q_b = (h3, B_W, 1)
    if past is None:
        attn_a = _band_attn_t(q_a, ka_t, (h3, A_KV_W, (AB_KV_COL0 + B_W) // A_KV_W), heads=A_HEADS,
                              kv_heads=A_KV_HEADS, band=WINDOW, sinks=pw['sinks_a'])
        bias_b = _rel_bias_folded(pw['rel_bias_b'], B_BAND_PAST, BAND_B_BLOCK)
        attn_b = _band_attn_t(q_b, kb_t, (h3, B_W, 2), heads=B_HEADS, kv_heads=B_HEADS, band=B_BAND_PAST,
                              bias=bias_b)
        ak, av, bk, bv = ka_new[:, -WINDOW:], va_new[:, -WINDOW:], kb_new, vb_new
    else:
        n_past_a, n_past_b = past[0].shape[1], past[2].shape[1]
        full = [jnp.concatenate([c.reshape(b, c.shape[1], -1), new], axis=1)
                for c, new in zip(past[:4], (ka_new, va_new, kb_new, vb_new))]
        ak, av = full[0][:, -WINDOW:], full[1][:, -WINDOW:]
        bk, bv = full[2][:, -B_BAND_PAST:], full[3][:, -B_BAND_PAST:]
        k_a, v_a, k_b, v_b = [(f.astype(BF16), f.shape[2], 0) for f in full]
        attn_a = _band_attn(q_a, k_a, v_a, heads=A_HEADS, kv_heads=A_KV_HEADS, band=WINDOW, tq=CHUNK,
                            past=n_past_a, sinks=pw['sinks_a'])
        bias_b = _rel_bias(pw['rel_bias_b'], B_BAND_PAST, CHUNK)
        attn_b = _band_attn(q_b, k_b, v_b, heads=B_HEADS, kv_heads=B_HEADS, band=B_BAND_PAST, tq=CHUNK,
                            past=n_past_b, bias=bias_b)
    xf = _outproj_ln([attn_a.reshape(n, A_Q_W), attn_b.reshape(n, B_W)], pw['w_out_ab'], xf,
                     pw['ln1_g'][0], pw['ln1_b'][0])
    p_all = p.reshape(p.shape[0], n, p.shape[3])
    xf = _channel_mix(xf, p_all, pw, 0)

    cos_c, sin_c = _rope_tables(t, pos0, C_ROPE, max(t, tm))
    q, ckv, kr, krp = _proj_cq(xf, pw['w_in_c'], pw['g_q_c'], pw['g_kv_c'], cos_c, sin_c, pw['w_q_b_c'])
    q = q.reshape(b, t, C_HEADS * C_QK)
    if past is None:
        kt_c, krt_c, v_c = _kv_c(ckv, krp, pw['w_k_t_c'], pw['w_v_c'], b, min(MLA_LONG_BLOCK, t))
        attn_c = _mla_attn(q, kt_c, krt_c, v_c.reshape(b, t, C_HEADS * C_V), pos0=pos0)
    else:
        attn_c = _mla_absorbed(q, ckv.reshape(b, t, C_KV_RANK), kr.reshape(b, t, C_ROPE), past[4], past[5],
                               pw['w_k_t_c'], pw['w_v_c'], pos0=pos0)
    xf = _outproj_ln([attn_c.reshape(n, C_HEADS * C_V)], pw['w_out_c'], xf, pw['ln1_g'][1], pw['ln1_b'][1])
    xf = _channel_mix(xf, p_all, pw, 1)

    heads4 = lambda a, hh: a.reshape(1, b, a.shape[1], hh, HEAD_DIM)
    return (xf.reshape(b, t, d), heads4(ak, A_KV_HEADS), heads4(av, A_KV_HEADS), heads4(bk, B_HEADS),
            heads4(bv, B_HEADS), ckv.reshape(1, b, t, C_KV_RANK), kr.reshape(1, b, t, C_ROPE))


def kernel(x_prompt, x_sample, cache_a_k, cache_a_v, cache_b_k, cache_b_v, cache_c_kv, cache_c_krope, p_prompt,
           p_sample, w_in_ab, sinks_a, rel_bias_b, w_out_ab, w_in_c, g_q_c, w_q_b_c, g_kv_c, w_kv_b_c, w_out_c,
           ln1_g, ln1_b, ln2_g, ln2_b, w_mlp_up, w_mlp_down, w_ple_gate, b_ple_gate, w_ple):
    pw = _prepare_weights({
        'w_in_ab': w_in_ab, 'sinks_a': sinks_a, 'rel_bias_b': rel_bias_b, 'w_out_ab': w_out_ab,
        'w_in_c': w_in_c, 'g_q_c': g_q_c, 'w_q_b_c': w_q_b_c, 'g_kv_c': g_kv_c, 'w_kv_b_c': w_kv_b_c,
        'w_out_c': w_out_c, 'ln1_g': ln1_g, 'ln1_b': ln1_b, 'ln2_g': ln2_g, 'ln2_b': ln2_b,
        'w_mlp_up': w_mlp_up, 'w_mlp_down': w_mlp_down, 'w_ple_gate': w_ple_gate, 'b_ple_gate': b_ple_gate,
        'w_ple': w_ple,
    })
    prompt = _trunk(x_prompt, p_prompt, 0, None, pw)
    past = (cache_a_k[0], cache_a_v[0], cache_b_k[0], cache_b_v[0], cache_c_kv, cache_c_krope)
    sample = _trunk(x_sample, p_sample, cache_c_kv.shape[2], past, pw)
    return (prompt[0], sample[0]) + prompt[1:] + sample[1:]
```

```python
import functools

import jax
import jax.numpy as jnp
from jax import lax
from jax.experimental import pallas as pl
from jax.experimental.pallas import tpu as pltpu

F32 = jnp.float32
BF16 = jnp.bfloat16

CHUNK = 64
HEAD_DIM = 128
A_HEADS = 8
A_KV_HEADS = 2
WINDOW = 128
B_HEADS = 8
B_BAND_PAST = 512
REL_CLIP = 128
C_HEADS = 16
C_Q_RANK = 768
C_KV_RANK = 512
C_NOPE = 128
C_ROPE = 64
C_V = 128
DEPTH = 2
ROPE_THETA = 10000.0
LN_EPS = 1e-5
RMS_EPS = 1e-6
NEG_INF = -1e30
DEEPNORM_ALPHA = (2 * DEPTH) ** 0.25
LOG2_E = 1.4426950408889634
MLA_SCALE = (C_NOPE + C_ROPE) ** -0.5

A_Q_W = A_HEADS * HEAD_DIM
A_KV_W = A_KV_HEADS * HEAD_DIM
B_W = B_HEADS * HEAD_DIM
AB_IN_W = A_Q_W + 2 * A_KV_W + 3 * B_W
AB_KV_COL0 = A_Q_W + B_W
AB_KV_W = AB_IN_W - AB_KV_COL0
C_IN_W = C_Q_RANK + C_KV_RANK + C_ROPE
C_QK = 256

LANES = 128
V7X_VMEM_BYTES = 64 * 1024 * 1024
VMEM_LIMIT = V7X_VMEM_BYTES - 8 * 1024 * 1024

ROW_TILE = 512
COL_CHUNK = 512
MLP_ROW_TILE = 512
MLP_FF_TILE = 1024
BAND_B_BLOCK = 256
BAND_A_BLOCK = 128
BAND_STEP_BLOCKS = 2


def _params(n_axes):
    return pltpu.CompilerParams(dimension_semantics=("arbitrary",) * n_axes, vmem_limit_bytes=VMEM_LIMIT)


def _resident(shape):
    nd = len(shape)
    return pl.BlockSpec(shape, lambda *_: (0,) * nd, pipeline_mode=pl.Buffered(1))


def _resident_slab(shape, index):
    nd = len(shape) - 1
    return pl.BlockSpec((None,) + tuple(shape[1:]), lambda *_: (index,) + (0,) * nd, pipeline_mode=pl.Buffered(1))


def _row_tile(n):
    return ROW_TILE if n % ROW_TILE == 0 else n


def _dot(a, b):
    return jnp.dot(a, b, preferred_element_type=F32)


def _dot_t(a, b):
    return lax.dot_general(a, b, (((1,), (1,)), ((), ())), preferred_element_type=F32)


def _layer_norm(y, g, b):
    mu = jnp.mean(y, -1, keepdims=True)
    var = jnp.mean(jnp.square(y - mu), -1, keepdims=True)
    return (y - mu) * lax.rsqrt(var + LN_EPS) * g + b


def _rms_norm(y, g):
    return y * lax.rsqrt(jnp.mean(jnp.square(y), -1, keepdims=True) + RMS_EPS) * g


def _rope_tile(t, cos, sin, d):
    if d == LANES:
        swapped = pltpu.roll(t, LANES // 2, 1)
    else:
        lane = lax.broadcasted_iota(jnp.int32, t.shape, 1)
        swapped = jnp.where((lane % d) < d // 2, pltpu.roll(t, LANES - d // 2, 1), pltpu.roll(t, d // 2, 1))
    return t * cos + swapped * sin


def _rope_tables(t, pos0, d, rows):
    half = d // 2
    inv = ROPE_THETA ** (-jnp.arange(half, dtype=F32) * (2.0 / d))
    ang = (jnp.arange(t, dtype=F32) + pos0)[:, None] * inv[None, :]
    cos = jnp.cos(ang)
    sin = jnp.sin(ang)
    reps = (rows // t, LANES // d)
    return jnp.tile(jnp.concatenate([cos, cos], 1), reps), jnp.tile(jnp.concatenate([-sin, sin], 1), reps)


def _proj_ab_kernel(*refs, kv_period, keys_transposed):
    if keys_transposed:
        (x_ref, w_ref, wkt_ref, cos_ref, sin_ref, cost_ref, sint_ref,
         h_ref, kb_ref, vb_ref, ka_ref, va_ref, kbt_ref, kat_ref) = refs
    else:
        x_ref, w_ref, cos_ref, sin_ref, h_ref, kb_ref, vb_ref, ka_ref, va_ref = refs
    xb = x_ref[...].astype(BF16)
    cos = cos_ref[...]
    sin = sin_ref[...]
    keep_state = (pl.program_id(0) % kv_period) == kv_period - 1
    kv0 = AB_KV_COL0
    rope_tiles = set(range(A_HEADS))
    if keys_transposed:
        state_refs = [(kv0, vb_ref), (kv0 + B_W, va_ref)]
    else:
        state_refs = [(kv0, kb_ref), (kv0 + B_W, vb_ref), (kv0 + 2 * B_W, ka_ref), (kv0 + 2 * B_W + A_KV_W, va_ref)]
        rope_tiles |= {(kv0 + 2 * B_W) // LANES + u for u in range(A_KV_HEADS)}
    if keys_transposed:
        k_t = _dot_t(wkt_ref[...], xb)
        kb_t, ka_t = k_t[:B_W], k_t[B_W:]
        cos_t = cost_ref[...]
        sin_t = sint_ref[...]
        half = HEAD_DIM // 2
        rotated = []
        for u in range(A_KV_HEADS):
            t = ka_t[u * HEAD_DIM:(u + 1) * HEAD_DIM]
            rotated.append(t * cos_t + jnp.concatenate([t[half:], t[:half]], axis=0) * sin_t)
        ka_t = jnp.concatenate(rotated, axis=0)
        for t_val, t_ref in ((kb_t, kbt_ref), (ka_t, kat_ref)):
            blk = t_ref.shape[3]
            for c in range(t_ref.shape[1]):
                t_ref[0, c] = t_val[:, c * blk:(c + 1) * blk].astype(t_ref.dtype)
    width = w_ref.shape[1]
    for c0 in range(0, width, COL_CHUNK):
        cw = min(COL_CHUNK, width - c0)
        acc = _dot(xb, w_ref[:, c0:c0 + cw])
        parts = []
        for u in range(cw // LANES):
            part = acc[:, u * LANES:(u + 1) * LANES]
            if c0 // LANES + u in rope_tiles:
                part = _rope_tile(part, cos, sin, HEAD_DIM)
            parts.append(part)
        acc = jnp.concatenate(parts, axis=1)
        h_ref[:, c0:c0 + cw] = acc.astype(h_ref.dtype)
        for s0, ref in state_refs:
            lo, hi = max(c0, s0), min(c0 + cw, s0 + ref.shape[1])
            if lo < hi:
                @pl.when(keep_state)
                def _(acc=acc, ref=ref, lo=lo, hi=hi, s0=s0, c0=c0):
                    ref[:, lo - s0:hi - s0] = acc[:, lo - c0:hi - c0]
    if keys_transposed:
        @pl.when(keep_state)
        def _():
            kb_ref[...] = kb_t.T
            ka_ref[...] = ka_t.T


def _proj_ab(xf, w, seq, pos0, w_k_t=None, kb_block=None, ka_block=None):
    n, d = xf.shape
    tm = _row_tile(n)
    assert seq % tm == 0 or tm % seq == 0
    kv_period = max(seq // tm, 1)
    assert min(seq, B_BAND_PAST) == min(seq, tm)
    cos, sin = _rope_tables(seq, pos0, HEAD_DIM, max(seq, tm))
    n_tab = cos.shape[0] // tm
    state_widths = (B_W, B_W, A_KV_W, A_KV_W)
    keys_transposed = w_k_t is not None
    rows = lambda i: (i, 0)
    tab = lambda i: (i % n_tab, 0)
    in_specs = [pl.BlockSpec((tm, d), rows), _resident(w.shape)]
    args = [xf, w]
    out_specs = [pl.BlockSpec((tm, w.shape[1]), rows)] + [
        pl.BlockSpec((tm, sw), lambda i: (i // kv_period, 0)) for sw in state_widths]
    out_shape = [jax.ShapeDtypeStruct((n, w.shape[1]), BF16)] + [
        jax.ShapeDtypeStruct((n // kv_period, sw), F32) for sw in state_widths]
    if keys_transposed:
        in_specs.append(_resident(w_k_t.shape))
        args.append(w_k_t)
    in_specs += [pl.BlockSpec((tm, LANES), tab), pl.BlockSpec((tm, LANES), tab)]
    args += [cos, sin]
    if keys_transposed:
        assert seq % tm == 0 and tm % kb_block == 0 and tm % ka_block == 0
        tiles = seq // tm
        tab_t = lambda i: (0, i % n_tab)
        in_specs += [pl.BlockSpec((LANES, tm), tab_t), pl.BlockSpec((LANES, tm), tab_t)]
        args += [cos.T, sin.T]
        for width, blk in ((B_W, kb_block), (A_KV_W, ka_block)):
            out_specs.append(pl.BlockSpec((1, tm // blk, width, blk), lambda i: (i // tiles, i % tiles, 0, 0)))
            out_shape.append(jax.ShapeDtypeStruct((n // seq, seq // blk, width, blk), BF16))
    return pl.pallas_call(
        functools.partial(_proj_ab_kernel, kv_period=kv_period, keys_transposed=keys_transposed),
        grid=(n // tm,),
        in_specs=in_specs,
        out_specs=out_specs,
        out_shape=out_shape,
        compiler_params=_params(1),
        name="proj_ab",
    )(*args)


def _rel_bias_kernel(tab_ref, o_ref, *, band):
    h = pl.program_id(0)
    shape = o_ref.shape[1:]
    r = lax.broadcasted_iota(jnp.int32, shape, 0)
    w = lax.broadcasted_iota(jnp.int32, shape, 1)
    idx = jnp.clip(band + r - w, -REL_CLIP, REL_CLIP) + REL_CLIP

    def body(d, acc):
        return jnp.where(idx == d, tab_ref[h, d], acc)

    o_ref[0] = lax.fori_loop(0, 2 * REL_CLIP + 1, body, jnp.zeros(shape, F32))


def _rel_bias(table, band, tq):
    heads = table.shape[0]
    return pl.pallas_call(
        functools.partial(_rel_bias_kernel, band=band),
        grid=(heads,),
        in_specs=[pl.BlockSpec(memory_space=pltpu.SMEM)],
        out_specs=pl.BlockSpec((1, tq, band + tq), lambda h: (h, 0, 0)),
        out_shape=jax.ShapeDtypeStruct((heads, tq, band + tq), F32),
        compiler_params=_params(1),
        name="rel_bias",
    )(table)


def _band_attn_kernel(*refs, heads, kv_heads, band, tq, past, has_bias, has_sinks):
    q_ref, k_ref, v_ref = refs[:3]
    rest = list(refs[3:])
    bias_ref = rest.pop(0) if has_bias else None
    sink_ref = rest.pop(0) if has_sinks else None
    o_ref = rest.pop(0)

    width = band + tq
    scale = HEAD_DIM ** -0.5
    ws = past + pl.program_id(1) * tq - band
    r = lax.broadcasted_iota(jnp.int32, (tq, width), 0) // CHUNK
    w = lax.broadcasted_iota(jnp.int32, (tq, width), 1)
    wc = w // CHUNK
    allowed = (wc >= r) & (wc <= r + band // CHUNK) & (w + ws >= 0)

    def window(ref, cols):
        if past >= band:
            return ref[0, pl.ds(pl.multiple_of(ws, CHUNK), width), cols]
        pieces = [ref[0, pl.ds(pl.multiple_of(jnp.maximum(ws + c * LANES, 0), LANES), LANES), cols]
                  for c in range(width // LANES)]
        return jnp.concatenate(pieces, axis=0)

    group = heads // kv_heads
    outs = []
    for kh in range(kv_heads):
        cols = slice(kh * HEAD_DIM, (kh + 1) * HEAD_DIM)
        k_w = window(k_ref, cols)
        v_w = window(v_ref, cols)
        for g in range(group):
            h = kh * group + g
            hcols = slice(h * HEAD_DIM, (h + 1) * HEAD_DIM)
            s = _dot_t(q_ref[0, :, hcols], k_w) * scale
            if has_bias:
                s = s + bias_ref[h]
            s = jnp.where(allowed, s, NEG_INF)
            m = jnp.max(s, -1, keepdims=True)
            if has_sinks:
                sink = sink_ref[h]
                m = jnp.maximum(m, sink)
            e = jnp.exp(s - m)
            den = jnp.sum(e, -1, keepdims=True)
            if has_sinks:
                den = den + jnp.exp(sink - m)
            p = (e * (1.0 / den)).astype(BF16)
            outs.append(_dot(p, v_w).astype(o_ref.dtype))
    o_ref[0] = jnp.concatenate(outs, axis=1)


def _band_attn(q, k, v, *, heads, kv_heads, band, tq, past, bias=None, sinks=None):
    (qa, qw, qi), (ka, kw, ki), (va, vw, vi) = q, k, v
    b, t, _ = qa.shape
    s_len = ka.shape[1]
    assert t % tq == 0 and s_len == past + t and qw == heads * HEAD_DIM and kw == kv_heads * HEAD_DIM
    assert past >= band or (past == 0 and tq % LANES == 0 and band % LANES == 0)
    in_specs = [
        pl.BlockSpec((1, tq, qw), lambda bi, i: (bi, i, qi)),
        pl.BlockSpec((1, s_len, kw), lambda bi, i: (bi, 0, ki)),
        pl.BlockSpec((1, s_len, vw), lambda bi, i: (bi, 0, vi)),
    ]
    args = [qa, ka, va]
    if bias is not None:
        in_specs.append(_resident(bias.shape))
        args.append(bias)
    if sinks is not None:
        in_specs.append(pl.BlockSpec(memory_space=pltpu.SMEM))
        args.append(sinks)
    return pl.pallas_call(
        functools.partial(_band_attn_kernel, heads=heads, kv_heads=kv_heads, band=band, tq=tq, past=past,
                          has_bias=bias is not None, has_sinks=sinks is not None),
        grid=(b, t // tq),
        in_specs=in_specs,
        out_specs=pl.BlockSpec((1, tq, qw), lambda bi, i: (bi, i, 0)),
        out_shape=jax.ShapeDtypeStruct((b, t, qw), BF16),
        compiler_params=_params(2),
        name="band_attn",
    )(*args)


def _rel_bias_folded_kernel(tab_ref, o_ref, *, band):
    h = pl.program_id(0)
    tq, width = o_ref.shape[1:]
    period = tq + width
    j = lax.broadcasted_iota(jnp.int32, (8, period), 1)
    dist = jnp.where(j < width, band - j, band - (j - period))
    idx = jnp.clip(dist, -REL_CLIP, REL_CLIP) + REL_CLIP

    def body(d, acc):
        return jnp.where(idx == d, tab_ref[h, d], acc)

    g = lax.fori_loop(0, 2 * REL_CLIP + 1, body, jnp.zeros((8, period), F32))
    full = jnp.concatenate([g] * (tq // 8), axis=0)
    bias = pltpu.roll(full, 0, 1, stride=1, stride_axis=0)[:, :width]
    rc = lax.broadcasted_iota(jnp.int32, (tq, width), 0) // CHUNK
    wc = lax.broadcasted_iota(jnp.int32, (tq, width), 1) // CHUNK
    o_ref[0] = jnp.where((wc >= rc) & (wc <= rc + band // CHUNK), bias * LOG2_E, NEG_INF)


def _rel_bias_folded(table, band, tq):
    heads = table.shape[0]
    assert (band + 2 * tq) % LANES == 0 and tq % 8 == 0
    return pl.pallas_call(
        functools.partial(_rel_bias_folded_kernel, band=band),
        grid=(heads,),
        in_specs=[pl.BlockSpec(memory_space=pltpu.SMEM)],
        out_specs=pl.BlockSpec((1, tq, band + tq), lambda h: (h, 0, 0)),
        out_shape=jax.ShapeDtypeStruct((heads, tq, band + tq), F32),
        compiler_params=_params(1),
        name="rel_bias_folded",
    )(table)


def _band_attn_t_kernel(*refs, heads, kv_heads, band, has_bias, has_sinks):
    q_ref, kt_ref, v_ref = refs[:3]
    rest = list(refs[3:])
    bias_ref = rest.pop(0) if has_bias else None
    sink_ref = rest.pop(0) if has_sinks else None
    o_ref = rest.pop(0)
    tq = kt_ref.shape[3]
    n_past = band // tq
    width = band + tq
    group = heads // kv_heads
    static_ok = None
    if not has_bias:
        rc = (lax.broadcasted_iota(jnp.int32, (group * tq, width), 0) % tq) // CHUNK
        wc = lax.broadcasted_iota(jnp.int32, (group * tq, width), 1) // CHUNK
        static_ok = (wc >= rc) & (wc <= rc + band // CHUNK)
    ones = jnp.ones((width, HEAD_DIM), BF16)
    for sub in range(q_ref.shape[1] // tq):
        i = pl.program_id(1) * (q_ref.shape[1] // tq) + sub
        rows = slice(sub * tq, (sub + 1) * tq)
        ws = i * tq - band
        allowed = (lax.broadcasted_iota(jnp.int32, (1, width), 1) + ws) >= 0
        if static_ok is not None:
            allowed = allowed & static_ok
        outs = []
        for kh in range(kv_heads):
            kc = slice(kh * HEAD_DIM, (kh + 1) * HEAD_DIM)
            hs = range(kh * group, (kh + 1) * group)
            q_st = jnp.concatenate([q_ref[0, rows, h * HEAD_DIM:(h + 1) * HEAD_DIM] for h in hs], axis=0)
            k_t = jnp.concatenate([kt_ref[0, jnp.maximum(i - n_past + c, 0), kc, :] for c in range(n_past + 1)],
                                  axis=1)
            s = _dot(q_st, k_t) * (HEAD_DIM ** -0.5 * LOG2_E)
            if has_bias:
                s = s + jnp.concatenate([bias_ref[h] for h in hs], axis=0)
            s = jnp.where(allowed, s, NEG_INF)
            ms, ps = [], []
            for g, h in enumerate(hs):
                s_g = s[g * tq:(g + 1) * tq]
                m_g = jnp.max(s_g, -1, keepdims=True)
                if has_sinks:
                    m_g = jnp.maximum(m_g, sink_ref[h] * LOG2_E)
                ms.append(m_g)
                ps.append(jnp.exp2(s_g - m_g).astype(BF16))
            v_w = jnp.concatenate([v_ref[0, pl.ds(pl.multiple_of(jnp.maximum(ws + c * tq, 0), tq), tq), kc]
                                   for c in range(n_past + 1)], axis=0)
            o_ext = _dot(jnp.concatenate(ps, axis=0), jnp.concatenate([v_w, ones], axis=1))
            for g, h in enumerate(hs):
                o_g = o_ext[g * tq:(g + 1) * tq]
                den = o_g[:, HEAD_DIM:]
                if has_sinks:
                    den = den + jnp.exp2(sink_ref[h] * LOG2_E - ms[g])
                outs.append((o_g[:, :HEAD_DIM] * (1.0 / den)).astype(o_ref.dtype))
        o_ref[0, rows, :] = jnp.concatenate(outs, axis=1)


def _band_attn_t(q, kt, v, *, heads, kv_heads, band, bias=None, sinks=None):
    (qa, qw, qi), (va, vw, vi) = q, v
    b, t, _ = qa.shape
    _, nkb, _, tq = kt.shape
    assert nkb * tq == t and band % tq == 0 and qw == heads * HEAD_DIM and vw == kv_heads * HEAD_DIM
    assert bias is None or heads == kv_heads
    rows = BAND_STEP_BLOCKS * tq if t % (BAND_STEP_BLOCKS * tq) == 0 else tq
    in_specs = [
        pl.BlockSpec((1, rows, qw), lambda bi, i: (bi, i, qi)),
        pl.BlockSpec((1, nkb, kv_heads * HEAD_DIM, tq), lambda bi, i: (bi, 0, 0, 0)),
        pl.BlockSpec((1, t, vw), lambda bi, i: (bi, 0, vi)),
    ]
    args = [qa, kt, va]
    if bias is not None:
        in_specs.append(_resident(bias.shape))
        args.append(bias)
    if sinks is not None:
        in_specs.append(pl.BlockSpec(memory_space=pltpu.SMEM))
        args.append(sinks)
    return pl.pallas_call(
        functools.partial(_band_attn_t_kernel, heads=heads, kv_heads=kv_heads, band=band,
                          has_bias=bias is not None, has_sinks=sinks is not None),
        grid=(b, t // rows),
        in_specs=in_specs,
        out_specs=pl.BlockSpec((1, rows, qw), lambda bi, i: (bi, i, 0)),
        out_shape=jax.ShapeDtypeStruct((b, t, qw), BF16),
        compiler_params=_params(2),
        name="band_attn_t",
    )(*args)


def _outproj_ln_kernel(*refs, n_in):
    a_refs = refs[:n_in]
    w_ref, x_ref, g_ref, b_ref, o_ref = refs[n_in:]
    d_out = o_ref.shape[1]
    for j in range(d_out // COL_CHUNK):
        cols = slice(j * COL_CHUNK, (j + 1) * COL_CHUNK)
        y = DEEPNORM_ALPHA * x_ref[:, cols]
        r0 = 0
        for a_ref in a_refs:
            kk = a_ref.shape[1]
            y = y + _dot(a_ref[...], w_ref[r0:r0 + kk, cols])
            r0 += kk
        o_ref[:, cols] = y
    o_ref[...] = _layer_norm(o_ref[...], g_ref[...], b_ref[...])


def _outproj_ln(a_list, w, xf, g, b):
    n, d = xf.shape
    tm = _row_tile(n)
    assert sum(a.shape[1] for a in a_list) == w.shape[0]
    return pl.pallas_call(
        functools.partial(_outproj_ln_kernel, n_in=len(a_list)),
        grid=(n // tm,),
        in_specs=[pl.BlockSpec((tm, a.shape[1]), lambda i: (i, 0)) for a in a_list] + [
            _resident(w.shape),
            pl.BlockSpec((tm, d), lambda i: (i, 0)),
            _resident(g.shape),
            _resident(b.shape),
        ],
        out_specs=pl.BlockSpec((tm, d), lambda i: (i, 0)),
        out_shape=jax.ShapeDtypeStruct((n, d), F32),
        compiler_params=_params(1),
        name="outproj_ln",
    )(*a_list, w, xf, g, b)


def _mlp_ln_kernel(x_ref, wu_ref, wd_ref, g_ref, b_ref, o_ref, xb_ref):
    f = pl.program_id(1)

    @pl.when(f == 0)
    def _():
        xb_ref[...] = x_ref[...].astype(BF16)
        o_ref[...] = jnp.zeros(o_ref.shape, o_ref.dtype)

    hid = _dot(xb_ref[...], wu_ref[...])
    hid = jnp.square(jnp.maximum(hid, 0.0)).astype(BF16)
    for j in range(o_ref.shape[1] // COL_CHUNK):
        cols = slice(j * COL_CHUNK, (j + 1) * COL_CHUNK)
        o_ref[:, cols] += _dot(hid, wd_ref[:, cols])

    @pl.when(f == pl.num_programs(1) - 1)
    def _():
        o_ref[...] = _layer_norm(DEEPNORM_ALPHA * x_ref[...] + o_ref[...], g_ref[...], b_ref[...])


def _mlp_ln(xf, layer, w_up, w_down, g, b):
    n, d = xf.shape
    d_ff = w_up.shape[2]
    tm = MLP_ROW_TILE if n % MLP_ROW_TILE == 0 else n
    tf = MLP_FF_TILE
    return pl.pallas_call(
        _mlp_ln_kernel,
        grid=(n // tm, d_ff // tf),
        in_specs=[
            pl.BlockSpec((tm, d), lambda i, f: (i, 0)),
            pl.BlockSpec((None, d, tf), lambda i, f: (layer, 0, f)),
            pl.BlockSpec((None, tf, d), lambda i, f: (layer, f, 0)),
            _resident(g.shape),
            _resident(b.shape),
        ],
        out_specs=pl.BlockSpec((tm, d), lambda i, f: (i, 0)),
        out_shape=jax.ShapeDtypeStruct((n, d), F32),
        scratch_shapes=[pltpu.VMEM((tm, d), BF16)],
        compiler_params=_params(2),
        name="mlp_ln",
    )(xf, w_up, w_down, g, b)


def _ple_kernel(x_ref, p_ref, wg_ref, bg_ref, wp_ref, o_ref):
    xb = x_ref[...].astype(BF16)
    pb = p_ref[...].astype(BF16)
    for j in range(o_ref.shape[1] // COL_CHUNK):
        cols = slice(j * COL_CHUNK, (j + 1) * COL_CHUNK)
        gate = jax.nn.sigmoid(_dot(xb, wg_ref[:, cols]) + bg_ref[:, cols])
        o_ref[:, cols] = x_ref[:, cols] + gate * _dot(pb, wp_ref[:, cols])


def _ple(xf, p_all, layer, wg, bg, wp):
    n, d = xf.shape
    tm = _row_tile(n)
    return pl.pallas_call(
        _ple_kernel,
        grid=(n // tm,),
        in_specs=[
            pl.BlockSpec((tm, d), lambda i: (i, 0)),
            pl.BlockSpec((None, tm, p_all.shape[2]), lambda i: (layer, i, 0)),
            _resident_slab(wg.shape, layer),
            _resident(bg.shape),
            _resident_slab(wp.shape, layer),
        ],
        out_specs=pl.BlockSpec((tm, d), lambda i: (i, 0)),
        out_shape=jax.ShapeDtypeStruct((n, d), F32),
        compiler_params=_params(1),
        name="ple",
    )(xf, p_all, wg, bg, wp)


Q_GROUP = 4


def _proj_cq_kernel(x_ref, w_ref, gq_ref, gkv_ref, cos_ref, sin_ref, wq_ref, q_ref, ckv_ref, kr_ref, krp_ref):
    cos = cos_ref[...]
    sin = sin_ref[...]
    h = _dot(x_ref[...].astype(BF16), w_ref[...])
    cq = _rms_norm(h[:, :C_Q_RANK], gq_ref[...]).astype(BF16)
    ckv_ref[...] = _rms_norm(h[:, C_Q_RANK:C_Q_RANK + C_KV_RANK], gkv_ref[...])
    rot = _rope_tile(h[:, C_Q_RANK + C_KV_RANK:], cos, sin, C_ROPE)
    lane = lax.broadcasted_iota(jnp.int32, rot.shape, 1)
    rot = jnp.where(lane < C_ROPE, rot, 0.0)
    kr_ref[...] = rot[:, :C_ROPE]
    krp_ref[...] = rot.astype(krp_ref.dtype)

    grp_w = Q_GROUP * (C_NOPE + C_ROPE)
    for gi in range(C_HEADS // Q_GROUP):
        acc = _dot(cq, wq_ref[:, gi * grp_w:(gi + 1) * grp_w]) * (MLA_SCALE * LOG2_E)
        for u in range(Q_GROUP):
            o0 = (gi * Q_GROUP + u) * C_QK
            q_ref[:, o0:o0 + C_NOPE] = acc[:, u * C_NOPE:(u + 1) * C_NOPE].astype(q_ref.dtype)
            if u % 2 == 0:
                r0 = Q_GROUP * C_NOPE + (u // 2) * LANES
                qrot = _rope_tile(acc[:, r0:r0 + LANES], cos, sin, C_ROPE)
                piece = qrot
            else:
                piece = pltpu.roll(qrot, C_ROPE, 1)
            q_ref[:, o0 + C_NOPE:o0 + C_QK] = jnp.where(lane < C_ROPE, piece, 0.0).astype(q_ref.dtype)


def _proj_cq(xf, w, gq, gkv, cos, sin, wq):
    n, d = xf.shape
    tm = _row_tile(n)
    n_tab = cos.shape[0] // tm
    rows = lambda i: (i, 0)
    tab = lambda i: (i % n_tab, 0)
    return pl.pallas_call(
        _proj_cq_kernel,
        grid=(n // tm,),
        in_specs=[
            pl.BlockSpec((tm, d), rows),
            _resident(w.shape),
            _resident(gq.shape),
            _resident(gkv.shape),
            pl.BlockSpec((tm, LANES), tab),
            pl.BlockSpec((tm, LANES), tab),
            _resident(wq.shape),
        ],
        out_specs=[
            pl.BlockSpec((tm, C_HEADS * C_QK), rows),
            pl.BlockSpec((tm, C_KV_RANK), rows),
            pl.BlockSpec((tm, C_ROPE), rows),
            pl.BlockSpec((tm, LANES), rows),
        ],
        out_shape=[
            jax.ShapeDtypeStruct((n, C_HEADS * C_QK), BF16),
            jax.ShapeDtypeStruct((n, C_KV_RANK), F32),
            jax.ShapeDtypeStruct((n, C_ROPE), F32),
            jax.ShapeDtypeStruct((n, LANES), BF16),
        ],
        compiler_params=_params(1),
        name="proj_cq",
    )(xf, w, gq, gkv, cos, sin, wq)


def _kv_c_kernel(ckv_ref, krp_ref, wkt_ref, wv_ref, kt_ref, krt_ref, v_ref):
    cb = ckv_ref[...].astype(BF16)
    eye = (lax.broadcasted_iota(jnp.int32, (LANES, LANES), 0)
           == lax.broadcasted_iota(jnp.int32, (LANES, LANES), 1)).astype(F32).astype(BF16)
    krt_ref[0, 0] = _dot_t(eye, krp_ref[...]).astype(krt_ref.dtype)
    kt_ref[0, 0] = _dot_t(wkt_ref[...], cb).astype(kt_ref.dtype)
    for j in range(C_HEADS * C_V // COL_CHUNK):
        cols = slice(j * COL_CHUNK, (j + 1) * COL_CHUNK)
        v_ref[:, cols] = _dot(cb, wv_ref[:, cols]).astype(v_ref.dtype)


def _kv_c(ckv, krp, wkt, wv, batch, tk):
    n = ckv.shape[0]
    nkb = n // batch // tk
    assert n == batch * nkb * tk
    rows = lambda i: (i, 0)
    blk = lambda i: (i // nkb, i % nkb, 0, 0)
    return pl.pallas_call(
        _kv_c_kernel,
        grid=(n // tk,),
        in_specs=[pl.BlockSpec((tk, C_KV_RANK), rows), pl.BlockSpec((tk, LANES), rows), _resident(wkt.shape),
                  _resident(wv.shape)],
        out_specs=[pl.BlockSpec((1, 1, C_HEADS * C_NOPE, tk), blk), pl.BlockSpec((1, 1, LANES, tk), blk),
                   pl.BlockSpec((tk, C_HEADS * C_V), rows)],
        out_shape=[
            jax.ShapeDtypeStruct((batch, nkb, C_HEADS * C_NOPE, tk), BF16),
            jax.ShapeDtypeStruct((batch, nkb, LANES, tk), BF16),
            jax.ShapeDtypeStruct((n, C_HEADS * C_V), BF16),
        ],
        compiler_params=_params(1),
        name="kv_c",
    )(ckv, krp, wkt, wv)


MLA_LONG_BLOCK = 512


def _mla_attn_kernel(q_ref, kt_ref, krt_ref, v_ref, o_ref, *, pos0, heads):
    t = q_ref.shape[1]
    tq = tk = kt_ref.shape[3]
    row = lax.broadcasted_iota(jnp.int32, (tq, tk), 0) // CHUNK
    col = lax.broadcasted_iota(jnp.int32, (tq, tk), 1) // CHUNK
    diag_ok = col <= row
    ones = jnp.ones((tk, C_V), BF16)

    def k_slab(g, kb):
        return jnp.concatenate([kt_ref[0, kb, g * C_NOPE:(g + 1) * C_NOPE, :], krt_ref[0, kb]], axis=0)

    def v_slab(g, kb):
        return jnp.concatenate([v_ref[0, kb * tk:(kb + 1) * tk, g * C_V:(g + 1) * C_V], ones], axis=1)

    def step(s, m, acc, v_ext):
        m_new = jnp.maximum(m, jnp.max(s, -1, keepdims=True))
        p = jnp.exp2(s - m_new).astype(BF16)
        return m_new, jnp.exp2(m - m_new) * acc + _dot(p, v_ext)

    def finish(acc):
        return (acc[:, :C_V] * (1.0 / acc[:, C_V:])).astype(o_ref.dtype)

    for qi in range(t // tq):
        q0 = qi * tq
        qs = [q_ref[0, q0:q0 + tq, g * C_QK:(g + 1) * C_QK] for g in range(heads)]
        n_full = (pos0 + q0) // tk
        state = [(jnp.full((tq, 1), NEG_INF, F32), jnp.zeros((tq, 2 * C_V), F32)) for _ in range(heads)]
        for kb in range(n_full):
            raw = [_dot(qs[g], k_slab(g, kb)) for g in range(heads)]
            state = [step(raw[g], *state[g], v_slab(g, kb)) for g in range(heads)]
        raw = [jnp.where(diag_ok, _dot(qs[g], k_slab(g, n_full)), NEG_INF) for g in range(heads)]
        state = [step(raw[g], *state[g], v_slab(g, n_full)) for g in range(heads)]
        for g in range(heads):
            o_ref[0, q0:q0 + tq, g * C_V:(g + 1) * C_V] = finish(state[g][1])


def _mla_attn(q, kt, krt, v, *, pos0, heads=4):
    b, t, _ = q.shape
    _, nkb, _, tk = kt.shape
    s_len = nkb * tk
    assert t % tk == 0 and pos0 % tk == 0 and pos0 + t <= s_len and v.shape[1] == s_len
    return pl.pallas_call(
        functools.partial(_mla_attn_kernel, pos0=pos0, heads=heads),
        grid=(b, C_HEADS // heads),
        in_specs=[
            pl.BlockSpec((1, t, heads * C_QK), lambda bi, h: (bi, 0, h)),
            pl.BlockSpec((1, nkb, heads * C_NOPE, tk), lambda bi, h: (bi, 0, h, 0)),
            pl.BlockSpec((1, nkb, LANES, tk), lambda bi, h: (bi, 0, 0, 0)),
            pl.BlockSpec((1, s_len, heads * C_V), lambda bi, h: (bi, 0, h)),
        ],
        out_specs=pl.BlockSpec((1, t, heads * C_V), lambda bi, h: (bi, 0, h)),
        out_shape=jax.ShapeDtypeStruct((b, t, C_HEADS * C_V), BF16),
        compiler_params=_params(2),
        name="mla_attn",
    )(q, kt, krt, v)


def _mla_absorbed_kernel(q_ref, ckv_new_ref, kr_new_ref, ckv_past_ref, kr_past_ref, wkt_ref, wv_ref, o_ref):
    t = q_ref.shape[1]
    q_lat, q_rope = [], []
    for h in range(C_HEADS):
        q_lat.append(_dot(q_ref[0, :, h * C_QK:h * C_QK + C_NOPE], wkt_ref[h * C_NOPE:(h + 1) * C_NOPE, :]))
        q_rope.append(q_ref[0, :, h * C_QK + C_NOPE:h * C_QK + C_NOPE + C_ROPE])
    q_lat = jnp.concatenate(q_lat, axis=0).astype(BF16)
    q_rope = jnp.concatenate(q_rope, axis=0)
    lat = [ckv_past_ref[...].astype(BF16), ckv_new_ref[0].astype(BF16)]
    rot = [kr_past_ref[...].astype(BF16), kr_new_ref[0].astype(BF16)]
    s = jnp.concatenate([_dot_t(q_lat, c) + _dot_t(q_rope, r) for c, r in zip(lat, rot)], axis=1)
    e = jnp.exp2(s - jnp.max(s, -1, keepdims=True))
    p = (e * (1.0 / jnp.sum(e, -1, keepdims=True))).astype(BF16)
    n_past = lat[0].shape[0]
    o_lat = (_dot(p[:, :n_past], lat[0]) + _dot(p[:, n_past:], lat[1])).astype(BF16)
    o_ref[0] = jnp.concatenate([_dot(o_lat[h * t:(h + 1) * t], wv_ref[:, h * C_V:(h + 1) * C_V])
                                for h in range(C_HEADS)], axis=1).astype(o_ref.dtype)


def _mla_absorbed(q, ckv_new, kr_new, ckv_past, kr_past, wkt, wv, *, pos0):
    b, t, _ = q.shape
    n_past = ckv_past.shape[2]
    assert t == CHUNK and pos0 % CHUNK == 0 and n_past <= pos0
    return pl.pallas_call(
        _mla_absorbed_kernel,
        grid=(b,),
        in_specs=[
            pl.BlockSpec((1, t, C_HEADS * C_QK), lambda bi: (bi, 0, 0)),
            pl.BlockSpec((1, t, C_KV_RANK), lambda bi: (bi, 0, 0)),
            pl.BlockSpec((1, t, C_ROPE), lambda bi: (bi, 0, 0)),
            pl.BlockSpec((None, None, n_past, C_KV_RANK), lambda bi: (0, bi, 0, 0)),
            pl.BlockSpec((None, None, n_past, C_ROPE), lambda bi: (0, bi, 0, 0)),
            _resident(wkt.shape),
            _resident(wv.shape),
        ],
        out_specs=pl.BlockSpec((1, t, C_HEADS * C_V), lambda bi: (bi, 0, 0)),
        out_shape=jax.ShapeDtypeStruct((b, t, C_HEADS * C_V), BF16),
        compiler_params=_params(1),
        name="mla_absorbed",
    )(q, ckv_new, kr_new, ckv_past, kr_past, wkt, wv)


def _prepare_weights(w):
    o1 = A_Q_W
    o2 = o1 + A_KV_W
    o3 = o2 + A_KV_W
    o4 = o3 + B_W
    o5 = o4 + B_W
    w_ab = w['w_in_ab'][0]
    w_ab = jnp.concatenate([w_ab[:, :o1], w_ab[:, o3:o4], w_ab[:, o4:o5], w_ab[:, o5:], w_ab[:, o1:o2],
                            w_ab[:, o2:o3]], axis=1)
    ka0 = AB_KV_COL0 + 2 * B_W
    w_ab_long = jnp.concatenate([w_ab[:, :AB_KV_COL0], w_ab[:, AB_KV_COL0 + B_W:ka0], w_ab[:, ka0 + A_KV_W:]], axis=1)
    w_k_t_ab = jnp.concatenate([w_ab[:, AB_KV_COL0:AB_KV_COL0 + B_W], w_ab[:, ka0:ka0 + A_KV_W]], axis=1).T
    w_c = jnp.pad(w['w_in_c'][0], ((0, 0), (0, LANES - C_ROPE)))
    hq = C_NOPE + C_ROPE
    q_cols = []
    for g0 in range(0, C_HEADS, Q_GROUP):
        q_cols += [jnp.arange(h * hq, h * hq + C_NOPE) for h in range(g0, g0 + Q_GROUP)]
        q_cols += [jnp.arange(h * hq + C_NOPE, (h + 1) * hq) for h in range(g0, g0 + Q_GROUP)]
    w_q = w['w_q_b_c'][0][:, jnp.concatenate(q_cols)]
    hkv = C_NOPE + C_V
    k_cols = jnp.concatenate([jnp.arange(h * hkv, h * hkv + C_NOPE) for h in range(C_HEADS)])
    v_cols = jnp.concatenate([jnp.arange(h * hkv + C_NOPE, (h + 1) * hkv) for h in range(C_HEADS)])
    w_k_t = w['w_kv_b_c'][0][:, k_cols].T
    w_v = w['w_kv_b_c'][0][:, v_cols]
    row = lambda a: a.reshape(1, -1)
    return {
        'w_in_ab': w_ab.astype(BF16), 'w_in_ab_long': w_ab_long.astype(BF16), 'w_k_t_ab': w_k_t_ab.astype(BF16),
        'w_out_ab': w['w_out_ab'][0].astype(BF16),
        'w_in_c': w_c.astype(BF16), 'w_q_b_c': w_q.astype(BF16), 'w_k_t_c': w_k_t.astype(BF16),
        'w_v_c': w_v.astype(BF16),
        'w_out_c': w['w_out_c'][0].astype(BF16),
        'g_q_c': row(w['g_q_c'][0]), 'g_kv_c': row(w['g_kv_c'][0]),
        'sinks_a': w['sinks_a'][0], 'rel_bias_b': w['rel_bias_b'][0],
        'ln1_g': [row(w['ln1_g'][i]) for i in range(DEPTH)], 'ln1_b': [row(w['ln1_b'][i]) for i in range(DEPTH)],
        'ln2_g': [row(w['ln2_g'][i]) for i in range(DEPTH)], 'ln2_b': [row(w['ln2_b'][i]) for i in range(DEPTH)],
        'w_mlp_up': w['w_mlp_up'].astype(BF16), 'w_mlp_down': w['w_mlp_down'].astype(BF16),
        'w_ple_gate': w['w_ple_gate'].astype(BF16), 'w_ple': w['w_ple'].astype(BF16),
        'b_ple_gate': [row(w['b_ple_gate'][i]) for i in range(DEPTH)],
    }


def _channel_mix(xf, p_all, pw, i):
    xf = _mlp_ln(xf, i, pw['w_mlp_up'], pw['w_mlp_down'], pw['ln2_g'][i], pw['ln2_b'][i])
    return _ple(xf, p_all, i, pw['w_ple_gate'], pw['b_ple_gate'][i], pw['w_ple'])


def _trunk(x, p, pos0, past, pw):
    b, t, d = x.shape
    n = b * t
    tm = _row_tile(n)
    xf = x.reshape(n, d)

    if past is None:
        h, kb_new, vb_new, ka_new, va_new, kb_t, ka_t = _proj_ab(
            xf, pw['w_in_ab_long'], t, pos0, pw['w_k_t_ab'], BAND_B_BLOCK, BAND_A_BLOCK)
    else:
        h, kb_new, vb_new, ka_new, va_new = _proj_ab(xf, pw['w_in_ab'], t, pos0)
    h3 = h.reshape(b, t, h.shape[1])
    kb_new, vb_new, ka_new, va_new = [a.reshape(b, a.shape[0] // b, a.shape[1])
                                      for a in (kb_new, vb_new, ka_new, va_new)]
    q_a = (h3, A_Q_W, 0)
    q_b = (h3, B_W, 1)
    if past is None:
        attn_a = _band_attn_t(q_a, ka_t, (h3, A_KV_W, (AB_KV_COL0 + B_W) // A_KV_W), heads=A_HEADS,
                              kv_heads=A_KV_HEADS, band=WINDOW, sinks=pw['sinks_a'])
        bias_b = _rel_bias_folded(pw['rel_bias_b'], B_BAND_PAST, BAND_B_BLOCK)
        attn_b = _band_attn_t(q_b, kb_t, (h3, B_W, 2), heads=B_HEADS, kv_heads=B_HEADS, band=B_BAND_PAST,
                              bias=bias_b)
        ak, av, bk, bv = ka_new[:, -WINDOW:], va_new[:, -WINDOW:], kb_new, vb_new
    else:
        n_past_a, n_past_b = past[0].shape[1], past[2].shape[1]
        news = (ka_new, va_new, kb_new, vb_new)
        caps = (WINDOW, WINDOW, B_BAND_PAST, B_BAND_PAST)
        ak, av, bk, bv = [
            jnp.concatenate([c, new.reshape(b, t, c.shape[2], HEAD_DIM)], axis=1)[:, -min(cap, c.shape[1] + t):]
            for c, new, cap in zip(past[:4], news, caps)]
        k_a, v_a, k_b, v_b = [
            (jnp.concatenate([c.reshape(b, c.shape[1], -1).astype(BF16), new.astype(BF16)], axis=1), new.shape[2], 0)
            for c, new in zip(past[:4], news)]
        attn_a = _band_attn(q_a, k_a, v_a, heads=A_HEADS, kv_heads=A_KV_HEADS, band=WINDOW, tq=CHUNK,
                            past=n_past_a, sinks=pw['sinks_a'])
        bias_b = _rel_bias(pw['rel_bias_b'], B_BAND_PAST, CHUNK)
        attn_b = _band_attn(q_b, k_b, v_b, heads=B_HEADS, kv_heads=B_HEADS, band=B_BAND_PAST, tq=CHUNK,
                            past=n_past_b, bias=bias_b)
    xf = _outproj_ln([attn_a.reshape(n, A_Q_W), attn_b.reshape(n, B_W)], pw['w_out_ab'], xf,
                     pw['ln1_g'][0], pw['ln1_b'][0])
    p_all = p.reshape(p.shape[0], n, p.shape[3])
    xf = _channel_mix(xf, p_all, pw, 0)

    cos_c, sin_c = _rope_tables(t, pos0, C_ROPE, max(t, tm))
    q, ckv, kr, krp = _proj_cq(xf, pw['w_in_c'], pw['g_q_c'], pw['g_kv_c'], cos_c, sin_c, pw['w_q_b_c'])
    q = q.reshape(b, t, C_HEADS * C_QK)
    if past is None:
        kt_c, krt_c, v_c = _kv_c(ckv, krp, pw['w_k_t_c'], pw['w_v_c'], b, min(MLA_LONG_BLOCK, t))
        attn_c = _mla_attn(q, kt_c, krt_c, v_c.reshape(b, t, C_HEADS * C_V), pos0=pos0)
    else:
        attn_c = _mla_absorbed(q, ckv.reshape(b, t, C_KV_RANK), kr.reshape(b, t, C_ROPE), past[4], past[5],
                               pw['w_k_t_c'], pw['w_v_c'], pos0=pos0)
    xf = _outproj_ln([attn_c.reshape(n, C_HEADS * C_V)], pw['w_out_c'], xf, pw['ln1_g'][1], pw['ln1_b'][1])
    xf = _channel_mix(xf, p_all, pw, 1)

    heads4 = lambda a, hh: a.reshape(1, b, a.shape[1], hh, HEAD_DIM)
    return (xf.reshape(b, t, d), heads4(ak, A_KV_HEADS), heads4(av, A_KV_HEADS), heads4(bk, B_HEADS),
            heads4(bv, B_HEADS), ckv.reshape(1, b, t, C_KV_RANK), kr.reshape(1, b, t, C_ROPE))


def kernel(x_prompt, x_sample, cache_a_k, cache_a_v, cache_b_k, cache_b_v, cache_c_kv, cache_c_krope, p_prompt,
           p_sample, w_in_ab, sinks_a, rel_bias_b, w_out_ab, w_in_c, g_q_c, w_q_b_c, g_kv_c, w_kv_b_c, w_out_c,
           ln1_g, ln1_b, ln2_g, ln2_b, w_mlp_up, w_mlp_down, w_ple_gate, b_ple_gate, w_ple):
    pw = _prepare_weights({
        'w_in_ab': w_in_ab, 'sinks_a': sinks_a, 'rel_bias_b': rel_bias_b, 'w_out_ab': w_out_ab,
        'w_in_c': w_in_c, 'g_q_c': g_q_c, 'w_q_b_c': w_q_b_c, 'g_kv_c': g_kv_c, 'w_kv_b_c': w_kv_b_c,
        'w_out_c': w_out_c, 'ln1_g': ln1_g, 'ln1_b': ln1_b, 'ln2_g': ln2_g, 'ln2_b': ln2_b,
        'w_mlp_up': w_mlp_up, 'w_mlp_down': w_mlp_down, 'w_ple_gate': w_ple_gate, 'b_ple_gate': b_ple_gate,
        'w_ple': w_ple,
    })
    prompt = _trunk(x_prompt, p_prompt, 0, None, pw)
    past = (cache_a_k[0], cache_a_v[0], cache_b_k[0], cache_b_v[0], cache_c_kv, cache_c_krope)
    sample = _trunk(x_sample, p_sample, cache_c_kv.shape[2], past, pw)
    return (prompt[0], sample[0]) + prompt[1:] + sample[1:]
```

```python
import functools

import jax
import jax.numpy as jnp
from jax import lax
from jax.experimental import pallas as pl
from jax.experimental.pallas import tpu as pltpu

F32 = jnp.float32
BF16 = jnp.bfloat16

CHUNK = 64
HEAD_DIM = 128
A_HEADS = 8
A_KV_HEADS = 2
WINDOW = 128
B_HEADS = 8
B_BAND_PAST = 512
REL_CLIP = 128
C_HEADS = 16
C_Q_RANK = 768
C_KV_RANK = 512
C_NOPE = 128
C_ROPE = 64
C_V = 128
DEPTH = 2
ROPE_THETA = 10000.0
LN_EPS = 1e-5
RMS_EPS = 1e-6
NEG_INF = -1e30
DEEPNORM_ALPHA = (2 * DEPTH) ** 0.25
LOG2_E = 1.4426950408889634
MLA_SCALE = (C_NOPE + C_ROPE) ** -0.5

A_Q_W = A_HEADS * HEAD_DIM
A_KV_W = A_KV_HEADS * HEAD_DIM
B_W = B_HEADS * HEAD_DIM
AB_IN_W = A_Q_W + 2 * A_KV_W + 3 * B_W
AB_KV_COL0 = A_Q_W + B_W
AB_KV_W = AB_IN_W - AB_KV_COL0
C_IN_W = C_Q_RANK + C_KV_RANK + C_ROPE
C_QK = 256

LANES = 128
V7X_VMEM_BYTES = 64 * 1024 * 1024
VMEM_LIMIT = V7X_VMEM_BYTES - 8 * 1024 * 1024

ROW_TILE = 512
COL_CHUNK = 512
MLP_ROW_TILE = 512
MLP_FF_TILE = 1024
BAND_B_BLOCK = 256
BAND_A_BLOCK = 128
BAND_STEP_BLOCKS = 4


def _params(n_axes):
    return pltpu.CompilerParams(dimension_semantics=("arbitrary",) * n_axes, vmem_limit_bytes=VMEM_LIMIT)


def _resident(shape):
    nd = len(shape)
    return pl.BlockSpec(shape, lambda *_: (0,) * nd, pipeline_mode=pl.Buffered(1))


def _resident_slab(shape, index):
    nd = len(shape) - 1
    return pl.BlockSpec((None,) + tuple(shape[1:]), lambda *_: (index,) + (0,) * nd, pipeline_mode=pl.Buffered(1))


def _row_tile(n):
    return ROW_TILE if n % ROW_TILE == 0 else n


def _dot(a, b):
    return jnp.dot(a, b, preferred_element_type=F32)


def _dot_t(a, b):
    return lax.dot_general(a, b, (((1,), (1,)), ((), ())), preferred_element_type=F32)


def _layer_norm(y, g, b):
    mu = jnp.mean(y, -1, keepdims=True)
    var = jnp.mean(jnp.square(y - mu), -1, keepdims=True)
    return (y - mu) * lax.rsqrt(var + LN_EPS) * g + b


def _rms_norm(y, g):
    return y * lax.rsqrt(jnp.mean(jnp.square(y), -1, keepdims=True) + RMS_EPS) * g


def _rope_tile(t, cos, sin, d):
    if d == LANES:
        swapped = pltpu.roll(t, LANES // 2, 1)
    else:
        lane = lax.broadcasted_iota(jnp.int32, t.shape, 1)
        swapped = jnp.where((lane % d) < d // 2, pltpu.roll(t, LANES - d // 2, 1), pltpu.roll(t, d // 2, 1))
    return t * cos + swapped * sin


def _rope_tables(t, pos0, d, rows):
    half = d // 2
    inv = ROPE_THETA ** (-jnp.arange(half, dtype=F32) * (2.0 / d))
    ang = (jnp.arange(t, dtype=F32) + pos0)[:, None] * inv[None, :]
    cos = jnp.cos(ang)
    sin = jnp.sin(ang)
    reps = (rows // t, LANES // d)
    return jnp.tile(jnp.concatenate([cos, cos], 1), reps), jnp.tile(jnp.concatenate([-sin, sin], 1), reps)


def _proj_ab_kernel(*refs, kv_period, keys_transposed):
    if keys_transposed:
        (x_ref, w_ref, wkt_ref, cos_ref, sin_ref, cost_ref, sint_ref,
         h_ref, kb_ref, vb_ref, ka_ref, va_ref, kbt_ref, kat_ref) = refs
    else:
        x_ref, w_ref, cos_ref, sin_ref, h_ref, kb_ref, vb_ref, ka_ref, va_ref = refs
    xb = x_ref[...].astype(BF16)
    cos = cos_ref[...]
    sin = sin_ref[...]
    keep_state = (pl.program_id(0) % kv_period) == kv_period - 1
    kv0 = AB_KV_COL0
    rope_tiles = set(range(A_HEADS))
    if keys_transposed:
        state_refs = [(kv0, vb_ref), (kv0 + B_W, va_ref)]
    else:
        state_refs = [(kv0, kb_ref), (kv0 + B_W, vb_ref), (kv0 + 2 * B_W, ka_ref), (kv0 + 2 * B_W + A_KV_W, va_ref)]
        rope_tiles |= {(kv0 + 2 * B_W) // LANES + u for u in range(A_KV_HEADS)}
    if keys_transposed:
        k_t = _dot_t(wkt_ref[...], xb)
        kb_t, ka_t = k_t[:B_W], k_t[B_W:]
        cos_t = cost_ref[...]
        sin_t = sint_ref[...]
        half = HEAD_DIM // 2
        rotated = []
        for u in range(A_KV_HEADS):
            t = ka_t[u * HEAD_DIM:(u + 1) * HEAD_DIM]
            rotated.append(t * cos_t + jnp.concatenate([t[half:], t[:half]], axis=0) * sin_t)
        ka_t = jnp.concatenate(rotated, axis=0)
        for t_val, t_ref in ((kb_t, kbt_ref), (ka_t, kat_ref)):
            blk = t_ref.shape[3]
            for c in range(t_ref.shape[1]):
                t_ref[0, c] = t_val[:, c * blk:(c + 1) * blk].astype(t_ref.dtype)
    width = w_ref.shape[1]
    for c0 in range(0, width, COL_CHUNK):
        cw = min(COL_CHUNK, width - c0)
        acc = _dot(xb, w_ref[:, c0:c0 + cw])
        parts = []
        for u in range(cw // LANES):
            part = acc[:, u * LANES:(u + 1) * LANES]
            if c0 // LANES + u in rope_tiles:
                part = _rope_tile(part, cos, sin, HEAD_DIM)
            parts.append(part)
        acc = jnp.concatenate(parts, axis=1)
        h_ref[:, c0:c0 + cw] = acc.astype(h_ref.dtype)
        for s0, ref in state_refs:
            lo, hi = max(c0, s0), min(c0 + cw, s0 + ref.shape[1])
            if lo < hi:
                @pl.when(keep_state)
                def _(acc=acc, ref=ref, lo=lo, hi=hi, s0=s0, c0=c0):
                    ref[:, lo - s0:hi - s0] = acc[:, lo - c0:hi - c0]
    if keys_transposed:
        @pl.when(keep_state)
        def _():
            kb_ref[...] = kb_t.T
            ka_ref[...] = ka_t.T


def _proj_ab(xf, w, seq, pos0, w_k_t=None, kb_block=None, ka_block=None):
    n, d = xf.shape
    tm = _row_tile(n)
    assert seq % tm == 0 or tm % seq == 0
    kv_period = max(seq // tm, 1)
    assert min(seq, B_BAND_PAST) == min(seq, tm)
    cos, sin = _rope_tables(seq, pos0, HEAD_DIM, max(seq, tm))
    n_tab = cos.shape[0] // tm
    state_widths = (B_W, B_W, A_KV_W, A_KV_W)
    keys_transposed = w_k_t is not None
    rows = lambda i: (i, 0)
    tab = lambda i: (i % n_tab, 0)
    in_specs = [pl.BlockSpec((tm, d), rows), _resident(w.shape)]
    args = [xf, w]
    out_specs = [pl.BlockSpec((tm, w.shape[1]), rows)] + [
        pl.BlockSpec((tm, sw), lambda i: (i // kv_period, 0)) for sw in state_widths]
    out_shape = [jax.ShapeDtypeStruct((n, w.shape[1]), BF16)] + [
        jax.ShapeDtypeStruct((n // kv_period, sw), F32) for sw in state_widths]
    if keys_transposed:
        in_specs.append(_resident(w_k_t.shape))
        args.append(w_k_t)
    in_specs += [pl.BlockSpec((tm, LANES), tab), pl.BlockSpec((tm, LANES), tab)]
    args += [cos, sin]
    if keys_transposed:
        assert seq % tm == 0 and tm % kb_block == 0 and tm % ka_block == 0
        tiles = seq // tm
        tab_t = lambda i: (0, i % n_tab)
        in_specs += [pl.BlockSpec((LANES, tm), tab_t), pl.BlockSpec((LANES, tm), tab_t)]
        args += [cos.T, sin.T]
        for width, blk in ((B_W, kb_block), (A_KV_W, ka_block)):
            out_specs.append(pl.BlockSpec((1, tm // blk, width, blk), lambda i: (i // tiles, i % tiles, 0, 0)))
            out_shape.append(jax.ShapeDtypeStruct((n // seq, seq // blk, width, blk), BF16))
    return pl.pallas_call(
        functools.partial(_proj_ab_kernel, kv_period=kv_period, keys_transposed=keys_transposed),
        grid=(n // tm,),
        in_specs=in_specs,
        out_specs=out_specs,
        out_shape=out_shape,
        compiler_params=_params(1),
        name="proj_ab",
    )(*args)


def _rel_bias_kernel(tab_ref, o_ref, *, band):
    h = pl.program_id(0)
    shape = o_ref.shape[1:]
    r = lax.broadcasted_iota(jnp.int32, shape, 0)
    w = lax.broadcasted_iota(jnp.int32, shape, 1)
    idx = jnp.clip(band + r - w, -REL_CLIP, REL_CLIP) + REL_CLIP

    def body(d, acc):
        return jnp.where(idx == d, tab_ref[h, d], acc)

    o_ref[0] = lax.fori_loop(0, 2 * REL_CLIP + 1, body, jnp.zeros(shape, F32))


def _rel_bias(table, band, tq):
    heads = table.shape[0]
    return pl.pallas_call(
        functools.partial(_rel_bias_kernel, band=band),
        grid=(heads,),
        in_specs=[pl.BlockSpec(memory_space=pltpu.SMEM)],
        out_specs=pl.BlockSpec((1, tq, band + tq), lambda h: (h, 0, 0)),
        out_shape=jax.ShapeDtypeStruct((heads, tq, band + tq), F32),
        compiler_params=_params(1),
        name="rel_bias",
    )(table)


def _band_attn_kernel(*refs, heads, kv_heads, band, tq, past, has_bias, has_sinks):
    q_ref, k_ref, v_ref = refs[:3]
    rest = list(refs[3:])
    bias_ref = rest.pop(0) if has_bias else None
    sink_ref = rest.pop(0) if has_sinks else None
    o_ref = rest.pop(0)

    width = band + tq
    scale = HEAD_DIM ** -0.5
    ws = past + pl.program_id(1) * tq - band
    r = lax.broadcasted_iota(jnp.int32, (tq, width), 0) // CHUNK
    w = lax.broadcasted_iota(jnp.int32, (tq, width), 1)
    wc = w // CHUNK
    allowed = (wc >= r) & (wc <= r + band // CHUNK) & (w + ws >= 0)

    def window(ref, cols):
        if past >= band:
            return ref[0, pl.ds(pl.multiple_of(ws, CHUNK), width), cols]
        pieces = [ref[0, pl.ds(pl.multiple_of(jnp.maximum(ws + c * LANES, 0), LANES), LANES), cols]
                  for c in range(width // LANES)]
        return jnp.concatenate(pieces, axis=0)

    group = heads // kv_heads
    outs = []
    for kh in range(kv_heads):
        cols = slice(kh * HEAD_DIM, (kh + 1) * HEAD_DIM)
        k_w = window(k_ref, cols)
        v_w = window(v_ref, cols)
        for g in range(group):
            h = kh * group + g
            hcols = slice(h * HEAD_DIM, (h + 1) * HEAD_DIM)
            s = _dot_t(q_ref[0, :, hcols], k_w) * scale
            if has_bias:
                s = s + bias_ref[h]
            s = jnp.where(allowed, s, NEG_INF)
            m = jnp.max(s, -1, keepdims=True)
            if has_sinks:
                sink = sink_ref[h]
                m = jnp.maximum(m, sink)
            e = jnp.exp(s - m)
            den = jnp.sum(e, -1, keepdims=True)
            if has_sinks:
                den = den + jnp.exp(sink - m)
            p = (e * (1.0 / den)).astype(BF16)
            outs.append(_dot(p, v_w).astype(o_ref.dtype))
    o_ref[0] = jnp.concatenate(outs, axis=1)


def _band_attn(q, k, v, *, heads, kv_heads, band, tq, past, bias=None, sinks=None):
    (qa, qw, qi), (ka, kw, ki), (va, vw, vi) = q, k, v
    b, t, _ = qa.shape
    s_len = ka.shape[1]
    assert t % tq == 0 and s_len == past + t and qw == heads * HEAD_DIM and kw == kv_heads * HEAD_DIM
    assert past >= band or (past == 0 and tq % LANES == 0 and band % LANES == 0)
    in_specs = [
        pl.BlockSpec((1, tq, qw), lambda bi, i: (bi, i, qi)),
        pl.BlockSpec((1, s_len, kw), lambda bi, i: (bi, 0, ki)),
        pl.BlockSpec((1, s_len, vw), lambda bi, i: (bi, 0, vi)),
    ]
    args = [qa, ka, va]
    if bias is not None:
        in_specs.append(_resident(bias.shape))
        args.append(bias)
    if sinks is not None:
        in_specs.append(pl.BlockSpec(memory_space=pltpu.SMEM))
        args.append(sinks)
    return pl.pallas_call(
        functools.partial(_band_attn_kernel, heads=heads, kv_heads=kv_heads, band=band, tq=tq, past=past,
                          has_bias=bias is not None, has_sinks=sinks is not None),
        grid=(b, t // tq),
        in_specs=in_specs,
        out_specs=pl.BlockSpec((1, tq, qw), lambda bi, i: (bi, i, 0)),
        out_shape=jax.ShapeDtypeStruct((b, t, qw), BF16),
        compiler_params=_params(2),
        name="band_attn",
    )(*args)


def _rel_bias_folded_kernel(tab_ref, o_ref, *, band):
    h = pl.program_id(0)
    tq, width = o_ref.shape[1:]
    period = tq + width
    j = lax.broadcasted_iota(jnp.int32, (8, period), 1)
    dist = jnp.where(j < width, band - j, band - (j - period))
    idx = jnp.clip(dist, -REL_CLIP, REL_CLIP) + REL_CLIP

    def body(d, acc):
        return jnp.where(idx == d, tab_ref[h, d], acc)

    g = lax.fori_loop(0, 2 * REL_CLIP + 1, body, jnp.zeros((8, period), F32))
    full = jnp.concatenate([g] * (tq // 8), axis=0)
    bias = pltpu.roll(full, 0, 1, stride=1, stride_axis=0)[:, :width]
    rc = lax.broadcasted_iota(jnp.int32, (tq, width), 0) // CHUNK
    wc = lax.broadcasted_iota(jnp.int32, (tq, width), 1) // CHUNK
    o_ref[0] = jnp.where((wc >= rc) & (wc <= rc + band // CHUNK), bias * LOG2_E, NEG_INF)


def _rel_bias_folded(table, band, tq):
    heads = table.shape[0]
    assert (band + 2 * tq) % LANES == 0 and tq % 8 == 0
    return pl.pallas_call(
        functools.partial(_rel_bias_folded_kernel, band=band),
        grid=(heads,),
        in_specs=[pl.BlockSpec(memory_space=pltpu.SMEM)],
        out_specs=pl.BlockSpec((1, tq, band + tq), lambda h: (h, 0, 0)),
        out_shape=jax.ShapeDtypeStruct((heads, tq, band + tq), F32),
        compiler_params=_params(1),
        name="rel_bias_folded",
    )(table)


def _band_attn_t_kernel(*refs, heads, kv_heads, band, has_bias, has_sinks):
    q_ref, kt_ref, v_ref = refs[:3]
    rest = list(refs[3:])
    bias_ref = rest.pop(0) if has_bias else None
    sink_ref = rest.pop(0) if has_sinks else None
    o_ref = rest.pop(0)
    tq = kt_ref.shape[3]
    n_past = band // tq
    width = band + tq
    group = heads // kv_heads
    static_ok = None
    if not has_bias:
        rc = (lax.broadcasted_iota(jnp.int32, (group * tq, width), 0) % tq) // CHUNK
        wc = lax.broadcasted_iota(jnp.int32, (group * tq, width), 1) // CHUNK
        static_ok = (wc >= rc) & (wc <= rc + band // CHUNK)
    ones = jnp.ones((width, HEAD_DIM), BF16)
    for sub in range(q_ref.shape[1] // tq):
        i = pl.program_id(1) * (q_ref.shape[1] // tq) + sub
        rows = slice(sub * tq, (sub + 1) * tq)
        ws = i * tq - band
        allowed = (lax.broadcasted_iota(jnp.int32, (1, width), 1) + ws) >= 0
        if static_ok is not None:
            allowed = allowed & static_ok
        outs = []
        for kh in range(kv_heads):
            kc = slice(kh * HEAD_DIM, (kh + 1) * HEAD_DIM)
            hs = range(kh * group, (kh + 1) * group)
            q_st = jnp.concatenate([q_ref[0, rows, h * HEAD_DIM:(h + 1) * HEAD_DIM] for h in hs], axis=0)
            k_t = jnp.concatenate([kt_ref[0, jnp.maximum(i - n_past + c, 0), kc, :] for c in range(n_past + 1)],
                                  axis=1)
            s = _dot(q_st, k_t) * (HEAD_DIM ** -0.5 * LOG2_E)
            if has_bias:
                s = s + jnp.concatenate([bias_ref[h] for h in hs], axis=0)
            s = jnp.where(allowed, s, NEG_INF)
            ms, ps = [], []
            for g, h in enumerate(hs):
                s_g = s[g * tq:(g + 1) * tq]
                m_g = jnp.max(s_g, -1, keepdims=True)
                if has_sinks:
                    m_g = jnp.maximum(m_g, sink_ref[h] * LOG2_E)
                ms.append(m_g)
                ps.append(jnp.exp2(s_g - m_g).astype(BF16))
            v_w = jnp.concatenate([v_ref[0, pl.ds(pl.multiple_of(jnp.maximum(ws + c * tq, 0), tq), tq), kc]
                                   for c in range(n_past + 1)], axis=0)
            o_ext = _dot(jnp.concatenate(ps, axis=0), jnp.concatenate([v_w, ones], axis=1))
            for g, h in enumerate(hs):
                o_g = o_ext[g * tq:(g + 1) * tq]
                den = o_g[:, HEAD_DIM:]
                if has_sinks:
                    den = den + jnp.exp2(sink_ref[h] * LOG2_E - ms[g])
                outs.append((o_g[:, :HEAD_DIM] * (1.0 / den)).astype(o_ref.dtype))
        o_ref[0, rows, :] = jnp.concatenate(outs, axis=1)


def _band_attn_t(q, kt, v, *, heads, kv_heads, band, bias=None, sinks=None):
    (qa, qw, qi), (va, vw, vi) = q, v
    b, t, _ = qa.shape
    _, nkb, _, tq = kt.shape
    assert nkb * tq == t and band % tq == 0 and qw == heads * HEAD_DIM and vw == kv_heads * HEAD_DIM
    assert bias is None or heads == kv_heads
    rows = BAND_STEP_BLOCKS * tq if t % (BAND_STEP_BLOCKS * tq) == 0 else tq
    in_specs = [
        pl.BlockSpec((1, rows, qw), lambda bi, i: (bi, i, qi)),
        pl.BlockSpec((1, nkb, kv_heads * HEAD_DIM, tq), lambda bi, i: (bi, 0, 0, 0)),
        pl.BlockSpec((1, t, vw), lambda bi, i: (bi, 0, vi)),
    ]
    args = [qa, kt, va]
    if bias is not None:
        in_specs.append(_resident(bias.shape))
        args.append(bias)
    if sinks is not None:
        in_specs.append(pl.BlockSpec(memory_space=pltpu.SMEM))
        args.append(sinks)
    return pl.pallas_call(
        functools.partial(_band_attn_t_kernel, heads=heads, kv_heads=kv_heads, band=band,
                          has_bias=bias is not None, has_sinks=sinks is not None),
        grid=(b, t // rows),
        in_specs=in_specs,
        out_specs=pl.BlockSpec((1, rows, qw), lambda bi, i: (bi, i, 0)),
        out_shape=jax.ShapeDtypeStruct((b, t, qw), BF16),
        compiler_params=_params(2),
        name="band_attn_t",
    )(*args)


def _outproj_ln_kernel(*refs, n_in):
    a_refs = refs[:n_in]
    w_ref, x_ref, g_ref, b_ref, o_ref = refs[n_in:]
    d_out = o_ref.shape[1]
    for j in range(d_out // COL_CHUNK):
        cols = slice(j * COL_CHUNK, (j + 1) * COL_CHUNK)
        y = DEEPNORM_ALPHA * x_ref[:, cols]
        r0 = 0
        for a_ref in a_refs:
            kk = a_ref.shape[1]
            y = y + _dot(a_ref[...], w_ref[r0:r0 + kk, cols])
            r0 += kk
        o_ref[:, cols] = y
    o_ref[...] = _layer_norm(o_ref[...], g_ref[...], b_ref[...])


def _outproj_ln(a_list, w, xf, g, b):
    n, d = xf.shape
    tm = _row_tile(n)
    assert sum(a.shape[1] for a in a_list) == w.shape[0]
    return pl.pallas_call(
        functools.partial(_outproj_ln_kernel, n_in=len(a_list)),
        grid=(n // tm,),
        in_specs=[pl.BlockSpec((tm, a.shape[1]), lambda i: (i, 0)) for a in a_list] + [
            _resident(w.shape),
            pl.BlockSpec((tm, d), lambda i: (i, 0)),
            _resident(g.shape),
            _resident(b.shape),
        ],
        out_specs=pl.BlockSpec((tm, d), lambda i: (i, 0)),
        out_shape=jax.ShapeDtypeStruct((n, d), F32),
        compiler_params=_params(1),
        name="outproj_ln",
    )(*a_list, w, xf, g, b)


def _channel_mix_kernel(x_ref, wu_ref, wd_ref, g_ref, b_ref, p_ref, wg_ref, bg_ref, wp_ref, o_ref, xb_ref):
    f = pl.program_id(1)

    @pl.when(f == 0)
    def _():
        xb_ref[...] = x_ref[...].astype(BF16)
        o_ref[...] = jnp.zeros(o_ref.shape, o_ref.dtype)

    hid = _dot(xb_ref[...], wu_ref[...])
    hid = jnp.square(jnp.maximum(hid, 0.0)).astype(BF16)
    for j in range(o_ref.shape[1] // COL_CHUNK):
        cols = slice(j * COL_CHUNK, (j + 1) * COL_CHUNK)
        o_ref[:, cols] += _dot(hid, wd_ref[:, cols])

    @pl.when(f == pl.num_programs(1) - 1)
    def _():
        o_ref[...] = _layer_norm(DEEPNORM_ALPHA * x_ref[...] + o_ref[...], g_ref[...], b_ref[...])
        xb_ref[...] = o_ref[...].astype(BF16)
        pb = p_ref[...].astype(BF16)
        for j in range(o_ref.shape[1] // COL_CHUNK):
            cols = slice(j * COL_CHUNK, (j + 1) * COL_CHUNK)
            gate = jax.nn.sigmoid(_dot(xb_ref[...], wg_ref[:, cols]) + bg_ref[:, cols])
            o_ref[:, cols] = o_ref[:, cols] + gate * _dot(pb, wp_ref[:, cols])


def _channel_mix(xf, p_all, pw, layer):
    w_up, w_down, wg, wp = pw['w_mlp_up'], pw['w_mlp_down'], pw['w_ple_gate'], pw['w_ple']
    g, b, bg = pw['ln2_g'][layer], pw['ln2_b'][layer], pw['b_ple_gate'][layer]
    n, d = xf.shape
    d_ff = w_up.shape[2]
    tm = MLP_ROW_TILE if n % MLP_ROW_TILE == 0 else n
    tf = MLP_FF_TILE
    return pl.pallas_call(
        _channel_mix_kernel,
        grid=(n // tm, d_ff // tf),
        in_specs=[
            pl.BlockSpec((tm, d), lambda i, f: (i, 0)),
            pl.BlockSpec((None, d, tf), lambda i, f: (layer, 0, f)),
            pl.BlockSpec((None, tf, d), lambda i, f: (layer, f, 0)),
            _resident(g.shape),
            _resident(b.shape),
            pl.BlockSpec((None, tm, p_all.shape[2]), lambda i, f: (layer, i, 0)),
            _resident_slab(wg.shape, layer),
            _resident(bg.shape),
            _resident_slab(wp.shape, layer),
        ],
        out_specs=pl.BlockSpec((tm, d), lambda i, f: (i, 0)),
        out_shape=jax.ShapeDtypeStruct((n, d), F32),
        scratch_shapes=[pltpu.VMEM((tm, d), BF16)],
        compiler_params=_params(2),
        name="channel_mix",
    )(xf, w_up, w_down, g, b, p_all, wg, bg, wp)


Q_GROUP = 4


def _proj_cq_kernel(x_ref, w_ref, gq_ref, gkv_ref, cos_ref, sin_ref, wq_ref, q_ref, ckv_ref, kr_ref, krp_ref):
    cos = cos_ref[...]
    sin = sin_ref[...]
    h = _dot(x_ref[...].astype(BF16), w_ref[...])
    cq = _rms_norm(h[:, :C_Q_RANK], gq_ref[...]).astype(BF16)
    ckv_ref[...] = _rms_norm(h[:, C_Q_RANK:C_Q_RANK + C_KV_RANK], gkv_ref[...])
    rot = _rope_tile(h[:, C_Q_RANK + C_KV_RANK:], cos, sin, C_ROPE)
    lane = lax.broadcasted_iota(jnp.int32, rot.shape, 1)
    rot = jnp.where(lane < C_ROPE, rot, 0.0)
    kr_ref[...] = rot[:, :C_ROPE]
    krp_ref[...] = rot.astype(krp_ref.dtype)

    grp_w = Q_GROUP * (C_NOPE + C_ROPE)
    for gi in range(C_HEADS // Q_GROUP):
        acc = _dot(cq, wq_ref[:, gi * grp_w:(gi + 1) * grp_w]) * (MLA_SCALE * LOG2_E)
        for u in range(Q_GROUP):
            o0 = (gi * Q_GROUP + u) * C_QK
            q_ref[:, o0:o0 + C_NOPE] = acc[:, u * C_NOPE:(u + 1) * C_NOPE].astype(q_ref.dtype)
            if u % 2 == 0:
                r0 = Q_GROUP * C_NOPE + (u // 2) * LANES
                qrot = _rope_tile(acc[:, r0:r0 + LANES], cos, sin, C_ROPE)
                piece = qrot
            else:
                piece = pltpu.roll(qrot, C_ROPE, 1)
            q_ref[:, o0 + C_NOPE:o0 + C_QK] = jnp.where(lane < C_ROPE, piece, 0.0).astype(q_ref.dtype)


def _proj_cq(xf, w, gq, gkv, cos, sin, wq):
    n, d = xf.shape
    tm = _row_tile(n)
    n_tab = cos.shape[0] // tm
    rows = lambda i: (i, 0)
    tab = lambda i: (i % n_tab, 0)
    return pl.pallas_call(
        _proj_cq_kernel,
        grid=(n // tm,),
        in_specs=[
            pl.BlockSpec((tm, d), rows),
            _resident(w.shape),
            _resident(gq.shape),
            _resident(gkv.shape),
            pl.BlockSpec((tm, LANES), tab),
            pl.BlockSpec((tm, LANES), tab),
            _resident(wq.shape),
        ],
        out_specs=[
            pl.BlockSpec((tm, C_HEADS * C_QK), rows),
            pl.BlockSpec((tm, C_KV_RANK), rows),
            pl.BlockSpec((tm, C_ROPE), rows),
            pl.BlockSpec((tm, LANES), rows),
        ],
        out_shape=[
            jax.ShapeDtypeStruct((n, C_HEADS * C_QK), BF16),
            jax.ShapeDtypeStruct((n, C_KV_RANK), F32),
            jax.ShapeDtypeStruct((n, C_ROPE), F32),
            jax.ShapeDtypeStruct((n, LANES), BF16),
        ],
        compiler_params=_params(1),
        name="proj_cq",
    )(xf, w, gq, gkv, cos, sin, wq)


def _kv_c_kernel(ckv_ref, krp_ref, wkt_ref, wv_ref, kt_ref, krt_ref, v_ref):
    cb = ckv_ref[...].astype(BF16)
    eye = (lax.broadcasted_iota(jnp.int32, (LANES, LANES), 0)
           == lax.broadcasted_iota(jnp.int32, (LANES, LANES), 1)).astype(F32).astype(BF16)
    krt_ref[0, 0] = _dot_t(eye, krp_ref[...]).astype(krt_ref.dtype)
    kt_ref[0, 0] = _dot_t(wkt_ref[...], cb).astype(kt_ref.dtype)
    for j in range(C_HEADS * C_V // COL_CHUNK):
        cols = slice(j * COL_CHUNK, (j + 1) * COL_CHUNK)
        v_ref[:, cols] = _dot(cb, wv_ref[:, cols]).astype(v_ref.dtype)


def _kv_c(ckv, krp, wkt, wv, batch, tk):
    n = ckv.shape[0]
    nkb = n // batch // tk
    assert n == batch * nkb * tk
    rows = lambda i: (i, 0)
    blk = lambda i: (i // nkb, i % nkb, 0, 0)
    return pl.pallas_call(
        _kv_c_kernel,
        grid=(n // tk,),
        in_specs=[pl.BlockSpec((tk, C_KV_RANK), rows), pl.BlockSpec((tk, LANES), rows), _resident(wkt.shape),
                  _resident(wv.shape)],
        out_specs=[pl.BlockSpec((1, 1, C_HEADS * C_NOPE, tk), blk), pl.BlockSpec((1, 1, LANES, tk), blk),
                   pl.BlockSpec((tk, C_HEADS * C_V), rows)],
        out_shape=[
            jax.ShapeDtypeStruct((batch, nkb, C_HEADS * C_NOPE, tk), BF16),
            jax.ShapeDtypeStruct((batch, nkb, LANES, tk), BF16),
            jax.ShapeDtypeStruct((n, C_HEADS * C_V), BF16),
        ],
        compiler_params=_params(1),
        name="kv_c",
    )(ckv, krp, wkt, wv)


MLA_LONG_BLOCK = 512


def _mla_attn_kernel(q_ref, kt_ref, krt_ref, v_ref, o_ref, *, pos0, heads):
    t = q_ref.shape[1]
    tq = tk = kt_ref.shape[3]
    row = lax.broadcasted_iota(jnp.int32, (tq, tk), 0) // CHUNK
    col = lax.broadcasted_iota(jnp.int32, (tq, tk), 1) // CHUNK
    diag_ok = col <= row
    ones = jnp.ones((tk, C_V), BF16)

    def k_slab(g, kb):
        return jnp.concatenate([kt_ref[0, kb, g * C_NOPE:(g + 1) * C_NOPE, :], krt_ref[0, kb]], axis=0)

    def v_slab(g, kb):
        return jnp.concatenate([v_ref[0, kb * tk:(kb + 1) * tk, g * C_V:(g + 1) * C_V], ones], axis=1)

    def step(s, m, acc, v_ext):
        m_new = jnp.maximum(m, jnp.max(s, -1, keepdims=True))
        p = jnp.exp2(s - m_new).astype(BF16)
        return m_new, jnp.exp2(m - m_new) * acc + _dot(p, v_ext)

    def finish(acc):
        return (acc[:, :C_V] * (1.0 / acc[:, C_V:])).astype(o_ref.dtype)

    for qi in range(t // tq):
        q0 = qi * tq
        qs = [q_ref[0, q0:q0 + tq, g * C_QK:(g + 1) * C_QK] for g in range(heads)]
        n_full = (pos0 + q0) // tk
        state = [(jnp.full((tq, 1), NEG_INF, F32), jnp.zeros((tq, 2 * C_V), F32)) for _ in range(heads)]
        for kb in range(n_full):
            raw = [_dot(qs[g], k_slab(g, kb)) for g in range(heads)]
            state = [step(raw[g], *state[g], v_slab(g, kb)) for g in range(heads)]
        raw = [jnp.where(diag_ok, _dot(qs[g], k_slab(g, n_full)), NEG_INF) for g in range(heads)]
        state = [step(raw[g], *state[g], v_slab(g, n_full)) for g in range(heads)]
        for g in range(heads):
            o_ref[0, q0:q0 + tq, g * C_V:(g + 1) * C_V] = finish(state[g][1])


def _mla_attn(q, kt, krt, v, *, pos0, heads=4):
    b, t, _ = q.shape
    _, nkb, _, tk = kt.shape
    s_len = nkb * tk
    assert t % tk == 0 and pos0 % tk == 0 and pos0 + t <= s_len and v.shape[1] == s_len
    return pl.pallas_call(
        functools.partial(_mla_attn_kernel, pos0=pos0, heads=heads),
        grid=(b, C_HEADS // heads),
        in_specs=[
            pl.BlockSpec((1, t, heads * C_QK), lambda bi, h: (bi, 0, h)),
            pl.BlockSpec((1, nkb, heads * C_NOPE, tk), lambda bi, h: (bi, 0, h, 0)),
            pl.BlockSpec((1, nkb, LANES, tk), lambda bi, h: (bi, 0, 0, 0)),
            pl.BlockSpec((1, s_len, heads * C_V), lambda bi, h: (bi, 0, h)),
        ],
        out_specs=pl.BlockSpec((1, t, heads * C_V), lambda bi, h: (bi, 0, h)),
        out_shape=jax.ShapeDtypeStruct((b, t, C_HEADS * C_V), BF16),
        compiler_params=_params(2),
        name="mla_attn",
    )(q, kt, krt, v)


def _mla_absorbed_kernel(q_ref, ckv_new_ref, kr_new_ref, ckv_past_ref, kr_past_ref, wkt_ref, wv_ref, o_ref):
    t = q_ref.shape[1]
    q_lat, q_rope = [], []
    for h in range(C_HEADS):
        q_lat.append(_dot(q_ref[0, :, h * C_QK:h * C_QK + C_NOPE], wkt_ref[h * C_NOPE:(h + 1) * C_NOPE, :]))
        q_rope.append(q_ref[0, :, h * C_QK + C_NOPE:h * C_QK + C_NOPE + C_ROPE])
    q_lat = jnp.concatenate(q_lat, axis=0).astype(BF16)
    q_rope = jnp.concatenate(q_rope, axis=0)
    lat = [ckv_past_ref[...].astype(BF16), ckv_new_ref[0].astype(BF16)]
    rot = [kr_past_ref[...].astype(BF16), kr_new_ref[0].astype(BF16)]
    s = jnp.concatenate([_dot_t(q_lat, c) + _dot_t(q_rope, r) for c, r in zip(lat, rot)], axis=1)
    e = jnp.exp2(s - jnp.max(s, -1, keepdims=True))
    p = (e * (1.0 / jnp.sum(e, -1, keepdims=True))).astype(BF16)
    n_past = lat[0].shape[0]
    o_lat = (_dot(p[:, :n_past], lat[0]) + _dot(p[:, n_past:], lat[1])).astype(BF16)
    o_ref[0] = jnp.concatenate([_dot(o_lat[h * t:(h + 1) * t], wv_ref[:, h * C_V:(h + 1) * C_V])
                                for h in range(C_HEADS)], axis=1).astype(o_ref.dtype)


def _mla_absorbed(q, ckv_new, kr_new, ckv_past, kr_past, wkt, wv, *, pos0):
    b, t, _ = q.shape
    n_past = ckv_past.shape[2]
    assert t == CHUNK and pos0 % CHUNK == 0 and n_past <= pos0
    return pl.pallas_call(
        _mla_absorbed_kernel,
        grid=(b,),
        in_specs=[
            pl.BlockSpec((1, t, C_HEADS * C_QK), lambda bi: (bi, 0, 0)),
            pl.BlockSpec((1, t, C_KV_RANK), lambda bi: (bi, 0, 0)),
            pl.BlockSpec((1, t, C_ROPE), lambda bi: (bi, 0, 0)),
            pl.BlockSpec((None, None, n_past, C_KV_RANK), lambda bi: (0, bi, 0, 0)),
            pl.BlockSpec((None, None, n_past, C_ROPE), lambda bi: (0, bi, 0, 0)),
            _resident(wkt.shape),
            _resident(wv.shape),
        ],
        out_specs=pl.BlockSpec((1, t, C_HEADS * C_V), lambda bi: (bi, 0, 0)),
        out_shape=jax.ShapeDtypeStruct((b, t, C_HEADS * C_V), BF16),
        compiler_params=_params(1),
        name="mla_absorbed",
    )(q, ckv_new, kr_new, ckv_past, kr_past, wkt, wv)


def _prepare_weights(w):
    o1 = A_Q_W
    o2 = o1 + A_KV_W
    o3 = o2 + A_KV_W
    o4 = o3 + B_W
    o5 = o4 + B_W
    w_ab = w['w_in_ab'][0]
    w_ab = jnp.concatenate([w_ab[:, :o1], w_ab[:, o3:o4], w_ab[:, o4:o5], w_ab[:, o5:], w_ab[:, o1:o2],
                            w_ab[:, o2:o3]], axis=1)
    ka0 = AB_KV_COL0 + 2 * B_W
    w_ab_long = jnp.concatenate([w_ab[:, :AB_KV_COL0], w_ab[:, AB_KV_COL0 + B_W:ka0], w_ab[:, ka0 + A_KV_W:]], axis=1)
    w_k_t_ab = jnp.concatenate([w_ab[:, AB_KV_COL0:AB_KV_COL0 + B_W], w_ab[:, ka0:ka0 + A_KV_W]], axis=1).T
    w_c = jnp.pad(w['w_in_c'][0], ((0, 0), (0, LANES - C_ROPE)))
    hq = C_NOPE + C_ROPE
    q_cols = []
    for g0 in range(0, C_HEADS, Q_GROUP):
        q_cols += [jnp.arange(h * hq, h * hq + C_NOPE) for h in range(g0, g0 + Q_GROUP)]
        q_cols += [jnp.arange(h * hq + C_NOPE, (h + 1) * hq) for h in range(g0, g0 + Q_GROUP)]
    w_q = w['w_q_b_c'][0][:, jnp.concatenate(q_cols)]
    hkv = C_NOPE + C_V
    k_cols = jnp.concatenate([jnp.arange(h * hkv, h * hkv + C_NOPE) for h in range(C_HEADS)])
    v_cols = jnp.concatenate([jnp.arange(h * hkv + C_NOPE, (h + 1) * hkv) for h in range(C_HEADS)])
    w_k_t = w['w_kv_b_c'][0][:, k_cols].T
    w_v = w['w_kv_b_c'][0][:, v_cols]
    row = lambda a: a.reshape(1, -1)
    return {
        'w_in_ab': w_ab.astype(BF16), 'w_in_ab_long': w_ab_long.astype(BF16), 'w_k_t_ab': w_k_t_ab.astype(BF16),
        'w_out_ab': w['w_out_ab'][0].astype(BF16),
        'w_in_c': w_c.astype(BF16), 'w_q_b_c': w_q.astype(BF16), 'w_k_t_c': w_k_t.astype(BF16),
        'w_v_c': w_v.astype(BF16),
        'w_out_c': w['w_out_c'][0].astype(BF16),
        'g_q_c': row(w['g_q_c'][0]), 'g_kv_c': row(w['g_kv_c'][0]),
        'sinks_a': w['sinks_a'][0], 'rel_bias_b': w['rel_bias_b'][0],
        'ln1_g': [row(w['ln1_g'][i]) for i in range(DEPTH)], 'ln1_b': [row(w['ln1_b'][i]) for i in range(DEPTH)],
        'ln2_g': [row(w['ln2_g'][i]) for i in range(DEPTH)], 'ln2_b': [row(w['ln2_b'][i]) for i in range(DEPTH)],
        'w_mlp_up': w['w_mlp_up'].astype(BF16), 'w_mlp_down': w['w_mlp_down'].astype(BF16),
        'w_ple_gate': w['w_ple_gate'].astype(BF16), 'w_ple': w['w_ple'].astype(BF16),
        'b_ple_gate': [row(w['b_ple_gate'][i]) for i in range(DEPTH)],
    }


def _trunk(x, p, pos0, past, pw):
    b, t, d = x.shape
    n = b * t
    tm = _row_tile(n)
    xf = x.reshape(n, d)

    if past is None:
        h, kb_new, vb_new, ka_new, va_new, kb_t, ka_t = _proj_ab(
            xf, pw['w_in_ab_long'], t, pos0, pw['w_k_t_ab'], BAND_B_BLOCK, BAND_A_BLOCK)
    else:
        h, kb_new, vb_new, ka_new, va_new = _proj_ab(xf, pw['w_in_ab'], t, pos0)
    h3 = h.reshape(b, t, h.shape[1])
    kb_new, vb_new, ka_new, va_new = [a.reshape(b, a.shape[0] // b, a.shape[1])
                                      for a in (kb_new, vb_new, ka_new, va_new)]
    q_a = (h3, A_Q_W, 0)
    q_b = (h3, B_W, 1)
    if past is None:
        attn_a = _band_attn_t(q_a, ka_t, (h3, A_KV_W, (AB_KV_COL0 + B_W) // A_KV_W), heads=A_HEADS,
                              kv_heads=A_KV_HEADS, band=WINDOW, sinks=pw['sinks_a'])
        bias_b = _rel_bias_folded(pw['rel_bias_b'], B_BAND_PAST, BAND_B_BLOCK)
        attn_b = _band_attn_t(q_b, kb_t, (h3, B_W, 2), heads=B_HEADS, kv_heads=B_HEADS, band=B_BAND_PAST,
                              bias=bias_b)
        ak, av, bk, bv = ka_new[:, -WINDOW:], va_new[:, -WINDOW:], kb_new, vb_new
    else:
        n_past_a, n_past_b = past[0].shape[1], past[2].shape[1]
        news = (ka_new, va_new, kb_new, vb_new)
        caps = (WINDOW, WINDOW, B_BAND_PAST, B_BAND_PAST)
        ak, av, bk, bv = [
            jnp.concatenate([c, new.reshape(b, t, c.shape[2], HEAD_DIM)], axis=1)[:, -min(cap, c.shape[1] + t):]
            for c, new, cap in zip(past[:4], news, caps)]
        k_a, v_a, k_b, v_b = [
            (jnp.concatenate([c.reshape(b, c.shape[1], -1).astype(BF16), new.astype(BF16)], axis=1), new.shape[2], 0)
            for c, new in zip(past[:4], news)]
        attn_a = _band_attn(q_a, k_a, v_a, heads=A_HEADS, kv_heads=A_KV_HEADS, band=WINDOW, tq=CHUNK,
                            past=n_past_a, sinks=pw['sinks_a'])
        bias_b = _rel_bias(pw['rel_bias_b'], B_BAND_PAST, CHUNK)
        attn_b = _band_attn(q_b, k_b, v_b, heads=B_HEADS, kv_heads=B_HEADS, band=B_BAND_PAST, tq=CHUNK,
                            past=n_past_b, bias=bias_b)
    xf = _outproj_ln([attn_a.reshape(n, A_Q_W), attn_b.reshape(n, B_W)], pw['w_out_ab'], xf,
                     pw['ln1_g'][0], pw['ln1_b'][0])
    p_all = p.reshape(p.shape[0], n, p.shape[3])
    xf = _channel_mix(xf, p_all, pw, 0)

    cos_c, sin_c = _rope_tables(t, pos0, C_ROPE, max(t, tm))
    q, ckv, kr, krp = _proj_cq(xf, pw['w_in_c'], pw['g_q_c'], pw['g_kv_c'], cos_c, sin_c, pw['w_q_b_c'])
    q = q.reshape(b, t, C_HEADS * C_QK)
    if past is None:
        kt_c, krt_c, v_c = _kv_c(ckv, krp, pw['w_k_t_c'], pw['w_v_c'], b, min(MLA_LONG_BLOCK, t))
        attn_c = _mla_attn(q, kt_c, krt_c, v_c.reshape(b, t, C_HEADS * C_V), pos0=pos0)
    else:
        attn_c = _mla_absorbed(q, ckv.reshape(b, t, C_KV_RANK), kr.reshape(b, t, C_ROPE), past[4], past[5],
                               pw['w_k_t_c'], pw['w_v_c'], pos0=pos0)
    xf = _outproj_ln([attn_c.reshape(n, C_HEADS * C_V)], pw['w_out_c'], xf, pw['ln1_g'][1], pw['ln1_b'][1])
    xf = _channel_mix(xf, p_all, pw, 1)

    heads4 = lambda a, hh: a.reshape(1, b, a.shape[1], hh, HEAD_DIM)
    return (xf.reshape(b, t, d), heads4(ak, A_KV_HEADS), heads4(av, A_KV_HEADS), heads4(bk, B_HEADS),
            heads4(bv, B_HEADS), ckv.reshape(1, b, t, C_KV_RANK), kr.reshape(1, b, t, C_ROPE))


def kernel(x_prompt, x_sample, cache_a_k, cache_a_v, cache_b_k, cache_b_v, cache_c_kv, cache_c_krope, p_prompt,
           p_sample, w_in_ab, sinks_a, rel_bias_b, w_out_ab, w_in_c, g_q_c, w_q_b_c, g_kv_c, w_kv_b_c, w_out_c,
           ln1_g, ln1_b, ln2_g, ln2_b, w_mlp_up, w_mlp_down, w_ple_gate, b_ple_gate, w_ple):
    pw = _prepare_weights({
        'w_in_ab': w_in_ab, 'sinks_a': sinks_a, 'rel_bias_b': rel_bias_b, 'w_out_ab': w_out_ab,
        'w_in_c': w_in_c, 'g_q_c': g_q_c, 'w_q_b_c': w_q_b_c, 'g_kv_c': g_kv_c, 'w_kv_b_c': w_kv_b_c,
        'w_out_c': w_out_c, 'ln1_g': ln1_g, 'ln1_b': ln1_b, 'ln2_g': ln2_g, 'ln2_b': ln2_b,
        'w_mlp_up': w_mlp_up, 'w_mlp_down': w_mlp_down, 'w_ple_gate': w_ple_gate, 'b_ple_gate': b_ple_gate,
        'w_ple': w_ple,
    })
    prompt = _trunk(x_prompt, p_prompt, 0, None, pw)
    past = (cache_a_k[0], cache_a_v[0], cache_b_k[0], cache_b_v[0], cache_c_kv, cache_c_krope)
    sample = _trunk(x_sample, p_sample, cache_c_kv.shape[2], past, pw)
    return (prompt[0], sample[0]) + prompt[1:] + sample[1:]
```

```python
import functools

import jax
import jax.numpy as jnp
from jax import lax
from jax.experimental import pallas as pl
from jax.experimental.pallas import tpu as pltpu

F32 = jnp.float32
BF16 = jnp.bfloat16

CHUNK = 64
HEAD_DIM = 128
A_HEADS = 8
A_KV_HEADS = 2
WINDOW = 128
B_HEADS = 8
B_BAND_PAST = 512
REL_CLIP = 128
C_HEADS = 16
C_Q_RANK = 768
C_KV_RANK = 512
C_NOPE = 128
C_ROPE = 64
C_V = 128
DEPTH = 2
ROPE_THETA = 10000.0
LN_EPS = 1e-5
RMS_EPS = 1e-6
NEG_INF = -1e30
DEEPNORM_ALPHA = (2 * DEPTH) ** 0.25
LOG2_E = 1.4426950408889634
MLA_SCALE = (C_NOPE + C_ROPE) ** -0.5

A_Q_W = A_HEADS * HEAD_DIM
A_KV_W = A_KV_HEADS * HEAD_DIM
B_W = B_HEADS * HEAD_DIM
AB_IN_W = A_Q_W + 2 * A_KV_W + 3 * B_W
AB_KV_COL0 = A_Q_W + B_W
AB_KV_W = AB_IN_W - AB_KV_COL0
C_IN_W = C_Q_RANK + C_KV_RANK + C_ROPE
C_QK = 256

LANES = 128
V7X_VMEM_BYTES = 64 * 1024 * 1024
VMEM_LIMIT = V7X_VMEM_BYTES - 8 * 1024 * 1024

ROW_TILE = 512
COL_CHUNK = 512
MLP_ROW_TILE = 512
MLP_FF_TILE = 1024
BAND_B_BLOCK = 256
BAND_A_BLOCK = 128
BAND_STEP_BLOCKS = 8


def _params(n_axes):
    return pltpu.CompilerParams(dimension_semantics=("arbitrary",) * n_axes, vmem_limit_bytes=VMEM_LIMIT)


def _resident(shape):
    nd = len(shape)
    return pl.BlockSpec(shape, lambda *_: (0,) * nd, pipeline_mode=pl.Buffered(1))


def _resident_slab(shape, index):
    nd = len(shape) - 1
    return pl.BlockSpec((None,) + tuple(shape[1:]), lambda *_: (index,) + (0,) * nd, pipeline_mode=pl.Buffered(1))


def _row_tile(n):
    return ROW_TILE if n % ROW_TILE == 0 else n


def _dot(a, b):
    return jnp.dot(a, b, preferred_element_type=F32)


def _dot_t(a, b):
    return lax.dot_general(a, b, (((1,), (1,)), ((), ())), preferred_element_type=F32)


def _layer_norm(y, g, b):
    mu = jnp.mean(y, -1, keepdims=True)
    var = jnp.mean(jnp.square(y - mu), -1, keepdims=True)
    return (y - mu) * lax.rsqrt(var + LN_EPS) * g + b


def _rms_norm(y, g):
    return y * lax.rsqrt(jnp.mean(jnp.square(y), -1, keepdims=True) + RMS_EPS) * g


def _rope_tile(t, cos, sin, d):
    if d == LANES:
        swapped = pltpu.roll(t, LANES // 2, 1)
    else:
        lane = lax.broadcasted_iota(jnp.int32, t.shape, 1)
        swapped = jnp.where((lane % d) < d // 2, pltpu.roll(t, LANES - d // 2, 1), pltpu.roll(t, d // 2, 1))
    return t * cos + swapped * sin


def _rope_tables(t, pos0, d, rows):
    half = d // 2
    inv = ROPE_THETA ** (-jnp.arange(half, dtype=F32) * (2.0 / d))
    ang = (jnp.arange(t, dtype=F32) + pos0)[:, None] * inv[None, :]
    cos = jnp.cos(ang)
    sin = jnp.sin(ang)
    reps = (rows // t, LANES // d)
    return jnp.tile(jnp.concatenate([cos, cos], 1), reps), jnp.tile(jnp.concatenate([-sin, sin], 1), reps)


def _proj_ab_kernel(*refs, kv_period, keys_transposed):
    if keys_transposed:
        (x_ref, w_ref, wkt_ref, cos_ref, sin_ref, cost_ref, sint_ref,
         h_ref, kb_ref, vb_ref, ka_ref, va_ref, kbt_ref, kat_ref) = refs
    else:
        x_ref, w_ref, cos_ref, sin_ref, h_ref, kb_ref, vb_ref, ka_ref, va_ref = refs
    xb = x_ref[...].astype(BF16)
    cos = cos_ref[...]
    sin = sin_ref[...]
    keep_state = (pl.program_id(0) % kv_period) == kv_period - 1
    kv0 = AB_KV_COL0
    rope_tiles = set(range(A_HEADS))
    if keys_transposed:
        state_refs = [(kv0, vb_ref), (kv0 + B_W, va_ref)]
    else:
        state_refs = [(kv0, kb_ref), (kv0 + B_W, vb_ref), (kv0 + 2 * B_W, ka_ref), (kv0 + 2 * B_W + A_KV_W, va_ref)]
        rope_tiles |= {(kv0 + 2 * B_W) // LANES + u for u in range(A_KV_HEADS)}
    if keys_transposed:
        k_t = _dot_t(wkt_ref[...], xb)
        kb_t, ka_t = k_t[:B_W], k_t[B_W:]
        cos_t = cost_ref[...]
        sin_t = sint_ref[...]
        half = HEAD_DIM // 2
        rotated = []
        for u in range(A_KV_HEADS):
            t = ka_t[u * HEAD_DIM:(u + 1) * HEAD_DIM]
            rotated.append(t * cos_t + jnp.concatenate([t[half:], t[:half]], axis=0) * sin_t)
        ka_t = jnp.concatenate(rotated, axis=0)
        for t_val, t_ref in ((kb_t, kbt_ref), (ka_t, kat_ref)):
            blk = t_ref.shape[3]
            for c in range(t_ref.shape[1]):
                t_ref[0, c] = t_val[:, c * blk:(c + 1) * blk].astype(t_ref.dtype)
    width = w_ref.shape[1]
    for c0 in range(0, width, COL_CHUNK):
        cw = min(COL_CHUNK, width - c0)
        acc = _dot(xb, w_ref[:, c0:c0 + cw])
        parts = []
        for u in range(cw // LANES):
            part = acc[:, u * LANES:(u + 1) * LANES]
            if c0 // LANES + u in rope_tiles:
                part = _rope_tile(part, cos, sin, HEAD_DIM)
            parts.append(part)
        acc = jnp.concatenate(parts, axis=1)
        h_ref[:, c0:c0 + cw] = acc.astype(h_ref.dtype)
        for s0, ref in state_refs:
            lo, hi = max(c0, s0), min(c0 + cw, s0 + ref.shape[1])
            if lo < hi:
                @pl.when(keep_state)
                def _(acc=acc, ref=ref, lo=lo, hi=hi, s0=s0, c0=c0):
                    ref[:, lo - s0:hi - s0] = acc[:, lo - c0:hi - c0]
    if keys_transposed:
        @pl.when(keep_state)
        def _():
            kb_ref[...] = kb_t.T
            ka_ref[...] = ka_t.T


def _proj_ab(xf, w, seq, pos0, w_k_t=None, kb_block=None, ka_block=None):
    n, d = xf.shape
    tm = _row_tile(n)
    assert seq % tm == 0 or tm % seq == 0
    kv_period = max(seq // tm, 1)
    assert min(seq, B_BAND_PAST) == min(seq, tm)
    cos, sin = _rope_tables(seq, pos0, HEAD_DIM, max(seq, tm))
    n_tab = cos.shape[0] // tm
    state_widths = (B_W, B_W, A_KV_W, A_KV_W)
    keys_transposed = w_k_t is not None
    rows = lambda i: (i, 0)
    tab = lambda i: (i % n_tab, 0)
    in_specs = [pl.BlockSpec((tm, d), rows), _resident(w.shape)]
    args = [xf, w]
    out_specs = [pl.BlockSpec((tm, w.shape[1]), rows)] + [
        pl.BlockSpec((tm, sw), lambda i: (i // kv_period, 0)) for sw in state_widths]
    out_shape = [jax.ShapeDtypeStruct((n, w.shape[1]), BF16)] + [
        jax.ShapeDtypeStruct((n // kv_period, sw), F32) for sw in state_widths]
    if keys_transposed:
        in_specs.append(_resident(w_k_t.shape))
        args.append(w_k_t)
    in_specs += [pl.BlockSpec((tm, LANES), tab), pl.BlockSpec((tm, LANES), tab)]
    args += [cos, sin]
    if keys_transposed:
        assert seq % tm == 0 and tm % kb_block == 0 and tm % ka_block == 0
        tiles = seq // tm
        tab_t = lambda i: (0, i % n_tab)
        in_specs += [pl.BlockSpec((LANES, tm), tab_t), pl.BlockSpec((LANES, tm), tab_t)]
        args += [cos.T, sin.T]
        for width, blk in ((B_W, kb_block), (A_KV_W, ka_block)):
            out_specs.append(pl.BlockSpec((1, tm // blk, width, blk), lambda i: (i // tiles, i % tiles, 0, 0)))
            out_shape.append(jax.ShapeDtypeStruct((n // seq, seq // blk, width, blk), BF16))
    return pl.pallas_call(
        functools.partial(_proj_ab_kernel, kv_period=kv_period, keys_transposed=keys_transposed),
        grid=(n // tm,),
        in_specs=in_specs,
        out_specs=out_specs,
        out_shape=out_shape,
        compiler_params=_params(1),
        name="proj_ab",
    )(*args)


def _band_attn_kernel(*refs, heads, kv_heads, band, tq, past, has_bias, has_sinks):
    q_ref, k_ref, v_ref = refs[:3]
    rest = list(refs[3:])
    bias_ref = rest.pop(0) if has_bias else None
    sink_ref = rest.pop(0) if has_sinks else None
    o_ref = rest.pop(0)

    width = band + tq
    scale = HEAD_DIM ** -0.5 * LOG2_E
    ws = past + pl.program_id(1) * tq - band
    r = lax.broadcasted_iota(jnp.int32, (tq, width), 0) // CHUNK
    w = lax.broadcasted_iota(jnp.int32, (tq, width), 1)
    wc = w // CHUNK
    allowed = (wc >= r) & (wc <= r + band // CHUNK) & (w + ws >= 0)

    def window(ref, cols):
        if past >= band:
            return ref[0, pl.ds(pl.multiple_of(ws, CHUNK), width), cols]
        pieces = [ref[0, pl.ds(pl.multiple_of(jnp.maximum(ws + c * LANES, 0), LANES), LANES), cols]
                  for c in range(width // LANES)]
        return jnp.concatenate(pieces, axis=0)

    group = heads // kv_heads
    outs = []
    for kh in range(kv_heads):
        cols = slice(kh * HEAD_DIM, (kh + 1) * HEAD_DIM)
        k_w = window(k_ref, cols)
        v_w = window(v_ref, cols)
        for g in range(group):
            h = kh * group + g
            hcols = slice(h * HEAD_DIM, (h + 1) * HEAD_DIM)
            s = _dot_t(q_ref[0, :, hcols], k_w) * scale
            if has_bias:
                s = s + bias_ref[h]
            s = jnp.where(allowed, s, NEG_INF)
            m = jnp.max(s, -1, keepdims=True)
            if has_sinks:
                sink = sink_ref[h] * LOG2_E
                m = jnp.maximum(m, sink)
            e = jnp.exp2(s - m)
            den = jnp.sum(e, -1, keepdims=True)
            if has_sinks:
                den = den + jnp.exp2(sink - m)
            p = (e * (1.0 / den)).astype(BF16)
            outs.append(_dot(p, v_w).astype(o_ref.dtype))
    o_ref[0] = jnp.concatenate(outs, axis=1)


def _band_attn(q, k, v, *, heads, kv_heads, band, tq, past, bias=None, sinks=None):
    (qa, qw, qi), (ka, kw, ki), (va, vw, vi) = q, k, v
    b, t, _ = qa.shape
    s_len = ka.shape[1]
    assert t % tq == 0 and s_len == past + t and qw == heads * HEAD_DIM and kw == kv_heads * HEAD_DIM
    assert past >= band or (past == 0 and tq % LANES == 0 and band % LANES == 0)
    in_specs = [
        pl.BlockSpec((1, tq, qw), lambda bi, i: (bi, i, qi)),
        pl.BlockSpec((1, s_len, kw), lambda bi, i: (bi, 0, ki)),
        pl.BlockSpec((1, s_len, vw), lambda bi, i: (bi, 0, vi)),
    ]
    args = [qa, ka, va]
    if bias is not None:
        in_specs.append(_resident(bias.shape))
        args.append(bias)
    if sinks is not None:
        in_specs.append(pl.BlockSpec(memory_space=pltpu.SMEM))
        args.append(sinks)
    return pl.pallas_call(
        functools.partial(_band_attn_kernel, heads=heads, kv_heads=kv_heads, band=band, tq=tq, past=past,
                          has_bias=bias is not None, has_sinks=sinks is not None),
        grid=(b, t // tq),
        in_specs=in_specs,
        out_specs=pl.BlockSpec((1, tq, qw), lambda bi, i: (bi, i, 0)),
        out_shape=jax.ShapeDtypeStruct((b, t, qw), BF16),
        compiler_params=_params(2),
        name="band_attn",
    )(*args)


def _rel_bias_folded_kernel(tab_ref, o_ref, *, band):
    h = pl.program_id(0)
    tq, width = o_ref.shape[1:]
    period = tq + width
    j = lax.broadcasted_iota(jnp.int32, (8, period), 1)
    dist = jnp.where(j < width, band - j, band - (j - period))
    idx = jnp.clip(dist, -REL_CLIP, REL_CLIP) + REL_CLIP

    def body(d, acc):
        return jnp.where(idx == d, tab_ref[h, d], acc)

    g = lax.fori_loop(0, 2 * REL_CLIP + 1, body, jnp.zeros((8, period), F32))
    full = jnp.concatenate([g] * (tq // 8), axis=0)
    bias = pltpu.roll(full, 0, 1, stride=1, stride_axis=0)[:, :width]
    rc = lax.broadcasted_iota(jnp.int32, (tq, width), 0) // CHUNK
    wc = lax.broadcasted_iota(jnp.int32, (tq, width), 1) // CHUNK
    o_ref[0] = jnp.where((wc >= rc) & (wc <= rc + band // CHUNK), bias * LOG2_E, NEG_INF)


def _rel_bias_folded(table, band, tq):
    heads = table.shape[0]
    assert (band + 2 * tq) % LANES == 0 and tq % 8 == 0
    return pl.pallas_call(
        functools.partial(_rel_bias_folded_kernel, band=band),
        grid=(heads,),
        in_specs=[pl.BlockSpec(memory_space=pltpu.SMEM)],
        out_specs=pl.BlockSpec((1, tq, band + tq), lambda h: (h, 0, 0)),
        out_shape=jax.ShapeDtypeStruct((heads, tq, band + tq), F32),
        compiler_params=_params(1),
        name="rel_bias_folded",
    )(table)


def _band_attn_t_kernel(*refs, heads, kv_heads, band, has_bias, has_sinks):
    q_ref, kt_ref, v_ref = refs[:3]
    rest = list(refs[3:])
    bias_ref = rest.pop(0) if has_bias else None
    sink_ref = rest.pop(0) if has_sinks else None
    o_ref = rest.pop(0)
    tq = kt_ref.shape[3]
    n_past = band // tq
    width = band + tq
    group = heads // kv_heads
    static_ok = None
    if not has_bias:
        rc = (lax.broadcasted_iota(jnp.int32, (group * tq, width), 0) % tq) // CHUNK
        wc = lax.broadcasted_iota(jnp.int32, (group * tq, width), 1) // CHUNK
        static_ok = (wc >= rc) & (wc <= rc + band // CHUNK)
    ones = jnp.ones((width, HEAD_DIM), BF16)
    for sub in range(q_ref.shape[1] // tq):
        i = pl.program_id(1) * (q_ref.shape[1] // tq) + sub
        rows = slice(sub * tq, (sub + 1) * tq)
        ws = i * tq - band
        allowed = (lax.broadcasted_iota(jnp.int32, (1, width), 1) + ws) >= 0
        if static_ok is not None:
            allowed = allowed & static_ok
        outs = []
        for kh in range(kv_heads):
            kc = slice(kh * HEAD_DIM, (kh + 1) * HEAD_DIM)
            hs = range(kh * group, (kh + 1) * group)
            q_st = jnp.concatenate([q_ref[0, rows, h * HEAD_DIM:(h + 1) * HEAD_DIM] for h in hs], axis=0)
            k_t = jnp.concatenate([kt_ref[0, jnp.maximum(i - n_past + c, 0), kc, :] for c in range(n_past + 1)],
                                  axis=1)
            s = _dot(q_st, k_t) * (HEAD_DIM ** -0.5 * LOG2_E)
            if has_bias:
                s = s + jnp.concatenate([bias_ref[h] for h in hs], axis=0)
            s = jnp.where(allowed, s, NEG_INF)
            ms, ps = [], []
            for g, h in enumerate(hs):
                s_g = s[g * tq:(g + 1) * tq]
                m_g = jnp.max(s_g, -1, keepdims=True)
                if has_sinks:
                    m_g = jnp.maximum(m_g, sink_ref[h] * LOG2_E)
                ms.append(m_g)
                ps.append(jnp.exp2(s_g - m_g).astype(BF16))
            v_w = jnp.concatenate([v_ref[0, pl.ds(pl.multiple_of(jnp.maximum(ws + c * tq, 0), tq), tq), kc]
                                   for c in range(n_past + 1)], axis=0)
            o_ext = _dot(jnp.concatenate(ps, axis=0), jnp.concatenate([v_w, ones], axis=1))
            for g, h in enumerate(hs):
                o_g = o_ext[g * tq:(g + 1) * tq]
                den = o_g[:, HEAD_DIM:]
                if has_sinks:
                    den = den + jnp.exp2(sink_ref[h] * LOG2_E - ms[g])
                outs.append((o_g[:, :HEAD_DIM] * (1.0 / den)).astype(o_ref.dtype))
        o_ref[0, rows, :] = jnp.concatenate(outs, axis=1)


def _band_attn_t(q, kt, v, *, heads, kv_heads, band, bias=None, sinks=None):
    (qa, qw, qi), (va, vw, vi) = q, v
    b, t, _ = qa.shape
    _, nkb, _, tq = kt.shape
    assert nkb * tq == t and band % tq == 0 and qw == heads * HEAD_DIM and vw == kv_heads * HEAD_DIM
    assert bias is None or heads == kv_heads
    rows = BAND_STEP_BLOCKS * tq if t % (BAND_STEP_BLOCKS * tq) == 0 else tq
    in_specs = [
        pl.BlockSpec((1, rows, qw), lambda bi, i: (bi, i, qi)),
        pl.BlockSpec((1, nkb, kv_heads * HEAD_DIM, tq), lambda bi, i: (bi, 0, 0, 0)),
        pl.BlockSpec((1, t, vw), lambda bi, i: (bi, 0, vi)),
    ]
    args = [qa, kt, va]
    if bias is not None:
        in_specs.append(_resident(bias.shape))
        args.append(bias)
    if sinks is not None:
        in_specs.append(pl.BlockSpec(memory_space=pltpu.SMEM))
        args.append(sinks)
    return pl.pallas_call(
        functools.partial(_band_attn_t_kernel, heads=heads, kv_heads=kv_heads, band=band,
                          has_bias=bias is not None, has_sinks=sinks is not None),
        grid=(b, t // rows),
        in_specs=in_specs,
        out_specs=pl.BlockSpec((1, rows, qw), lambda bi, i: (bi, i, 0)),
        out_shape=jax.ShapeDtypeStruct((b, t, qw), BF16),
        compiler_params=_params(2),
        name="band_attn_t",
    )(*args)


def _outproj_ln_kernel(*refs, n_in):
    a_refs = refs[:n_in]
    w_ref, x_ref, g_ref, b_ref, o_ref = refs[n_in:]
    d_out = o_ref.shape[1]
    for j in range(d_out // COL_CHUNK):
        cols = slice(j * COL_CHUNK, (j + 1) * COL_CHUNK)
        y = DEEPNORM_ALPHA * x_ref[:, cols]
        r0 = 0
        for a_ref in a_refs:
            kk = a_ref.shape[1]
            y = y + _dot(a_ref[...], w_ref[r0:r0 + kk, cols])
            r0 += kk
        o_ref[:, cols] = y
    o_ref[...] = _layer_norm(o_ref[...], g_ref[...], b_ref[...])


def _outproj_ln(a_list, w, xf, g, b):
    n, d = xf.shape
    tm = _row_tile(n)
    assert sum(a.shape[1] for a in a_list) == w.shape[0]
    return pl.pallas_call(
        functools.partial(_outproj_ln_kernel, n_in=len(a_list)),
        grid=(n // tm,),
        in_specs=[pl.BlockSpec((tm, a.shape[1]), lambda i: (i, 0)) for a in a_list] + [
            _resident(w.shape),
            pl.BlockSpec((tm, d), lambda i: (i, 0)),
            _resident(g.shape),
            _resident(b.shape),
        ],
        out_specs=pl.BlockSpec((tm, d), lambda i: (i, 0)),
        out_shape=jax.ShapeDtypeStruct((n, d), F32),
        compiler_params=_params(1),
        name="outproj_ln",
    )(*a_list, w, xf, g, b)


def _channel_mix_kernel(x_ref, wu_ref, wd_ref, g_ref, b_ref, p_ref, wg_ref, bg_ref, wp_ref, o_ref, xb_ref):
    f = pl.program_id(1)

    @pl.when(f == 0)
    def _():
        xb_ref[...] = x_ref[...].astype(BF16)
        o_ref[...] = jnp.zeros(o_ref.shape, o_ref.dtype)

    hid = _dot(xb_ref[...], wu_ref[...])
    hid = jnp.square(jnp.maximum(hid, 0.0)).astype(BF16)
    for j in range(o_ref.shape[1] // COL_CHUNK):
        cols = slice(j * COL_CHUNK, (j + 1) * COL_CHUNK)
        o_ref[:, cols] += _dot(hid, wd_ref[:, cols])

    @pl.when(f == pl.num_programs(1) - 1)
    def _():
        o_ref[...] = _layer_norm(DEEPNORM_ALPHA * x_ref[...] + o_ref[...], g_ref[...], b_ref[...])
        xb_ref[...] = o_ref[...].astype(BF16)
        pb = p_ref[...].astype(BF16)
        for j in range(o_ref.shape[1] // COL_CHUNK):
            cols = slice(j * COL_CHUNK, (j + 1) * COL_CHUNK)
            gate = jax.nn.sigmoid(_dot(xb_ref[...], wg_ref[:, cols]) + bg_ref[:, cols])
            o_ref[:, cols] = o_ref[:, cols] + gate * _dot(pb, wp_ref[:, cols])


def _channel_mix(xf, p_all, pw, layer):
    w_up, w_down, wg, wp = pw['w_mlp_up'], pw['w_mlp_down'], pw['w_ple_gate'], pw['w_ple']
    g, b, bg = pw['ln2_g'][layer], pw['ln2_b'][layer], pw['b_ple_gate'][layer]
    n, d = xf.shape
    d_ff = w_up.shape[2]
    tm = MLP_ROW_TILE if n % MLP_ROW_TILE == 0 else n
    tf = MLP_FF_TILE
    return pl.pallas_call(
        _channel_mix_kernel,
        grid=(n // tm, d_ff // tf),
        in_specs=[
            pl.BlockSpec((tm, d), lambda i, f: (i, 0)),
            pl.BlockSpec((None, d, tf), lambda i, f: (layer, 0, f)),
            pl.BlockSpec((None, tf, d), lambda i, f: (layer, f, 0)),
            _resident(g.shape),
            _resident(b.shape),
            pl.BlockSpec((None, tm, p_all.shape[2]), lambda i, f: (layer, i, 0)),
            _resident_slab(wg.shape, layer),
            _resident(bg.shape),
            _resident_slab(wp.shape, layer),
        ],
        out_specs=pl.BlockSpec((tm, d), lambda i, f: (i, 0)),
        out_shape=jax.ShapeDtypeStruct((n, d), F32),
        scratch_shapes=[pltpu.VMEM((tm, d), BF16)],
        compiler_params=_params(2),
        name="channel_mix",
    )(xf, w_up, w_down, g, b, p_all, wg, bg, wp)


Q_GROUP = 4


def _proj_cq_kernel(x_ref, w_ref, gq_ref, gkv_ref, cos_ref, sin_ref, wq_ref, q_ref, ckv_ref, kr_ref, krp_ref):
    cos = cos_ref[...]
    sin = sin_ref[...]
    h = _dot(x_ref[...].astype(BF16), w_ref[...])
    cq = _rms_norm(h[:, :C_Q_RANK], gq_ref[...]).astype(BF16)
    ckv_ref[...] = _rms_norm(h[:, C_Q_RANK:C_Q_RANK + C_KV_RANK], gkv_ref[...])
    rot = _rope_tile(h[:, C_Q_RANK + C_KV_RANK:], cos, sin, C_ROPE)
    lane = lax.broadcasted_iota(jnp.int32, rot.shape, 1)
    rot = jnp.where(lane < C_ROPE, rot, 0.0)
    kr_ref[...] = rot[:, :C_ROPE]
    krp_ref[...] = rot.astype(krp_ref.dtype)

    grp_w = Q_GROUP * (C_NOPE + C_ROPE)
    for gi in range(C_HEADS // Q_GROUP):
        acc = _dot(cq, wq_ref[:, gi * grp_w:(gi + 1) * grp_w]) * (MLA_SCALE * LOG2_E)
        for u in range(Q_GROUP):
            o0 = (gi * Q_GROUP + u) * C_QK
            q_ref[:, o0:o0 + C_NOPE] = acc[:, u * C_NOPE:(u + 1) * C_NOPE].astype(q_ref.dtype)
            if u % 2 == 0:
                r0 = Q_GROUP * C_NOPE + (u // 2) * LANES
                qrot = _rope_tile(acc[:, r0:r0 + LANES], cos, sin, C_ROPE)
                piece = qrot
            else:
                piece = pltpu.roll(qrot, C_ROPE, 1)
            q_ref[:, o0 + C_NOPE:o0 + C_QK] = jnp.where(lane < C_ROPE, piece, 0.0).astype(q_ref.dtype)


def _proj_cq(xf, w, gq, gkv, cos, sin, wq):
    n, d = xf.shape
    tm = _row_tile(n)
    n_tab = cos.shape[0] // tm
    rows = lambda i: (i, 0)
    tab = lambda i: (i % n_tab, 0)
    return pl.pallas_call(
        _proj_cq_kernel,
        grid=(n // tm,),
        in_specs=[
            pl.BlockSpec((tm, d), rows),
            _resident(w.shape),
            _resident(gq.shape),
            _resident(gkv.shape),
            pl.BlockSpec((tm, LANES), tab),
            pl.BlockSpec((tm, LANES), tab),
            _resident(wq.shape),
        ],
        out_specs=[
            pl.BlockSpec((tm, C_HEADS * C_QK), rows),
            pl.BlockSpec((tm, C_KV_RANK), rows),
            pl.BlockSpec((tm, C_ROPE), rows),
            pl.BlockSpec((tm, LANES), rows),
        ],
        out_shape=[
            jax.ShapeDtypeStruct((n, C_HEADS * C_QK), BF16),
            jax.ShapeDtypeStruct((n, C_KV_RANK), F32),
            jax.ShapeDtypeStruct((n, C_ROPE), F32),
            jax.ShapeDtypeStruct((n, LANES), BF16),
        ],
        compiler_params=_params(1),
        name="proj_cq",
    )(xf, w, gq, gkv, cos, sin, wq)


def _kv_c_kernel(ckv_ref, krp_ref, wkt_ref, wv_ref, kt_ref, krt_ref, v_ref):
    cb = ckv_ref[...].astype(BF16)
    eye = (lax.broadcasted_iota(jnp.int32, (LANES, LANES), 0)
           == lax.broadcasted_iota(jnp.int32, (LANES, LANES), 1)).astype(F32).astype(BF16)
    krt_ref[0, 0] = _dot_t(eye, krp_ref[...]).astype(krt_ref.dtype)
    kt_ref[0, 0] = _dot_t(wkt_ref[...], cb).astype(kt_ref.dtype)
    for j in range(C_HEADS * C_V // COL_CHUNK):
        cols = slice(j * COL_CHUNK, (j + 1) * COL_CHUNK)
        v_ref[:, cols] = _dot(cb, wv_ref[:, cols]).astype(v_ref.dtype)


def _kv_c(ckv, krp, wkt, wv, batch, tk):
    n = ckv.shape[0]
    nkb = n // batch // tk
    assert n == batch * nkb * tk
    rows = lambda i: (i, 0)
    blk = lambda i: (i // nkb, i % nkb, 0, 0)
    return pl.pallas_call(
        _kv_c_kernel,
        grid=(n // tk,),
        in_specs=[pl.BlockSpec((tk, C_KV_RANK), rows), pl.BlockSpec((tk, LANES), rows), _resident(wkt.shape),
                  _resident(wv.shape)],
        out_specs=[pl.BlockSpec((1, 1, C_HEADS * C_NOPE, tk), blk), pl.BlockSpec((1, 1, LANES, tk), blk),
                   pl.BlockSpec((tk, C_HEADS * C_V), rows)],
        out_shape=[
            jax.ShapeDtypeStruct((batch, nkb, C_HEADS * C_NOPE, tk), BF16),
            jax.ShapeDtypeStruct((batch, nkb, LANES, tk), BF16),
            jax.ShapeDtypeStruct((n, C_HEADS * C_V), BF16),
        ],
        compiler_params=_params(1),
        name="kv_c",
    )(ckv, krp, wkt, wv)


MLA_LONG_BLOCK = 512


def _mla_attn_kernel(q_ref, kt_ref, krt_ref, v_ref, o_ref, *, pos0, heads):
    t = q_ref.shape[1]
    tq = tk = kt_ref.shape[3]
    row = lax.broadcasted_iota(jnp.int32, (tq, tk), 0) // CHUNK
    col = lax.broadcasted_iota(jnp.int32, (tq, tk), 1) // CHUNK
    diag_ok = col <= row
    ones = jnp.ones((tk, C_V), BF16)

    def k_slab(g, kb):
        return jnp.concatenate([kt_ref[0, kb, g * C_NOPE:(g + 1) * C_NOPE, :], krt_ref[0, kb]], axis=0)

    def v_slab(g, kb):
        return jnp.concatenate([v_ref[0, kb * tk:(kb + 1) * tk, g * C_V:(g + 1) * C_V], ones], axis=1)

    def step(s, m, acc, v_ext):
        m_new = jnp.maximum(m, jnp.max(s, -1, keepdims=True))
        p = jnp.exp2(s - m_new).astype(BF16)
        return m_new, jnp.exp2(m - m_new) * acc + _dot(p, v_ext)

    def finish(acc):
        return (acc[:, :C_V] * (1.0 / acc[:, C_V:])).astype(o_ref.dtype)

    for qi in range(t // tq):
        q0 = qi * tq
        qs = [q_ref[0, q0:q0 + tq, g * C_QK:(g + 1) * C_QK] for g in range(heads)]
        n_full = (pos0 + q0) // tk
        state = [(jnp.full((tq, 1), NEG_INF, F32), jnp.zeros((tq, 2 * C_V), F32)) for _ in range(heads)]
        for kb in range(n_full):
            raw = [_dot(qs[g], k_slab(g, kb)) for g in range(heads)]
            state = [step(raw[g], *state[g], v_slab(g, kb)) for g in range(heads)]
        raw = [jnp.where(diag_ok, _dot(qs[g], k_slab(g, n_full)), NEG_INF) for g in range(heads)]
        state = [step(raw[g], *state[g], v_slab(g, n_full)) for g in range(heads)]
        for g in range(heads):
            o_ref[0, q0:q0 + tq, g * C_V:(g + 1) * C_V] = finish(state[g][1])


def _mla_attn(q, kt, krt, v, *, pos0, heads=4):
    b, t, _ = q.shape
    _, nkb, _, tk = kt.shape
    s_len = nkb * tk
    assert t % tk == 0 and pos0 % tk == 0 and pos0 + t <= s_len and v.shape[1] == s_len
    return pl.pallas_call(
        functools.partial(_mla_attn_kernel, pos0=pos0, heads=heads),
        grid=(b, C_HEADS // heads),
        in_specs=[
            pl.BlockSpec((1, t, heads * C_QK), lambda bi, h: (bi, 0, h)),
            pl.BlockSpec((1, nkb, heads * C_NOPE, tk), lambda bi, h: (bi, 0, h, 0)),
            pl.BlockSpec((1, nkb, LANES, tk), lambda bi, h: (bi, 0, 0, 0)),
            pl.BlockSpec((1, s_len, heads * C_V), lambda bi, h: (bi, 0, h)),
        ],
        out_specs=pl.BlockSpec((1, t, heads * C_V), lambda bi, h: (bi, 0, h)),
        out_shape=jax.ShapeDtypeStruct((b, t, C_HEADS * C_V), BF16),
        compiler_params=_params(2),
        name="mla_attn",
    )(q, kt, krt, v)


def _mla_absorbed_kernel(q_ref, ckv_new_ref, kr_new_ref, ckv_past_ref, kr_past_ref, wkt_ref, wv_ref, o_ref):
    t = q_ref.shape[1]
    q_lat, q_rope = [], []
    for h in range(C_HEADS):
        q_lat.append(_dot(q_ref[0, :, h * C_QK:h * C_QK + C_NOPE], wkt_ref[h * C_NOPE:(h + 1) * C_NOPE, :]))
        q_rope.append(q_ref[0, :, h * C_QK + C_NOPE:h * C_QK + C_NOPE + C_ROPE])
    q_lat = jnp.concatenate(q_lat, axis=0).astype(BF16)
    q_rope = jnp.concatenate(q_rope, axis=0)
    lat = [ckv_past_ref[...].astype(BF16), ckv_new_ref[0].astype(BF16)]
    rot = [kr_past_ref[...].astype(BF16), kr_new_ref[0].astype(BF16)]
    s = jnp.concatenate([_dot_t(q_lat, c) + _dot_t(q_rope, r) for c, r in zip(lat, rot)], axis=1)
    e = jnp.exp2(s - jnp.max(s, -1, keepdims=True))
    p = (e * (1.0 / jnp.sum(e, -1, keepdims=True))).astype(BF16)
    n_past = lat[0].shape[0]
    o_lat = (_dot(p[:, :n_past], lat[0]) + _dot(p[:, n_past:], lat[1])).astype(BF16)
    o_ref[0] = jnp.concatenate([_dot(o_lat[h * t:(h + 1) * t], wv_ref[:, h * C_V:(h + 1) * C_V])
                                for h in range(C_HEADS)], axis=1).astype(o_ref.dtype)


def _mla_absorbed(q, ckv_new, kr_new, ckv_past, kr_past, wkt, wv, *, pos0):
    b, t, _ = q.shape
    n_past = ckv_past.shape[2]
    assert t == CHUNK and pos0 % CHUNK == 0 and n_past <= pos0
    return pl.pallas_call(
        _mla_absorbed_kernel,
        grid=(b,),
        in_specs=[
            pl.BlockSpec((1, t, C_HEADS * C_QK), lambda bi: (bi, 0, 0)),
            pl.BlockSpec((1, t, C_KV_RANK), lambda bi: (bi, 0, 0)),
            pl.BlockSpec((1, t, C_ROPE), lambda bi: (bi, 0, 0)),
            pl.BlockSpec((None, None, n_past, C_KV_RANK), lambda bi: (0, bi, 0, 0)),
            pl.BlockSpec((None, None, n_past, C_ROPE), lambda bi: (0, bi, 0, 0)),
            _resident(wkt.shape),
            _resident(wv.shape),
        ],
        out_specs=pl.BlockSpec((1, t, C_HEADS * C_V), lambda bi: (bi, 0, 0)),
        out_shape=jax.ShapeDtypeStruct((b, t, C_HEADS * C_V), BF16),
        compiler_params=_params(1),
        name="mla_absorbed",
    )(q, ckv_new, kr_new, ckv_past, kr_past, wkt, wv)


def _prepare_weights(w):
    o1 = A_Q_W
    o2 = o1 + A_KV_W
    o3 = o2 + A_KV_W
    o4 = o3 + B_W
    o5 = o4 + B_W
    w_ab = w['w_in_ab'][0]
    w_ab = jnp.concatenate([w_ab[:, :o1], w_ab[:, o3:o4], w_ab[:, o4:o5], w_ab[:, o5:], w_ab[:, o1:o2],
                            w_ab[:, o2:o3]], axis=1)
    ka0 = AB_KV_COL0 + 2 * B_W
    w_ab_long = jnp.concatenate([w_ab[:, :AB_KV_COL0], w_ab[:, AB_KV_COL0 + B_W:ka0], w_ab[:, ka0 + A_KV_W:]], axis=1)
    w_k_t_ab = jnp.concatenate([w_ab[:, AB_KV_COL0:AB_KV_COL0 + B_W], w_ab[:, ka0:ka0 + A_KV_W]], axis=1).T
    w_c = jnp.pad(w['w_in_c'][0], ((0, 0), (0, LANES - C_ROPE)))
    hq = C_NOPE + C_ROPE
    q_cols = []
    for g0 in range(0, C_HEADS, Q_GROUP):
        q_cols += [jnp.arange(h * hq, h * hq + C_NOPE) for h in range(g0, g0 + Q_GROUP)]
        q_cols += [jnp.arange(h * hq + C_NOPE, (h + 1) * hq) for h in range(g0, g0 + Q_GROUP)]
    w_q = w['w_q_b_c'][0][:, jnp.concatenate(q_cols)]
    hkv = C_NOPE + C_V
    k_cols = jnp.concatenate([jnp.arange(h * hkv, h * hkv + C_NOPE) for h in range(C_HEADS)])
    v_cols = jnp.concatenate([jnp.arange(h * hkv + C_NOPE, (h + 1) * hkv) for h in range(C_HEADS)])
    w_k_t = w['w_kv_b_c'][0][:, k_cols].T
    w_v = w['w_kv_b_c'][0][:, v_cols]
    row = lambda a: a.reshape(1, -1)
    return {
        'w_in_ab': w_ab.astype(BF16), 'w_in_ab_long': w_ab_long.astype(BF16), 'w_k_t_ab': w_k_t_ab.astype(BF16),
        'w_out_ab': w['w_out_ab'][0].astype(BF16),
        'w_in_c': w_c.astype(BF16), 'w_q_b_c': w_q.astype(BF16), 'w_k_t_c': w_k_t.astype(BF16),
        'w_v_c': w_v.astype(BF16),
        'w_out_c': w['w_out_c'][0].astype(BF16),
        'g_q_c': row(w['g_q_c'][0]), 'g_kv_c': row(w['g_kv_c'][0]),
        'sinks_a': w['sinks_a'][0], 'rel_bias_b': w['rel_bias_b'][0],
        'ln1_g': [row(w['ln1_g'][i]) for i in range(DEPTH)], 'ln1_b': [row(w['ln1_b'][i]) for i in range(DEPTH)],
        'ln2_g': [row(w['ln2_g'][i]) for i in range(DEPTH)], 'ln2_b': [row(w['ln2_b'][i]) for i in range(DEPTH)],
        'w_mlp_up': w['w_mlp_up'].astype(BF16), 'w_mlp_down': w['w_mlp_down'].astype(BF16),
        'w_ple_gate': w['w_ple_gate'].astype(BF16), 'w_ple': w['w_ple'].astype(BF16),
        'b_ple_gate': [row(w['b_ple_gate'][i]) for i in range(DEPTH)],
    }


def _trunk(x, p, pos0, past, pw):
    b, t, d = x.shape
    n = b * t
    tm = _row_tile(n)
    xf = x.reshape(n, d)

    if past is None:
        h, kb_new, vb_new, ka_new, va_new, kb_t, ka_t = _proj_ab(
            xf, pw['w_in_ab_long'], t, pos0, pw['w_k_t_ab'], BAND_B_BLOCK, BAND_A_BLOCK)
    else:
        h, kb_new, vb_new, ka_new, va_new = _proj_ab(xf, pw['w_in_ab'], t, pos0)
    h3 = h.reshape(b, t, h.shape[1])
    kb_new, vb_new, ka_new, va_new = [a.reshape(b, a.shape[0] // b, a.shape[1])
                                      for a in (kb_new, vb_new, ka_new, va_new)]
    q_a = (h3, A_Q_W, 0)
    q_b = (h3, B_W, 1)
    if past is None:
        attn_a = _band_attn_t(q_a, ka_t, (h3, A_KV_W, (AB_KV_COL0 + B_W) // A_KV_W), heads=A_HEADS,
                              kv_heads=A_KV_HEADS, band=WINDOW, sinks=pw['sinks_a'])
        bias_b = _rel_bias_folded(pw['rel_bias_b'], B_BAND_PAST, BAND_B_BLOCK)
        attn_b = _band_attn_t(q_b, kb_t, (h3, B_W, 2), heads=B_HEADS, kv_heads=B_HEADS, band=B_BAND_PAST,
                              bias=bias_b)
        ak, av, bk, bv = ka_new[:, -WINDOW:], va_new[:, -WINDOW:], kb_new, vb_new
    else:
        n_past_a, n_past_b = past[0].shape[1], past[2].shape[1]
        news = (ka_new, va_new, kb_new, vb_new)
        caps = (WINDOW, WINDOW, B_BAND_PAST, B_BAND_PAST)
        ak, av, bk, bv = [
            jnp.concatenate([c, new.reshape(b, t, c.shape[2], HEAD_DIM)], axis=1)[:, -min(cap, c.shape[1] + t):]
            for c, new, cap in zip(past[:4], news, caps)]
        k_a, v_a, k_b, v_b = [
            (jnp.concatenate([c.reshape(b, c.shape[1], -1).astype(BF16), new.astype(BF16)], axis=1), new.shape[2], 0)
            for c, new in zip(past[:4], news)]
        attn_a = _band_attn(q_a, k_a, v_a, heads=A_HEADS, kv_heads=A_KV_HEADS, band=WINDOW, tq=CHUNK,
                            past=n_past_a, sinks=pw['sinks_a'])
        bias_b = _rel_bias_folded(pw['rel_bias_b'], B_BAND_PAST, CHUNK)
        attn_b = _band_attn(q_b, k_b, v_b, heads=B_HEADS, kv_heads=B_HEADS, band=B_BAND_PAST, tq=CHUNK,
                            past=n_past_b, bias=bias_b)
    xf = _outproj_ln([attn_a.reshape(n, A_Q_W), attn_b.reshape(n, B_W)], pw['w_out_ab'], xf,
                     pw['ln1_g'][0], pw['ln1_b'][0])
    p_all = p.reshape(p.shape[0], n, p.shape[3])
    xf = _channel_mix(xf, p_all, pw, 0)

    cos_c, sin_c = _rope_tables(t, pos0, C_ROPE, max(t, tm))
    q, ckv, kr, krp = _proj_cq(xf, pw['w_in_c'], pw['g_q_c'], pw['g_kv_c'], cos_c, sin_c, pw['w_q_b_c'])
    q = q.reshape(b, t, C_HEADS * C_QK)
    if past is None:
        kt_c, krt_c, v_c = _kv_c(ckv, krp, pw['w_k_t_c'], pw['w_v_c'], b, min(MLA_LONG_BLOCK, t))
        attn_c = _mla_attn(q, kt_c, krt_c, v_c.reshape(b, t, C_HEADS * C_V), pos0=pos0)
    else:
        attn_c = _mla_absorbed(q, ckv.reshape(b, t, C_KV_RANK), kr.reshape(b, t, C_ROPE), past[4], past[5],
                               pw['w_k_t_c'], pw['w_v_c'], pos0=pos0)
    xf = _outproj_ln([attn_c.reshape(n, C_HEADS * C_V)], pw['w_out_c'], xf, pw['ln1_g'][1], pw['ln1_b'][1])
    xf = _channel_mix(xf, p_all, pw, 1)

    heads4 = lambda a, hh: a.reshape(1, b, a.shape[1], hh, HEAD_DIM)
    return (xf.reshape(b, t, d), heads4(ak, A_KV_HEADS), heads4(av, A_KV_HEADS), heads4(bk, B_HEADS),
            heads4(bv, B_HEADS), ckv.reshape(1, b, t, C_KV_RANK), kr.reshape(1, b, t, C_ROPE))


def kernel(x_prompt, x_sample, cache_a_k, cache_a_v, cache_b_k, cache_b_v, cache_c_kv, cache_c_krope, p_prompt,
           p_sample, w_in_ab, sinks_a, rel_bias_b, w_out_ab, w_in_c, g_q_c, w_q_b_c, g_kv_c, w_kv_b_c, w_out_c,
           ln1_g, ln1_b, ln2_g, ln2_b, w_mlp_up, w_mlp_down, w_ple_gate, b_ple_gate, w_ple):
    pw = _prepare_weights({
        'w_in_ab': w_in_ab, 'sinks_a': sinks_a, 'rel_bias_b': rel_bias_b, 'w_out_ab': w_out_ab,
        'w_in_c': w_in_c, 'g_q_c': g_q_c, 'w_q_b_c': w_q_b_c, 'g_kv_c': g_kv_c, 'w_kv_b_c': w_kv_b_c,
        'w_out_c': w_out_c, 'ln1_g': ln1_g, 'ln1_b': ln1_b, 'ln2_g': ln2_g, 'ln2_b': ln2_b,
        'w_mlp_up': w_mlp_up, 'w_mlp_down': w_mlp_down, 'w_ple_gate': w_ple_gate, 'b_ple_gate': b_ple_gate,
        'w_ple': w_ple,
    })
    prompt = _trunk(x_prompt, p_prompt, 0, None, pw)
    past = (cache_a_k[0], cache_a_v[0], cache_b_k[0], cache_b_v[0], cache_c_kv, cache_c_krope)
    sample = _trunk(x_sample, p_sample, cache_c_kv.shape[2], past, pw)
    return (prompt[0], sample[0]) + prompt[1:] + sample[1:]
```

```python
import functools

import jax
import jax.numpy as jnp
from jax import lax
from jax.experimental import pallas as pl
from jax.experimental.pallas import tpu as pltpu

F32 = jnp.float32
BF16 = jnp.bfloat16

CHUNK = 64
HEAD_DIM = 128
A_HEADS = 8
A_KV_HEADS = 2
WINDOW = 128
B_HEADS = 8
B_BAND_PAST = 512
REL_CLIP = 128
C_HEADS = 16
C_Q_RANK = 768
C_KV_RANK = 512
C_NOPE = 128
C_ROPE = 64
C_V = 128
DEPTH = 2
ROPE_THETA = 10000.0
LN_EPS = 1e-5
RMS_EPS = 1e-6
NEG_INF = -1e30
DEEPNORM_ALPHA = (2 * DEPTH) ** 0.25
LOG2_E = 1.4426950408889634
MLA_SCALE = (C_NOPE + C_ROPE) ** -0.5

A_Q_W = A_HEADS * HEAD_DIM
A_KV_W = A_KV_HEADS * HEAD_DIM
B_W = B_HEADS * HEAD_DIM
AB_IN_W = A_Q_W + 2 * A_KV_W + 3 * B_W
AB_KV_COL0 = A_Q_W + B_W
AB_KV_W = AB_IN_W - AB_KV_COL0
C_IN_W = C_Q_RANK + C_KV_RANK + C_ROPE
C_QK = 256

LANES = 128
V7X_VMEM_BYTES = 64 * 1024 * 1024
VMEM_LIMIT = V7X_VMEM_BYTES - 8 * 1024 * 1024

ROW_TILE = 512
COL_CHUNK = 512
MLP_ROW_TILE = 512
MLP_FF_TILE = 1024
BAND_B_BLOCK = 256
BAND_A_BLOCK = 128
BAND_STEP_BLOCKS = 8


def _params(n_axes):
    return pltpu.CompilerParams(dimension_semantics=("arbitrary",) * n_axes, vmem_limit_bytes=VMEM_LIMIT)


def _resident(shape):
    nd = len(shape)
    return pl.BlockSpec(shape, lambda *_: (0,) * nd, pipeline_mode=pl.Buffered(1))


def _resident_slab(shape, index):
    nd = len(shape) - 1
    return pl.BlockSpec((None,) + tuple(shape[1:]), lambda *_: (index,) + (0,) * nd, pipeline_mode=pl.Buffered(1))


def _row_tile(n):
    return ROW_TILE if n % ROW_TILE == 0 else n


def _dot(a, b):
    return jnp.dot(a, b, preferred_element_type=F32)


def _dot_t(a, b):
    return lax.dot_general(a, b, (((1,), (1,)), ((), ())), preferred_element_type=F32)


def _layer_norm(y, g, b):
    mu = jnp.mean(y, -1, keepdims=True)
    var = jnp.mean(jnp.square(y - mu), -1, keepdims=True)
    return (y - mu) * lax.rsqrt(var + LN_EPS) * g + b


def _rms_norm(y, g):
    return y * lax.rsqrt(jnp.mean(jnp.square(y), -1, keepdims=True) + RMS_EPS) * g


def _rope_tile(t, cos, sin, d):
    if d == LANES:
        swapped = pltpu.roll(t, LANES // 2, 1)
    else:
        lane = lax.broadcasted_iota(jnp.int32, t.shape, 1)
        swapped = jnp.where((lane % d) < d // 2, pltpu.roll(t, LANES - d // 2, 1), pltpu.roll(t, d // 2, 1))
    return t * cos + swapped * sin


def _rope_tables(t, pos0, d, rows):
    half = d // 2
    inv = ROPE_THETA ** (-jnp.arange(half, dtype=F32) * (2.0 / d))
    ang = (jnp.arange(t, dtype=F32) + pos0)[:, None] * inv[None, :]
    cos = jnp.cos(ang)
    sin = jnp.sin(ang)
    reps = (rows // t, LANES // d)
    return jnp.tile(jnp.concatenate([cos, cos], 1), reps), jnp.tile(jnp.concatenate([-sin, sin], 1), reps)


def _proj_ab_kernel(*refs, kv_period, keys_transposed):
    if keys_transposed:
        (x_ref, w_ref, wkt_ref, cos_ref, sin_ref, cost_ref, sint_ref,
         h_ref, kb_ref, vb_ref, ka_ref, va_ref, kbt_ref, kat_ref) = refs
    else:
        x_ref, w_ref, cos_ref, sin_ref, h_ref, kb_ref, vb_ref, ka_ref, va_ref = refs
    xb = x_ref[...].astype(BF16)
    cos = cos_ref[...]
    sin = sin_ref[...]
    keep_state = (pl.program_id(0) % kv_period) == kv_period - 1
    kv0 = AB_KV_COL0
    rope_tiles = set(range(A_HEADS))
    if keys_transposed:
        state_refs = [(kv0, vb_ref), (kv0 + B_W, va_ref)]
    else:
        state_refs = [(kv0, kb_ref), (kv0 + B_W, vb_ref), (kv0 + 2 * B_W, ka_ref), (kv0 + 2 * B_W + A_KV_W, va_ref)]
        rope_tiles |= {(kv0 + 2 * B_W) // LANES + u for u in range(A_KV_HEADS)}
    if keys_transposed:
        k_t = _dot_t(wkt_ref[...], xb)
        kb_t, ka_t = k_t[:B_W], k_t[B_W:]
        cos_t = cost_ref[...]
        sin_t = sint_ref[...]
        half = HEAD_DIM // 2
        rotated = []
        for u in range(A_KV_HEADS):
            t = ka_t[u * HEAD_DIM:(u + 1) * HEAD_DIM]
            rotated.append(t * cos_t + jnp.concatenate([t[half:], t[:half]], axis=0) * sin_t)
        ka_t = jnp.concatenate(rotated, axis=0)
        for t_val, t_ref in ((kb_t, kbt_ref), (ka_t, kat_ref)):
            blk = t_ref.shape[3]
            for c in range(t_ref.shape[1]):
                t_ref[0, c] = t_val[:, c * blk:(c + 1) * blk].astype(t_ref.dtype)
    width = w_ref.shape[1]
    for c0 in range(0, width, COL_CHUNK):
        cw = min(COL_CHUNK, width - c0)
        acc = _dot(xb, w_ref[:, c0:c0 + cw])
        parts = []
        for u in range(cw // LANES):
            part = acc[:, u * LANES:(u + 1) * LANES]
            if c0 // LANES + u in rope_tiles:
                part = _rope_tile(part, cos, sin, HEAD_DIM)
            parts.append(part)
        acc = jnp.concatenate(parts, axis=1)
        h_ref[:, c0:c0 + cw] = acc.astype(h_ref.dtype)
        for s0, ref in state_refs:
            lo, hi = max(c0, s0), min(c0 + cw, s0 + ref.shape[1])
            if lo < hi:
                @pl.when(keep_state)
                def _(acc=acc, ref=ref, lo=lo, hi=hi, s0=s0, c0=c0):
                    ref[:, lo - s0:hi - s0] = acc[:, lo - c0:hi - c0]
    if keys_transposed:
        @pl.when(keep_state)
        def _():
            kb_ref[...] = kb_t.T
            ka_ref[...] = ka_t.T


def _proj_ab(xf, w, seq, pos0, w_k_t=None, kb_block=None, ka_block=None):
    n, d = xf.shape
    tm = _row_tile(n)
    assert seq % tm == 0 or tm % seq == 0
    kv_period = max(seq // tm, 1)
    assert min(seq, B_BAND_PAST) == min(seq, tm)
    cos, sin = _rope_tables(seq, pos0, HEAD_DIM, max(seq, tm))
    n_tab = cos.shape[0] // tm
    state_widths = (B_W, B_W, A_KV_W, A_KV_W)
    keys_transposed = w_k_t is not None
    rows = lambda i: (i, 0)
    tab = lambda i: (i % n_tab, 0)
    in_specs = [pl.BlockSpec((tm, d), rows), _resident(w.shape)]
    args = [xf, w]
    out_specs = [pl.BlockSpec((tm, w.shape[1]), rows)] + [
        pl.BlockSpec((tm, sw), lambda i: (i // kv_period, 0)) for sw in state_widths]
    out_shape = [jax.ShapeDtypeStruct((n, w.shape[1]), BF16)] + [
        jax.ShapeDtypeStruct((n // kv_period, sw), F32) for sw in state_widths]
    if keys_transposed:
        in_specs.append(_resident(w_k_t.shape))
        args.append(w_k_t)
    in_specs += [pl.BlockSpec((tm, LANES), tab), pl.BlockSpec((tm, LANES), tab)]
    args += [cos, sin]
    if keys_transposed:
        assert seq % tm == 0 and tm % kb_block == 0 and tm % ka_block == 0
        tiles = seq // tm
        tab_t = lambda i: (0, i % n_tab)
        in_specs += [pl.BlockSpec((LANES, tm), tab_t), pl.BlockSpec((LANES, tm), tab_t)]
        args += [cos.T, sin.T]
        for width, blk in ((B_W, kb_block), (A_KV_W, ka_block)):
            out_specs.append(pl.BlockSpec((1, tm // blk, width, blk), lambda i: (i // tiles, i % tiles, 0, 0)))
            out_shape.append(jax.ShapeDtypeStruct((n // seq, seq // blk, width, blk), BF16))
    return pl.pallas_call(
        functools.partial(_proj_ab_kernel, kv_period=kv_period, keys_transposed=keys_transposed),
        grid=(n // tm,),
        in_specs=in_specs,
        out_specs=out_specs,
        out_shape=out_shape,
        compiler_params=_params(1),
        name="proj_ab",
    )(*args)


def _band_attn_kernel(*refs, heads, kv_heads, band, tq, past, has_bias, has_sinks):
    q_ref, k_ref, v_ref = refs[:3]
    rest = list(refs[3:])
    bias_ref = rest.pop(0) if has_bias else None
    sink_ref = rest.pop(0) if has_sinks else None
    o_ref = rest.pop(0)

    width = band + tq
    scale = HEAD_DIM ** -0.5 * LOG2_E
    ws = past + pl.program_id(1) * tq - band
    r = lax.broadcasted_iota(jnp.int32, (tq, width), 0) // CHUNK
    w = lax.broadcasted_iota(jnp.int32, (tq, width), 1)
    wc = w // CHUNK
    allowed = (wc >= r) & (wc <= r + band // CHUNK) & (w + ws >= 0)

    def window(ref, cols):
        if past >= band:
            return ref[0, pl.ds(pl.multiple_of(ws, CHUNK), width), cols]
        pieces = [ref[0, pl.ds(pl.multiple_of(jnp.maximum(ws + c * LANES, 0), LANES), LANES), cols]
                  for c in range(width // LANES)]
        return jnp.concatenate(pieces, axis=0)

    group = heads // kv_heads
    outs = []
    for kh in range(kv_heads):
        cols = slice(kh * HEAD_DIM, (kh + 1) * HEAD_DIM)
        k_w = window(k_ref, cols)
        v_w = window(v_ref, cols)
        for g in range(group):
            h = kh * group + g
            hcols = slice(h * HEAD_DIM, (h + 1) * HEAD_DIM)
            s = _dot_t(q_ref[0, :, hcols], k_w) * scale
            if has_bias:
                s = s + bias_ref[h]
            s = jnp.where(allowed, s, NEG_INF)
            m = jnp.max(s, -1, keepdims=True)
            if has_sinks:
                sink = sink_ref[h] * LOG2_E
                m = jnp.maximum(m, sink)
            e = jnp.exp2(s - m)
            den = jnp.sum(e, -1, keepdims=True)
            if has_sinks:
                den = den + jnp.exp2(sink - m)
            p = (e * (1.0 / den)).astype(BF16)
            outs.append(_dot(p, v_w).astype(o_ref.dtype))
    o_ref[0] = jnp.concatenate(outs, axis=1)


def _band_attn(q, k, v, *, heads, kv_heads, band, tq, past, bias=None, sinks=None):
    (qa, qw, qi), (ka, kw, ki), (va, vw, vi) = q, k, v
    b, t, _ = qa.shape
    s_len = ka.shape[1]
    assert t % tq == 0 and s_len == past + t and qw == heads * HEAD_DIM and kw == kv_heads * HEAD_DIM
    assert past >= band or (past == 0 and tq % LANES == 0 and band % LANES == 0)
    in_specs = [
        pl.BlockSpec((1, tq, qw), lambda bi, i: (bi, i, qi)),
        pl.BlockSpec((1, s_len, kw), lambda bi, i: (bi, 0, ki)),
        pl.BlockSpec((1, s_len, vw), lambda bi, i: (bi, 0, vi)),
    ]
    args = [qa, ka, va]
    if bias is not None:
        in_specs.append(_resident(bias.shape))
        args.append(bias)
    if sinks is not None:
        in_specs.append(pl.BlockSpec(memory_space=pltpu.SMEM))
        args.append(sinks)
    return pl.pallas_call(
        functools.partial(_band_attn_kernel, heads=heads, kv_heads=kv_heads, band=band, tq=tq, past=past,
                          has_bias=bias is not None, has_sinks=sinks is not None),
        grid=(b, t // tq),
        in_specs=in_specs,
        out_specs=pl.BlockSpec((1, tq, qw), lambda bi, i: (bi, i, 0)),
        out_shape=jax.ShapeDtypeStruct((b, t, qw), BF16),
        compiler_params=_params(2),
        name="band_attn",
    )(*args)


def _rel_bias_folded_kernel(tab_ref, o_ref, *, band):
    h = pl.program_id(0)
    tq, width = o_ref.shape[1:]
    period = tq + width
    j = lax.broadcasted_iota(jnp.int32, (8, period), 1)
    dist = jnp.where(j < width, band - j, band - (j - period))
    idx = jnp.clip(dist, -REL_CLIP, REL_CLIP) + REL_CLIP

    def body(d, acc):
        return jnp.where(idx == d, tab_ref[h, d], acc)

    g = lax.fori_loop(0, 2 * REL_CLIP + 1, body, jnp.zeros((8, period), F32))
    full = jnp.concatenate([g] * (tq // 8), axis=0)
    bias = pltpu.roll(full, 0, 1, stride=1, stride_axis=0)[:, :width]
    rc = lax.broadcasted_iota(jnp.int32, (tq, width), 0) // CHUNK
    wc = lax.broadcasted_iota(jnp.int32, (tq, width), 1) // CHUNK
    o_ref[0] = jnp.where((wc >= rc) & (wc <= rc + band // CHUNK), bias * LOG2_E, NEG_INF)


def _rel_bias_folded(table, band, tq):
    heads = table.shape[0]
    assert (band + 2 * tq) % LANES == 0 and tq % 8 == 0
    return pl.pallas_call(
        functools.partial(_rel_bias_folded_kernel, band=band),
        grid=(heads,),
        in_specs=[pl.BlockSpec(memory_space=pltpu.SMEM)],
        out_specs=pl.BlockSpec((1, tq, band + tq), lambda h: (h, 0, 0)),
        out_shape=jax.ShapeDtypeStruct((heads, tq, band + tq), F32),
        compiler_params=_params(1),
        name="rel_bias_folded",
    )(table)


def _band_attn_t_kernel(*refs, heads, kv_heads, band, has_bias, has_sinks):
    q_ref, kt_ref, v_ref = refs[:3]
    rest = list(refs[3:])
    bias_ref = rest.pop(0) if has_bias else None
    sink_ref = rest.pop(0) if has_sinks else None
    o_ref = rest.pop(0)
    tq = kt_ref.shape[3]
    n_past = band // tq
    width = band + tq
    group = heads // kv_heads
    static_ok = None
    if not has_bias:
        rc = (lax.broadcasted_iota(jnp.int32, (group * tq, width), 0) % tq) // CHUNK
        wc = lax.broadcasted_iota(jnp.int32, (group * tq, width), 1) // CHUNK
        static_ok = (wc >= rc) & (wc <= rc + band // CHUNK)
    ones = jnp.ones((width, HEAD_DIM), BF16)
    for sub in range(q_ref.shape[1] // tq):
        i = pl.program_id(1) * (q_ref.shape[1] // tq) + sub
        rows = slice(sub * tq, (sub + 1) * tq)
        ws = i * tq - band
        allowed = (lax.broadcasted_iota(jnp.int32, (1, width), 1) + ws) >= 0
        if static_ok is not None:
            allowed = allowed & static_ok
        outs = []
        for kh in range(kv_heads):
            kc = slice(kh * HEAD_DIM, (kh + 1) * HEAD_DIM)
            hs = range(kh * group, (kh + 1) * group)
            q_st = jnp.concatenate([q_ref[0, rows, h * HEAD_DIM:(h + 1) * HEAD_DIM] for h in hs], axis=0)
            k_t = jnp.concatenate([kt_ref[0, jnp.maximum(i - n_past + c, 0), kc, :] for c in range(n_past + 1)],
                                  axis=1)
            s = _dot(q_st, k_t) * (HEAD_DIM ** -0.5 * LOG2_E)
            if has_bias:
                s = s + jnp.concatenate([bias_ref[h] for h in hs], axis=0)
            s = jnp.where(allowed, s, NEG_INF)
            ms, ps = [], []
            for g, h in enumerate(hs):
                s_g = s[g * tq:(g + 1) * tq]
                m_g = jnp.max(s_g, -1, keepdims=True)
                if has_sinks:
                    m_g = jnp.maximum(m_g, sink_ref[h] * LOG2_E)
                ms.append(m_g)
                ps.append(jnp.exp2(s_g - m_g).astype(BF16))
            v_w = jnp.concatenate([v_ref[0, pl.ds(pl.multiple_of(jnp.maximum(ws + c * tq, 0), tq), tq), kc]
                                   for c in range(n_past + 1)], axis=0)
            o_ext = _dot(jnp.concatenate(ps, axis=0), jnp.concatenate([v_w, ones], axis=1))
            for g, h in enumerate(hs):
                o_g = o_ext[g * tq:(g + 1) * tq]
                den = o_g[:, HEAD_DIM:]
                if has_sinks:
                    den = den + jnp.exp2(sink_ref[h] * LOG2_E - ms[g])
                outs.append((o_g[:, :HEAD_DIM] * (1.0 / den)).astype(o_ref.dtype))
        o_ref[0, rows, :] = jnp.concatenate(outs, axis=1)


def _band_attn_t(q, kt, v, *, heads, kv_heads, band, bias=None, sinks=None):
    (qa, qw, qi), (va, vw, vi) = q, v
    b, t, _ = qa.shape
    _, nkb, _, tq = kt.shape
    assert nkb * tq == t and band % tq == 0 and qw == heads * HEAD_DIM and vw == kv_heads * HEAD_DIM
    assert bias is None or heads == kv_heads
    rows = BAND_STEP_BLOCKS * tq if t % (BAND_STEP_BLOCKS * tq) == 0 else tq
    in_specs = [
        pl.BlockSpec((1, rows, qw), lambda bi, i: (bi, i, qi)),
        pl.BlockSpec((1, nkb, kv_heads * HEAD_DIM, tq), lambda bi, i: (bi, 0, 0, 0)),
        pl.BlockSpec((1, t, vw), lambda bi, i: (bi, 0, vi)),
    ]
    args = [qa, kt, va]
    if bias is not None:
        in_specs.append(_resident(bias.shape))
        args.append(bias)
    if sinks is not None:
        in_specs.append(pl.BlockSpec(memory_space=pltpu.SMEM))
        args.append(sinks)
    return pl.pallas_call(
        functools.partial(_band_attn_t_kernel, heads=heads, kv_heads=kv_heads, band=band,
                          has_bias=bias is not None, has_sinks=sinks is not None),
        grid=(b, t // rows),
        in_specs=in_specs,
        out_specs=pl.BlockSpec((1, rows, qw), lambda bi, i: (bi, i, 0)),
        out_shape=jax.ShapeDtypeStruct((b, t, qw), BF16),
        compiler_params=_params(2),
        name="band_attn_t",
    )(*args)


def _outproj_ln_kernel(*refs, n_in):
    a_refs = refs[:n_in]
    w_ref, x_ref, g_ref, b_ref, o_ref, ob_ref = refs[n_in:]
    d_out = o_ref.shape[1]
    for j in range(d_out // COL_CHUNK):
        cols = slice(j * COL_CHUNK, (j + 1) * COL_CHUNK)
        y = DEEPNORM_ALPHA * x_ref[:, cols]
        r0 = 0
        for a_ref in a_refs:
            kk = a_ref.shape[1]
            y = y + _dot(a_ref[...], w_ref[r0:r0 + kk, cols])
            r0 += kk
        o_ref[:, cols] = y
    y = _layer_norm(o_ref[...], g_ref[...], b_ref[...])
    o_ref[...] = y
    ob_ref[...] = y.astype(ob_ref.dtype)


def _outproj_ln(a_list, w, xf, g, b):
    n, d = xf.shape
    tm = _row_tile(n)
    assert sum(a.shape[1] for a in a_list) == w.shape[0]
    return pl.pallas_call(
        functools.partial(_outproj_ln_kernel, n_in=len(a_list)),
        grid=(n // tm,),
        in_specs=[pl.BlockSpec((tm, a.shape[1]), lambda i: (i, 0)) for a in a_list] + [
            _resident(w.shape),
            pl.BlockSpec((tm, d), lambda i: (i, 0)),
            _resident(g.shape),
            _resident(b.shape),
        ],
        out_specs=[pl.BlockSpec((tm, d), lambda i: (i, 0)), pl.BlockSpec((tm, d), lambda i: (i, 0))],
        out_shape=[jax.ShapeDtypeStruct((n, d), F32), jax.ShapeDtypeStruct((n, d), BF16)],
        compiler_params=_params(1),
        name="outproj_ln",
    )(*a_list, w, xf, g, b)


def _channel_mix_kernel(x_ref, xin_ref, wu_ref, wd_ref, g_ref, b_ref, p_ref, wg_ref, bg_ref, wp_ref, o_ref, xb_ref):
    f = pl.program_id(1)

    @pl.when(f == 0)
    def _():
        o_ref[...] = jnp.zeros(o_ref.shape, o_ref.dtype)

    hid = _dot(xin_ref[...], wu_ref[...])
    hid = jnp.square(jnp.maximum(hid, 0.0)).astype(BF16)
    for j in range(o_ref.shape[1] // COL_CHUNK):
        cols = slice(j * COL_CHUNK, (j + 1) * COL_CHUNK)
        o_ref[:, cols] += _dot(hid, wd_ref[:, cols])

    @pl.when(f == pl.num_programs(1) - 1)
    def _():
        o_ref[...] = _layer_norm(DEEPNORM_ALPHA * x_ref[...] + o_ref[...], g_ref[...], b_ref[...])
        xb_ref[...] = o_ref[...].astype(BF16)
        pb = p_ref[...].astype(BF16)
        for j in range(o_ref.shape[1] // COL_CHUNK):
            cols = slice(j * COL_CHUNK, (j + 1) * COL_CHUNK)
            gate = jax.nn.sigmoid(_dot(xb_ref[...], wg_ref[:, cols]) + bg_ref[:, cols])
            o_ref[:, cols] = o_ref[:, cols] + gate * _dot(pb, wp_ref[:, cols])


def _channel_mix(xf, xb, p_all, pw, layer):
    w_up, w_down, wg, wp = pw['w_mlp_up'], pw['w_mlp_down'], pw['w_ple_gate'], pw['w_ple']
    g, b, bg = pw['ln2_g'][layer], pw['ln2_b'][layer], pw['b_ple_gate'][layer]
    n, d = xf.shape
    d_ff = w_up.shape[2]
    tm = MLP_ROW_TILE if n % MLP_ROW_TILE == 0 else n
    tf = MLP_FF_TILE
    return pl.pallas_call(
        _channel_mix_kernel,
        grid=(n // tm, d_ff // tf),
        in_specs=[
            pl.BlockSpec((tm, d), lambda i, f: (i, 0)),
            pl.BlockSpec((tm, d), lambda i, f: (i, 0)),
            pl.BlockSpec((None, d, tf), lambda i, f: (layer, 0, f)),
            pl.BlockSpec((None, tf, d), lambda i, f: (layer, f, 0)),
            _resident(g.shape),
            _resident(b.shape),
            pl.BlockSpec((None, tm, p_all.shape[2]), lambda i, f: (layer, i, 0)),
            _resident_slab(wg.shape, layer),
            _resident(bg.shape),
            _resident_slab(wp.shape, layer),
        ],
        out_specs=pl.BlockSpec((tm, d), lambda i, f: (i, 0)),
        out_shape=jax.ShapeDtypeStruct((n, d), F32),
        scratch_shapes=[pltpu.VMEM((tm, d), BF16)],
        compiler_params=_params(2),
        name="channel_mix",
    )(xf, xb, w_up, w_down, g, b, p_all, wg, bg, wp)


Q_GROUP = 4


def _proj_cq_kernel(x_ref, w_ref, gq_ref, gkv_ref, cos_ref, sin_ref, wq_ref, q_ref, ckv_ref, kr_ref, krp_ref):
    cos = cos_ref[...]
    sin = sin_ref[...]
    h = _dot(x_ref[...].astype(BF16), w_ref[...])
    cq = _rms_norm(h[:, :C_Q_RANK], gq_ref[...]).astype(BF16)
    ckv_ref[...] = _rms_norm(h[:, C_Q_RANK:C_Q_RANK + C_KV_RANK], gkv_ref[...])
    rot = _rope_tile(h[:, C_Q_RANK + C_KV_RANK:], cos, sin, C_ROPE)
    lane = lax.broadcasted_iota(jnp.int32, rot.shape, 1)
    rot = jnp.where(lane < C_ROPE, rot, 0.0)
    kr_ref[...] = rot[:, :C_ROPE]
    krp_ref[...] = rot.astype(krp_ref.dtype)

    grp_w = Q_GROUP * (C_NOPE + C_ROPE)
    for gi in range(C_HEADS // Q_GROUP):
        acc = _dot(cq, wq_ref[:, gi * grp_w:(gi + 1) * grp_w]) * (MLA_SCALE * LOG2_E)
        for u in range(Q_GROUP):
            o0 = (gi * Q_GROUP + u) * C_QK
            q_ref[:, o0:o0 + C_NOPE] = acc[:, u * C_NOPE:(u + 1) * C_NOPE].astype(q_ref.dtype)
            if u % 2 == 0:
                r0 = Q_GROUP * C_NOPE + (u // 2) * LANES
                qrot = _rope_tile(acc[:, r0:r0 + LANES], cos, sin, C_ROPE)
                piece = qrot
            else:
                piece = pltpu.roll(qrot, C_ROPE, 1)
            q_ref[:, o0 + C_NOPE:o0 + C_QK] = jnp.where(lane < C_ROPE, piece, 0.0).astype(q_ref.dtype)


def _proj_cq(xf, w, gq, gkv, cos, sin, wq):
    n, d = xf.shape
    tm = _row_tile(n)
    n_tab = cos.shape[0] // tm
    rows = lambda i: (i, 0)
    tab = lambda i: (i % n_tab, 0)
    return pl.pallas_call(
        _proj_cq_kernel,
        grid=(n // tm,),
        in_specs=[
            pl.BlockSpec((tm, d), rows),
            _resident(w.shape),
            _resident(gq.shape),
            _resident(gkv.shape),
            pl.BlockSpec((tm, LANES), tab),
            pl.BlockSpec((tm, LANES), tab),
            _resident(wq.shape),
        ],
        out_specs=[
            pl.BlockSpec((tm, C_HEADS * C_QK), rows),
            pl.BlockSpec((tm, C_KV_RANK), rows),
            pl.BlockSpec((tm, C_ROPE), rows),
            pl.BlockSpec((tm, LANES), rows),
        ],
        out_shape=[
            jax.ShapeDtypeStruct((n, C_HEADS * C_QK), BF16),
            jax.ShapeDtypeStruct((n, C_KV_RANK), F32),
            jax.ShapeDtypeStruct((n, C_ROPE), F32),
            jax.ShapeDtypeStruct((n, LANES), BF16),
        ],
        compiler_params=_params(1),
        name="proj_cq",
    )(xf, w, gq, gkv, cos, sin, wq)


def _kv_c_kernel(ckv_ref, krp_ref, wkt_ref, wv_ref, kt_ref, krt_ref, v_ref):
    cb = ckv_ref[...].astype(BF16)
    eye = (lax.broadcasted_iota(jnp.int32, (LANES, LANES), 0)
           == lax.broadcasted_iota(jnp.int32, (LANES, LANES), 1)).astype(F32).astype(BF16)
    krt_ref[0, 0] = _dot_t(eye, krp_ref[...]).astype(krt_ref.dtype)
    kt_ref[0, 0] = _dot_t(wkt_ref[...], cb).astype(kt_ref.dtype)
    for j in range(C_HEADS * C_V // COL_CHUNK):
        cols = slice(j * COL_CHUNK, (j + 1) * COL_CHUNK)
        v_ref[:, cols] = _dot(cb, wv_ref[:, cols]).astype(v_ref.dtype)


def _kv_c(ckv, krp, wkt, wv, batch, tk):
    n = ckv.shape[0]
    nkb = n // batch // tk
    assert n == batch * nkb * tk
    rows = lambda i: (i, 0)
    blk = lambda i: (i // nkb, i % nkb, 0, 0)
    return pl.pallas_call(
        _kv_c_kernel,
        grid=(n // tk,),
        in_specs=[pl.BlockSpec((tk, C_KV_RANK), rows), pl.BlockSpec((tk, LANES), rows), _resident(wkt.shape),
                  _resident(wv.shape)],
        out_specs=[pl.BlockSpec((1, 1, C_HEADS * C_NOPE, tk), blk), pl.BlockSpec((1, 1, LANES, tk), blk),
                   pl.BlockSpec((tk, C_HEADS * C_V), rows)],
        out_shape=[
            jax.ShapeDtypeStruct((batch, nkb, C_HEADS * C_NOPE, tk), BF16),
            jax.ShapeDtypeStruct((batch, nkb, LANES, tk), BF16),
            jax.ShapeDtypeStruct((n, C_HEADS * C_V), BF16),
        ],
        compiler_params=_params(1),
        name="kv_c",
    )(ckv, krp, wkt, wv)


MLA_LONG_BLOCK = 512


def _mla_attn_kernel(q_ref, kt_ref, krt_ref, v_ref, o_ref, *, pos0, heads):
    t = q_ref.shape[1]
    tq = tk = kt_ref.shape[3]
    row = lax.broadcasted_iota(jnp.int32, (tq, tk), 0) // CHUNK
    col = lax.broadcasted_iota(jnp.int32, (tq, tk), 1) // CHUNK
    diag_ok = col <= row
    ones = jnp.ones((tk, C_V), BF16)

    def k_slab(g, kb):
        return jnp.concatenate([kt_ref[0, kb, g * C_NOPE:(g + 1) * C_NOPE, :], krt_ref[0, kb]], axis=0)

    def v_slab(g, kb):
        return jnp.concatenate([v_ref[0, kb * tk:(kb + 1) * tk, g * C_V:(g + 1) * C_V], ones], axis=1)

    def step(s, m, acc, v_ext):
        m_new = jnp.maximum(m, jnp.max(s, -1, keepdims=True))
        p = jnp.exp2(s - m_new).astype(BF16)
        return m_new, jnp.exp2(m - m_new) * acc + _dot(p, v_ext)

    def finish(acc):
        return (acc[:, :C_V] * (1.0 / acc[:, C_V:])).astype(o_ref.dtype)

    for qi in range(t // tq):
        q0 = qi * tq
        qs = [q_ref[0, q0:q0 + tq, g * C_QK:(g + 1) * C_QK] for g in range(heads)]
        n_full = (pos0 + q0) // tk
        state = [(jnp.full((tq, 1), NEG_INF, F32), jnp.zeros((tq, 2 * C_V), F32)) for _ in range(heads)]
        for kb in range(n_full):
            raw = [_dot(qs[g], k_slab(g, kb)) for g in range(heads)]
            state = [step(raw[g], *state[g], v_slab(g, kb)) for g in range(heads)]
        raw = [jnp.where(diag_ok, _dot(qs[g], k_slab(g, n_full)), NEG_INF) for g in range(heads)]
        state = [step(raw[g], *state[g], v_slab(g, n_full)) for g in range(heads)]
        for g in range(heads):
            o_ref[0, q0:q0 + tq, g * C_V:(g + 1) * C_V] = finish(state[g][1])


def _mla_attn(q, kt, krt, v, *, pos0, heads=4):
    b, t, _ = q.shape
    _, nkb, _, tk = kt.shape
    s_len = nkb * tk
    assert t % tk == 0 and pos0 % tk == 0 and pos0 + t <= s_len and v.shape[1] == s_len
    return pl.pallas_call(
        functools.partial(_mla_attn_kernel, pos0=pos0, heads=heads),
        grid=(b, C_HEADS // heads),
        in_specs=[
            pl.BlockSpec((1, t, heads * C_QK), lambda bi, h: (bi, 0, h)),
            pl.BlockSpec((1, nkb, heads * C_NOPE, tk), lambda bi, h: (bi, 0, h, 0)),
            pl.BlockSpec((1, nkb, LANES, tk), lambda bi, h: (bi, 0, 0, 0)),
            pl.BlockSpec((1, s_len, heads * C_V), lambda bi, h: (bi, 0, h)),
        ],
        out_specs=pl.BlockSpec((1, t, heads * C_V), lambda bi, h: (bi, 0, h)),
        out_shape=jax.ShapeDtypeStruct((b, t, C_HEADS * C_V), BF16),
        compiler_params=_params(2),
        name="mla_attn",
    )(q, kt, krt, v)


def _mla_absorbed_kernel(q_ref, ckv_new_ref, kr_new_ref, ckv_past_ref, kr_past_ref, wkt_ref, wv_ref, o_ref):
    t = q_ref.shape[1]
    q_lat, q_rope = [], []
    for h in range(C_HEADS):
        q_lat.append(_dot(q_ref[0, :, h * C_QK:h * C_QK + C_NOPE], wkt_ref[h * C_NOPE:(h + 1) * C_NOPE, :]))
        q_rope.append(q_ref[0, :, h * C_QK + C_NOPE:h * C_QK + C_NOPE + C_ROPE])
    q_lat = jnp.concatenate(q_lat, axis=0).astype(BF16)
    q_rope = jnp.concatenate(q_rope, axis=0)
    lat = [ckv_past_ref[...].astype(BF16), ckv_new_ref[0].astype(BF16)]
    rot = [kr_past_ref[...].astype(BF16), kr_new_ref[0].astype(BF16)]
    s = jnp.concatenate([_dot_t(q_lat, c) + _dot_t(q_rope, r) for c, r in zip(lat, rot)], axis=1)
    e = jnp.exp2(s - jnp.max(s, -1, keepdims=True))
    p = (e * (1.0 / jnp.sum(e, -1, keepdims=True))).astype(BF16)
    n_past = lat[0].shape[0]
    o_lat = (_dot(p[:, :n_past], lat[0]) + _dot(p[:, n_past:], lat[1])).astype(BF16)
    o_ref[0] = jnp.concatenate([_dot(o_lat[h * t:(h + 1) * t], wv_ref[:, h * C_V:(h + 1) * C_V])
                                for h in range(C_HEADS)], axis=1).astype(o_ref.dtype)


def _mla_absorbed(q, ckv_new, kr_new, ckv_past, kr_past, wkt, wv, *, pos0):
    b, t, _ = q.shape
    n_past = ckv_past.shape[2]
    assert t == CHUNK and pos0 % CHUNK == 0 and n_past <= pos0
    return pl.pallas_call(
        _mla_absorbed_kernel,
        grid=(b,),
        in_specs=[
            pl.BlockSpec((1, t, C_HEADS * C_QK), lambda bi: (bi, 0, 0)),
            pl.BlockSpec((1, t, C_KV_RANK), lambda bi: (bi, 0, 0)),
            pl.BlockSpec((1, t, C_ROPE), lambda bi: (bi, 0, 0)),
            pl.BlockSpec((None, None, n_past, C_KV_RANK), lambda bi: (0, bi, 0, 0)),
            pl.BlockSpec((None, None, n_past, C_ROPE), lambda bi: (0, bi, 0, 0)),
            _resident(wkt.shape),
            _resident(wv.shape),
        ],
        out_specs=pl.BlockSpec((1, t, C_HEADS * C_V), lambda bi: (bi, 0, 0)),
        out_shape=jax.ShapeDtypeStruct((b, t, C_HEADS * C_V), BF16),
        compiler_params=_params(1),
        name="mla_absorbed",
    )(q, ckv_new, kr_new, ckv_past, kr_past, wkt, wv)


def _prepare_weights(w):
    o1 = A_Q_W
    o2 = o1 + A_KV_W
    o3 = o2 + A_KV_W
    o4 = o3 + B_W
    o5 = o4 + B_W
    w_ab = w['w_in_ab'][0]
    w_ab = jnp.concatenate([w_ab[:, :o1], w_ab[:, o3:o4], w_ab[:, o4:o5], w_ab[:, o5:], w_ab[:, o1:o2],
                            w_ab[:, o2:o3]], axis=1)
    ka0 = AB_KV_COL0 + 2 * B_W
    w_ab_long = jnp.concatenate([w_ab[:, :AB_KV_COL0], w_ab[:, AB_KV_COL0 + B_W:ka0], w_ab[:, ka0 + A_KV_W:]], axis=1)
    w_k_t_ab = jnp.concatenate([w_ab[:, AB_KV_COL0:AB_KV_COL0 + B_W], w_ab[:, ka0:ka0 + A_KV_W]], axis=1).T
    w_c = jnp.pad(w['w_in_c'][0], ((0, 0), (0, LANES - C_ROPE)))
    hq = C_NOPE + C_ROPE
    q_cols = []
    for g0 in range(0, C_HEADS, Q_GROUP):
        q_cols += [jnp.arange(h * hq, h * hq + C_NOPE) for h in range(g0, g0 + Q_GROUP)]
        q_cols += [jnp.arange(h * hq + C_NOPE, (h + 1) * hq) for h in range(g0, g0 + Q_GROUP)]
    w_q = w['w_q_b_c'][0][:, jnp.concatenate(q_cols)]
    hkv = C_NOPE + C_V
    k_cols = jnp.concatenate([jnp.arange(h * hkv, h * hkv + C_NOPE) for h in range(C_HEADS)])
    v_cols = jnp.concatenate([jnp.arange(h * hkv + C_NOPE, (h + 1) * hkv) for h in range(C_HEADS)])
    w_k_t = w['w_kv_b_c'][0][:, k_cols].T
    w_v = w['w_kv_b_c'][0][:, v_cols]
    row = lambda a: a.reshape(1, -1)
    return {
        'w_in_ab': w_ab.astype(BF16), 'w_in_ab_long': w_ab_long.astype(BF16), 'w_k_t_ab': w_k_t_ab.astype(BF16),
        'w_out_ab': w['w_out_ab'][0].astype(BF16),
        'w_in_c': w_c.astype(BF16), 'w_q_b_c': w_q.astype(BF16), 'w_k_t_c': w_k_t.astype(BF16),
        'w_v_c': w_v.astype(BF16),
        'w_out_c': w['w_out_c'][0].astype(BF16),
        'g_q_c': row(w['g_q_c'][0]), 'g_kv_c': row(w['g_kv_c'][0]),
        'sinks_a': w['sinks_a'][0], 'rel_bias_b': w['rel_bias_b'][0],
        'ln1_g': [row(w['ln1_g'][i]) for i in range(DEPTH)], 'ln1_b': [row(w['ln1_b'][i]) for i in range(DEPTH)],
        'ln2_g': [row(w['ln2_g'][i]) for i in range(DEPTH)], 'ln2_b': [row(w['ln2_b'][i]) for i in range(DEPTH)],
        'w_mlp_up': w['w_mlp_up'].astype(BF16), 'w_mlp_down': w['w_mlp_down'].astype(BF16),
        'w_ple_gate': w['w_ple_gate'].astype(BF16), 'w_ple': w['w_ple'].astype(BF16),
        'b_ple_gate': [row(w['b_ple_gate'][i]) for i in range(DEPTH)],
    }


def _trunk(x, p, pos0, past, pw):
    b, t, d = x.shape
    n = b * t
    tm = _row_tile(n)
    xf = x.reshape(n, d)

    if past is None:
        h, kb_new, vb_new, ka_new, va_new, kb_t, ka_t = _proj_ab(
            xf, pw['w_in_ab_long'], t, pos0, pw['w_k_t_ab'], BAND_B_BLOCK, BAND_A_BLOCK)
    else:
        h, kb_new, vb_new, ka_new, va_new = _proj_ab(xf, pw['w_in_ab'], t, pos0)
    h3 = h.reshape(b, t, h.shape[1])
    kb_new, vb_new, ka_new, va_new = [a.reshape(b, a.shape[0] // b, a.shape[1])
                                      for a in (kb_new, vb_new, ka_new, va_new)]
    q_a = (h3, A_Q_W, 0)
    q_b = (h3, B_W, 1)
    if past is None:
        attn_a = _band_attn_t(q_a, ka_t, (h3, A_KV_W, (AB_KV_COL0 + B_W) // A_KV_W), heads=A_HEADS,
                              kv_heads=A_KV_HEADS, band=WINDOW, sinks=pw['sinks_a'])
        bias_b = _rel_bias_folded(pw['rel_bias_b'], B_BAND_PAST, BAND_B_BLOCK)
        attn_b = _band_attn_t(q_b, kb_t, (h3, B_W, 2), heads=B_HEADS, kv_heads=B_HEADS, band=B_BAND_PAST,
                              bias=bias_b)
        ak, av, bk, bv = ka_new[:, -WINDOW:], va_new[:, -WINDOW:], kb_new, vb_new
    else:
        n_past_a, n_past_b = past[0].shape[1], past[2].shape[1]
        news = (ka_new, va_new, kb_new, vb_new)
        caps = (WINDOW, WINDOW, B_BAND_PAST, B_BAND_PAST)
        ak, av, bk, bv = [
            jnp.concatenate([c, new.reshape(b, t, c.shape[2], HEAD_DIM)], axis=1)[:, -min(cap, c.shape[1] + t):]
            for c, new, cap in zip(past[:4], news, caps)]
        k_a, v_a, k_b, v_b = [
            (jnp.concatenate([c.reshape(b, c.shape[1], -1).astype(BF16), new.astype(BF16)], axis=1), new.shape[2], 0)
            for c, new in zip(past[:4], news)]
        attn_a = _band_attn(q_a, k_a, v_a, heads=A_HEADS, kv_heads=A_KV_HEADS, band=WINDOW, tq=CHUNK,
                            past=n_past_a, sinks=pw['sinks_a'])
        bias_b = _rel_bias_folded(pw['rel_bias_b'], B_BAND_PAST, CHUNK)
        attn_b = _band_attn(q_b, k_b, v_b, heads=B_HEADS, kv_heads=B_HEADS, band=B_BAND_PAST, tq=CHUNK,
                            past=n_past_b, bias=bias_b)
    xf, xb = _outproj_ln([attn_a.reshape(n, A_Q_W), attn_b.reshape(n, B_W)], pw['w_out_ab'], xf,
                         pw['ln1_g'][0], pw['ln1_b'][0])
    p_all = p.reshape(p.shape[0], n, p.shape[3])
    xf = _channel_mix(xf, xb, p_all, pw, 0)

    cos_c, sin_c = _rope_tables(t, pos0, C_ROPE, max(t, tm))
    q, ckv, kr, krp = _proj_cq(xf, pw['w_in_c'], pw['g_q_c'], pw['g_kv_c'], cos_c, sin_c, pw['w_q_b_c'])
    q = q.reshape(b, t, C_HEADS * C_QK)
    if past is None:
        kt_c, krt_c, v_c = _kv_c(ckv, krp, pw['w_k_t_c'], pw['w_v_c'], b, min(MLA_LONG_BLOCK, t))
        attn_c = _mla_attn(q, kt_c, krt_c, v_c.reshape(b, t, C_HEADS * C_V), pos0=pos0)
    else:
        attn_c = _mla_absorbed(q, ckv.reshape(b, t, C_KV_RANK), kr.reshape(b, t, C_ROPE), past[4], past[5],
                               pw['w_k_t_c'], pw['w_v_c'], pos0=pos0)
    xf, xb = _outproj_ln([attn_c.reshape(n, C_HEADS * C_V)], pw['w_out_c'], xf, pw['ln1_g'][1], pw['ln1_b'][1])
    xf = _channel_mix(xf, xb, p_all, pw, 1)

    heads4 = lambda a, hh: a.reshape(1, b, a.shape[1], hh, HEAD_DIM)
    return (xf.reshape(b, t, d), heads4(ak, A_KV_HEADS), heads4(av, A_KV_HEADS), heads4(bk, B_HEADS),
            heads4(bv, B_HEADS), ckv.reshape(1, b, t, C_KV_RANK), kr.reshape(1, b, t, C_ROPE))


def kernel(x_prompt, x_sample, cache_a_k, cache_a_v, cache_b_k, cache_b_v, cache_c_kv, cache_c_krope, p_prompt,
           p_sample, w_in_ab, sinks_a, rel_bias_b, w_out_ab, w_in_c, g_q_c, w_q_b_c, g_kv_c, w_kv_b_c, w_out_c,
           ln1_g, ln1_b, ln2_g, ln2_b, w_mlp_up, w_mlp_down, w_ple_gate, b_ple_gate, w_ple):
    pw = _prepare_weights({
        'w_in_ab': w_in_ab, 'sinks_a': sinks_a, 'rel_bias_b': rel_bias_b, 'w_out_ab': w_out_ab,
        'w_in_c': w_in_c, 'g_q_c': g_q_c, 'w_q_b_c': w_q_b_c, 'g_kv_c': g_kv_c, 'w_kv_b_c': w_kv_b_c,
        'w_out_c': w_out_c, 'ln1_g': ln1_g, 'ln1_b': ln1_b, 'ln2_g': ln2_g, 'ln2_b': ln2_b,
        'w_mlp_up': w_mlp_up, 'w_mlp_down': w_mlp_down, 'w_ple_gate': w_ple_gate, 'b_ple_gate': b_ple_gate,
        'w_ple': w_ple,
    })
    prompt = _trunk(x_prompt, p_prompt, 0, None, pw)
    past = (cache_a_k[0], cache_a_v[0], cache_b_k[0], cache_b_v[0], cache_c_kv, cache_c_krope)
    sample = _trunk(x_sample, p_sample, cache_c_kv.shape[2], past, pw)
    return (prompt[0], sample[0]) + prompt[1:] + sample[1:]
```
